```python
import jax, jax.numpy as jnp
from jax import lax
import numpy as np

D_MODEL = 2048
BATCH = 8
SEQ = 4096
DEPTH = 4

GRID_W = 64
CTX_LEN = 256
N_BRANCH = 4
BRANCH_W = D_MODEL // N_BRANCH
CHUNK = 128
A_GROUPS = 4
A_GW = BRANCH_W // A_GROUPS
B_HD = 64
B_QH = BRANCH_W // B_HD
B_KVH = B_QH // 4
B_WIN = 128
B_BLK = 128
C_GROUPS = 4
C_GW = BRANCH_W // C_GROUPS
D_HD = 32
D_H = BRANCH_W // D_HD
NA_ROWS = 8
NA_COLS = 16
ROPE_BASE = 10000.0
EPS = 1e-6
NEG = -1e30

PARTS = ("a_u", "a_v", "a_z",
         "b_q", "b_k", "b_v", "b_z",
         "f_in", "f_z",
         "d_q", "d_k", "d_v", "d_z")
IN_SIZES = (BRANCH_W, BRANCH_W, BRANCH_W,
            B_QH * B_HD, B_KVH * B_HD, B_KVH * B_HD, BRANCH_W,
            BRANCH_W, BRANCH_W,
            BRANCH_W, BRANCH_W, BRANCH_W, BRANCH_W)
IN_WIDTH = sum(IN_SIZES)
CTX_KV_PARTS = ("b_k", "b_v", "d_k", "d_v")

kernel_name = "hybrid_gated_parallel_mixer_dit"


def rmsnorm(x, g):
    xf = x.astype(jnp.float32)
    y = xf * lax.rsqrt(jnp.mean(xf * xf, axis=-1, keepdims=True) + EPS)
    return (y * g.astype(jnp.float32)).astype(x.dtype)


def heads(t, n):
    return t.reshape(t.shape[:-1] + (n, -1))


def _rope_1d(x, pos):
    half = x.shape[-1] // 2
    inv = ROPE_BASE ** (-jnp.arange(half, dtype=jnp.float32) / half)
    ang = pos.astype(jnp.float32)[:, None] * inv[None, :]
    cos = jnp.cos(ang)[None, :, None, :]
    sin = jnp.sin(ang)[None, :, None, :]
    xf = x.astype(jnp.float32)
    x1, x2 = xf[..., :half], xf[..., half:]
    return jnp.concatenate([x1 * cos - x2 * sin, x2 * cos + x1 * sin], axis=-1).astype(x.dtype)


def rope_2d(x, pos_r, pos_c):
    d = x.shape[-1] // 2
    return jnp.concatenate([_rope_1d(x[..., :d], pos_r), _rope_1d(x[..., d:], pos_c)], axis=-1)


def project_all(n, w):
    z = n @ w
    offs = [int(o) for o in np.cumsum((0,) + IN_SIZES)]
    return {name: z[..., offs[i]:offs[i + 1]] for i, name in enumerate(PARTS)}


def project_some(n, w, names):
    offs = [int(o) for o in np.cumsum((0,) + IN_SIZES)]
    return {name: n @ w[:, offs[i]:offs[i + 1]] for i, name in enumerate(PARTS) if name in names}


def chunk_mlp(u, v, g_v, w_s, b_s):
    Bn, L, _ = u.shape
    u = jax.nn.gelu(u)
    v = rmsnorm(jax.nn.gelu(v), g_v)
    vb = v.reshape(Bn, L // CHUNK, CHUNK, A_GROUPS, A_GW)
    mixed = jnp.einsum('gts,bnsgc->bntgc', w_s, vb) + b_s.T[None, None, :, :, None]
    return u * mixed.reshape(Bn, L, BRANCH_W)


def fourier_mix(xf_in, w_f):
    Bn, L, _ = xf_in.shape
    xg = xf_in.astype(jnp.float32).reshape(Bn, L, C_GROUPS, C_GW)
    f = jnp.real(jnp.fft.fft2(xg, axes=(1, 3), norm="ortho")).astype(xf_in.dtype)
    return jnp.einsum('blgc,gcd->blgd', f, w_f).reshape(Bn, L, BRANCH_W)


def dense_attn(q, k, v, sink):
    Bn, Lq, Hq, hd = q.shape
    Lk, Hk = k.shape[1], k.shape[2]
    G = Hq // Hk
    qg = q.reshape(Bn, Lq, Hk, G, hd)
    s = jnp.einsum('bqkgd,bjkd->bkgqj', qg, k).astype(jnp.float32) * (hd ** -0.5)
    if sink is not None:
        sk = jnp.broadcast_to(sink.astype(jnp.float32).reshape(1, Hk, G, 1, 1), (Bn, Hk, G, Lq, 1))
        s = jnp.concatenate([s, sk], axis=-1)
    p = jax.nn.softmax(s, axis=-1)[..., :Lk].astype(v.dtype)
    o = jnp.einsum('bkgqj,bjkd->bqkgd', p, v)
    return o.reshape(Bn, Lq, Hq * hd)


def window_gqa(q, k, v, kc, vc, sink):
    Bn, S, Hq, hd = q.shape
    Hk = k.shape[2]
    G = Hq // Hk
    C = kc.shape[1]
    nb = S // B_BLK
    nl = 3 * B_BLK
    scale = hd ** -0.5
    pad = ((0, 0), (B_BLK, B_BLK), (0, 0), (0, 0))
    kp, vp = jnp.pad(k, pad), jnp.pad(v, pad)
    sink_f = sink.astype(jnp.float32).reshape(1, Hk, G, 1, 1)

    def block(i):
        start = i * B_BLK
        qi = lax.dynamic_slice_in_dim(q, start, B_BLK, axis=1).reshape(Bn, B_BLK, Hk, G, hd)
        ki = lax.dynamic_slice_in_dim(kp, start, nl, axis=1)
        vi = lax.dynamic_slice_in_dim(vp, start, nl, axis=1)
        qpos = start + jnp.arange(B_BLK)
        kpos = start - B_BLK + jnp.arange(nl)
        valid = (jnp.abs(kpos[None, :] - qpos[:, None]) <= B_WIN) & (kpos[None, :] >= 0) & (kpos[None, :] < S)
        s_loc = jnp.einsum('bqkgd,bjkd->bkgqj', qi, ki).astype(jnp.float32) * scale
        s_loc = jnp.where(valid, s_loc, NEG)
        s_ctx = jnp.einsum('bqkgd,bckd->bkgqc', qi, kc).astype(jnp.float32) * scale
        sk = jnp.broadcast_to(sink_f, s_loc.shape[:-1] + (1,))
        p = jax.nn.softmax(jnp.concatenate([s_loc, s_ctx, sk], axis=-1), axis=-1).astype(v.dtype)
        o = (jnp.einsum('bkgqj,bjkd->bqkgd', p[..., :nl], vi)
             + jnp.einsum('bkgqc,bckd->bqkgd', p[..., nl:nl + C], vc))
        return o.reshape(Bn, B_BLK, Hq * hd)

    out = lax.map(block, jnp.arange(nb))
    return out.transpose(1, 0, 2, 3).reshape(Bn, S, Hq * hd)


def neighborhood_attn(q, k, v, kc, vc, rpb):
    Bn, S, H, hd = q.shape
    rows = S // GRID_W
    wr = min(NA_ROWS, rows)
    nk = wr * NA_COLS
    scale = hd ** -0.5
    qg = q.reshape(Bn, rows, GRID_W, H, hd)
    kg = k.reshape(Bn, rows, GRID_W, H, hd)
    vg = v.reshape(Bn, rows, GRID_W, H, hd)
    row_start = jnp.clip(jnp.arange(rows) - wr // 2, 0, rows - wr)
    c_idx = jnp.arange(GRID_W)
    col_start = jnp.clip(c_idx - NA_COLS // 2, 0, GRID_W - NA_COLS)
    col_win = col_start[:, None] + jnp.arange(NA_COLS)[None, :]
    dc = col_win - c_idx[:, None] + (NA_COLS - 1)

    def row_block(r):
        r0 = row_start[r]
        krow = lax.dynamic_slice_in_dim(kg, r0, wr, axis=1)
        vrow = lax.dynamic_slice_in_dim(vg, r0, wr, axis=1)
        kn = krow[:, :, col_win]
        vn = vrow[:, :, col_win]
        qr = lax.dynamic_index_in_dim(qg, r, axis=1, keepdims=False)
        dr = r0 + jnp.arange(wr) - r + (NA_ROWS - 1)
        bias = rpb[:, dr[:, None, None], dc[None, :, :]].transpose(0, 2, 1, 3)
        s_loc = jnp.einsum('bqhd,brqjhd->bhqrj', qr, kn).astype(jnp.float32) * scale
        s_loc = s_loc + bias.astype(jnp.float32)[None]
        s_ctx = jnp.einsum('bqhd,bchd->bhqc', qr, kc).astype(jnp.float32) * scale
        logits = jnp.concatenate([s_loc.reshape(Bn, H, GRID_W, nk), s_ctx], axis=-1)
        p = jax.nn.softmax(logits, axis=-1).astype(v.dtype)
        p_loc = p[..., :nk].reshape(Bn, H, GRID_W, wr, NA_COLS)
        o = (jnp.einsum('bhqrj,brqjhd->bqhd', p_loc, vn)
             + jnp.einsum('bhqc,bchd->bqhd', p[..., nk:], vc))
        return o

    out = lax.map(row_block, jnp.arange(rows))
    return out.transpose(1, 0, 2, 3, 4).reshape(Bn, S, H * hd)


def merge_branches(n, ys, zs, w_gate, w_branch, w_out):
    acc = jax.nn.sigmoid(n @ w_gate[0]) * ((ys[0] * jax.nn.silu(zs[0])) @ w_branch[0])
    for b in range(1, N_BRANCH):
        acc = acc + jax.nn.sigmoid(n @ w_gate[b]) * ((ys[b] * jax.nn.silu(zs[b])) @ w_branch[b])
    return acc @ w_out


def setup_inputs(seed: int = 0) -> dict:
    key = jax.random.key(seed)
    ks = jax.random.split(key, 24)
    D = D_MODEL

    def nrm(k, shape, s):
        return jax.random.normal(k, shape, jnp.float32) * s

    return {
        "x": nrm(ks[0], (BATCH, SEQ, D), 1.0),
        "c": nrm(ks[1], (BATCH, D), 1.0),
        "ctx": nrm(ks[2], (BATCH, CTX_LEN, D), 1.0),
        "c_ctx": nrm(ks[3], (D,), 1.0),
        "norm_g": 1.0 + nrm(ks[4], (DEPTH, D), 0.05),
        "w_ada": nrm(ks[5], (DEPTH, D, 3 * D), 0.5 * D ** -0.5),
        "b_ada": nrm(ks[6], (DEPTH, 3 * D), 0.02),
        "w_in": nrm(ks[7], (DEPTH, D, IN_WIDTH), D ** -0.5),
        "a_norm_g": 1.0 + nrm(ks[8], (DEPTH, BRANCH_W), 0.05),
        "a_w_s": nrm(ks[9], (DEPTH, A_GROUPS, CHUNK, CHUNK), CHUNK ** -0.5),
        "a_b_s": 1.0 + nrm(ks[10], (DEPTH, A_GROUPS, CHUNK), 0.1),
        "b_q_g": 1.0 + nrm(ks[11], (DEPTH, B_HD), 0.05),
        "b_k_g": 1.0 + nrm(ks[12], (DEPTH, B_HD), 0.05),
        "b_sink": nrm(ks[13], (DEPTH, B_QH), 0.5),
        "c_w_f": nrm(ks[14], (DEPTH, C_GROUPS, C_GW, C_GW), C_GW ** -0.5),
        "d_q_g": 1.0 + nrm(ks[15], (DEPTH, D_HD), 0.05),
        "d_k_g": 1.0 + nrm(ks[16], (DEPTH, D_HD), 0.05),
        "d_rpb": nrm(ks[17], (DEPTH, D_H, 2 * NA_ROWS - 1, 2 * NA_COLS - 1), 0.1),
        "w_gate": nrm(ks[18], (DEPTH, N_BRANCH, D, D), D ** -0.5),
        "w_branch": nrm(ks[19], (DEPTH, N_BRANCH, BRANCH_W, D), BRANCH_W ** -0.5),
        "w_out": nrm(ks[20], (DEPTH, D, D), D ** -0.5),
    }


def reference(x, c, ctx, c_ctx, norm_g, w_ada, b_ada, w_in, a_norm_g, a_w_s, a_b_s,
              b_q_g, b_k_g, b_sink, c_w_f, d_q_g, d_k_g, d_rpb, w_gate, w_branch, w_out):
    S = x.shape[1]
    t = jnp.arange(S)
    pos_r, pos_c = t // GRID_W, t % GRID_W
    h, hc = x, ctx
    for l in range(DEPTH):
        last = l == DEPTH - 1
        m = jax.nn.silu(c) @ w_ada[l] + b_ada[l]
        mc = jax.nn.silu(c_ctx) @ w_ada[l] + b_ada[l]
        sh, sc, gt = jnp.split(m[:, None, :], 3, axis=-1)
        shc, scc, gtc = jnp.split(mc, 3, axis=-1)
        n = rmsnorm(h, norm_g[l]) * (1.0 + sc) + sh
        nc = rmsnorm(hc, norm_g[l]) * (1.0 + scc) + shc

        z = project_all(n, w_in[l])
        zc = project_some(nc, w_in[l], CTX_KV_PARTS) if last else project_all(nc, w_in[l])

        kBc = rmsnorm(heads(zc["b_k"], B_KVH), b_k_g[l])
        vBc = heads(zc["b_v"], B_KVH)
        kDc = rmsnorm(heads(zc["d_k"], D_H), d_k_g[l])
        vDc = heads(zc["d_v"], D_H)

        qB = rope_2d(rmsnorm(heads(z["b_q"], B_QH), b_q_g[l]), pos_r, pos_c)
        kB = rope_2d(rmsnorm(heads(z["b_k"], B_KVH), b_k_g[l]), pos_r, pos_c)
        vB = heads(z["b_v"], B_KVH)
        qD = rmsnorm(heads(z["d_q"], D_H), d_q_g[l])
        kD = rmsnorm(heads(z["d_k"], D_H), d_k_g[l])
        vD = heads(z["d_v"], D_H)
        y_lat = [chunk_mlp(z["a_u"], z["a_v"], a_norm_g[l], a_w_s[l], a_b_s[l]),
                 window_gqa(qB, kB, vB, kBc, vBc, b_sink[l]),
                 fourier_mix(z["f_in"], c_w_f[l]),
                 neighborhood_attn(qD, kD, vD, kDc, vDc, d_rpb[l])]
        g_lat = [z["a_z"], z["b_z"], z["f_z"], z["d_z"]]
        h_new = h + gt * merge_branches(n, y_lat, g_lat, w_gate[l], w_branch[l], w_out[l])

        if not last:
            qBc = rmsnorm(heads(zc["b_q"], B_QH), b_q_g[l])
            qDc = rmsnorm(heads(zc["d_q"], D_H), d_q_g[l])
            y_ctx = [chunk_mlp(zc["a_u"], zc["a_v"], a_norm_g[l], a_w_s[l], a_b_s[l]),
                     dense_attn(qBc, kBc, vBc, b_sink[l]),
                     fourier_mix(zc["f_in"], c_w_f[l]),
                     dense_attn(qDc, kDc, vDc, None)]
            g_ctx = [zc["a_z"], zc["b_z"], zc["f_z"], zc["d_z"]]
            hc = hc + gtc * merge_branches(nc, y_ctx, g_ctx, w_gate[l], w_branch[l], w_out[l])
        h = h_new
    return h
```

```python
import functools
import math

import jax
import jax.numpy as jnp
import numpy as np
from jax import lax
from jax.experimental import pallas as pl
from jax.experimental.pallas import tpu as pltpu

F32 = jnp.float32
BF16 = jnp.bfloat16

LANES = 128
V7X_VMEM_LIMIT_BYTES = 56 * 1024 * 1024

GRID_W = 64
N_BRANCH = 4
BRANCH_W = 512
CHUNK = 128
A_GROUPS = 4
B_HD = 64
B_QH = 8
B_KVH = 2
B_WIN = 128
B_BLK = 128
C_GROUPS = 4
C_GW = 128
D_HD = 32
D_H = 16
NA_ROWS = 8
NA_COLS = 16
ROPE_BASE = 10000.0
EPS = 1e-6
NEG = -1e30

Z_CHUNK = 3 * BRANCH_W
(COL_AU, COL_AV, COL_AZ, COL_BQ, COL_BKV, COL_BZ,
 COL_FIN, COL_FZ, COL_DQ, COL_DK, COL_DV, COL_DZ) = range(12)
Z_WIDTH = 12 * BRANCH_W


def _cparams(n_grid):
    return pltpu.CompilerParams(
        dimension_semantics=("arbitrary",) * n_grid,
        vmem_limit_bytes=V7X_VMEM_LIMIT_BYTES)


def _sigmoid(x):
    return 1.0 / (1.0 + jnp.exp(-x))


def _silu(x):
    return x * _sigmoid(x)


def _gelu(x):
    return 0.5 * x * (1.0 + jnp.tanh(math.sqrt(2.0 / math.pi) * (x + 0.044715 * (x * x * x))))


def _mod_kernel(c_ref, w_ref, b_ref, o_ref):
    c = c_ref[...]
    a = _silu(c).astype(BF16)
    o_ref[...] = jnp.dot(a, w_ref[...].astype(BF16), preferred_element_type=F32) + b_ref[...]


def _modulation(cc, w_ada, b_ada):
    L, D, N = w_ada.shape
    R = cc.shape[0]
    tn = 512
    return pl.pallas_call(
        _mod_kernel,
        grid=(L, N // tn),
        in_specs=[pl.BlockSpec((R, D), lambda l, j: (0, 0)),
                  pl.BlockSpec((None, D, tn), lambda l, j: (l, 0, j)),
                  pl.BlockSpec((None, 1, tn), lambda l, j: (l, 0, j))],
        out_specs=pl.BlockSpec((None, R, tn), lambda l, j: (l, 0, j)),
        out_shape=jax.ShapeDtypeStruct((L, R, N), F32),
        compiler_params=_cparams(2),
        name="modulation",
    )(cc, w_ada, b_ada.reshape(L, 1, N))


def _head_rms(x, bd_ref, width):
    sq = x * x
    hi = sq.astype(BF16)
    lo = (sq - hi.astype(F32)).astype(BF16)
    bd = bd_ref[0:width, 0:width]
    return (jnp.dot(hi, bd, preferred_element_type=F32)
            + jnp.dot(lo, bd, preferred_element_type=F32))


def _rope(x, cos, sina, sinb):
    w = x.shape[-1]
    reps = w // LANES
    cos = jnp.concatenate([cos] * reps, axis=1)
    sina = jnp.concatenate([sina] * reps, axis=1)
    sinb = jnp.concatenate([sinb] * reps, axis=1)
    up = pltpu.roll(x, w - 16, 1)
    dn = pltpu.roll(x, 16, 1)
    return x * cos + up * sina + dn * sinb


def _inproj_kernel(h_ref, mod_ref, ng_ref, w_ref, gains_ref, bd64_ref, bd32_ref,
                   cos_ref, sina_ref, sinb_ref, n_ref, z_ref, nscr, *, rope):
    D = h_ref.shape[-1]
    j = pl.program_id(2)

    @pl.when(j == 0)
    def _():
        x = h_ref[...]
        ms = jnp.mean(x * x, axis=-1, keepdims=True)
        y = x * lax.rsqrt(ms + EPS) * ng_ref[...]
        nn = y * (1.0 + mod_ref[:, D:2 * D]) + mod_ref[:, 0:D]
        nb = nn.astype(BF16)
        nscr[...] = nb
        n_ref[...] = nb

    acc = jnp.dot(nscr[...], w_ref[...], preferred_element_type=F32)
    W = BRANCH_W

    def maybe_rope(x):
        if rope:
            return _rope(x, cos_ref[...], sina_ref[...], sinb_ref[...])
        return x

    @pl.when(j == 0)
    def _():
        z_ref[...] = acc.astype(BF16)

    @pl.when(j == 1)
    def _():
        q = acc[:, 0:W]
        q = q * lax.rsqrt(_head_rms(q, bd64_ref, W) + EPS) * gains_ref[0:1, :]
        q = maybe_rope(q) * (B_HD ** -0.5)
        k = acc[:, W:W + W // 2]
        k = k * lax.rsqrt(_head_rms(k, bd64_ref, W // 2) + EPS) * gains_ref[1:2, 0:W // 2]
        k = maybe_rope(k)
        z_ref[:, 0:W] = q.astype(BF16)
        z_ref[:, W:W + W // 2] = k.astype(BF16)
        z_ref[:, W + W // 2:] = acc[:, W + W // 2:].astype(BF16)

    @pl.when(j == 2)
    def _():
        q = acc[:, 2 * W:]
        q = q * lax.rsqrt(_head_rms(q, bd32_ref, W) + EPS) * gains_ref[2:3, :] * (D_HD ** -0.5)
        z_ref[:, 0:2 * W] = acc[:, 0:2 * W].astype(BF16)
        z_ref[:, 2 * W:] = q.astype(BF16)

    @pl.when(j == 3)
    def _():
        k = acc[:, 0:W]
        k = k * lax.rsqrt(_head_rms(k, bd32_ref, W) + EPS) * gains_ref[3:4, :]
        z_ref[:, 0:W] = k.astype(BF16)
        z_ref[:, W:] = acc[:, W:].astype(BF16)


def _inproj(h, mod_l, mod_row, ng, w, gains, bd64, bd32, ropes, tm, rope):
    B, S, D = h.shape
    cos, sina, sinb = ropes
    nj = Z_WIDTH // Z_CHUNK
    rope_spec = pl.BlockSpec((tm, LANES), lambda b, i, j: (i, 0))
    return pl.pallas_call(
        functools.partial(_inproj_kernel, rope=rope),
        grid=(B, S // tm, nj),
        in_specs=[pl.BlockSpec((None, tm, D), lambda b, i, j: (b, i, 0)),
                  pl.BlockSpec((None, 1, 3 * D), lambda b, i, j: (mod_row(b), 0, 0)),
                  pl.BlockSpec((1, D), lambda b, i, j: (0, 0)),
                  pl.BlockSpec((D, Z_CHUNK), lambda b, i, j: (0, j)),
                  pl.BlockSpec((8, BRANCH_W), lambda b, i, j: (0, 0)),
                  pl.BlockSpec((BRANCH_W, BRANCH_W), lambda b, i, j: (0, 0)),
                  pl.BlockSpec((BRANCH_W, BRANCH_W), lambda b, i, j: (0, 0)),
                  rope_spec, rope_spec, rope_spec],
        out_specs=[pl.BlockSpec((None, tm, D), lambda b, i, j: (b, i, 0)),
                   pl.BlockSpec((None, tm, Z_CHUNK), lambda b, i, j: (b, i, j))],
        out_shape=[jax.ShapeDtypeStruct((B, S, D), BF16),
                   jax.ShapeDtypeStruct((B, S, Z_WIDTH), BF16)],
        scratch_shapes=[pltpu.VMEM((tm, D), BF16)],
        compiler_params=_cparams(3),
        name="inproj",
    )(h, mod_l, ng, w, gains, bd64, bd32, cos, sina, sinb)


def _branch_a_kernel(au_ref, av_ref, az_ref, g_ref, ws_ref, bs_ref, o_ref):
    tm = au_ref.shape[0]
    v = _gelu(av_ref[...].astype(F32))
    ms = jnp.mean(v * v, axis=-1, keepdims=True)
    vb = (v * lax.rsqrt(ms + EPS) * g_ref[...]).astype(BF16)
    gate = _gelu(au_ref[...].astype(F32)) * _silu(az_ref[...].astype(F32))
    for c in range(tm // CHUNK):
        rows = slice(c * CHUNK, (c + 1) * CHUNK)
        for g in range(A_GROUPS):
            cols = slice(g * LANES, (g + 1) * LANES)
            mixed = jnp.dot(ws_ref[g], vb[rows, cols], preferred_element_type=F32) + bs_ref[:, g:g + 1]
            o_ref[rows, cols] = (gate[rows, cols] * mixed).astype(BF16)


def _branch_a(z, g_v, w_s, b_sT, tm):
    B, S, _ = z.shape
    W = BRANCH_W
    col = lambda cb: pl.BlockSpec((None, tm, W), lambda b, i: (b, i, cb))
    return pl.pallas_call(
        _branch_a_kernel,
        grid=(B, S // tm),
        in_specs=[col(COL_AU), col(COL_AV), col(COL_AZ),
                  pl.BlockSpec((1, W), lambda b, i: (0, 0)),
                  pl.BlockSpec((A_GROUPS, CHUNK, CHUNK), lambda b, i: (0, 0, 0)),
                  pl.BlockSpec((CHUNK, A_GROUPS), lambda b, i: (0, 0))],
        out_specs=pl.BlockSpec((None, tm, W), lambda b, i: (b, i, 0)),
        out_shape=jax.ShapeDtypeStruct((B, S, W), BF16),
        compiler_params=_cparams(2),
        name="branch_a",
    )(z, z, z, g_v, w_s, b_sT)


def _dftw_kernel(c_ref, s_ref, w_ref, cw_ref, sw_ref):
    for g in range(C_GROUPS):
        w = w_ref[g]
        cw_ref[g] = jnp.dot(c_ref[...], w, preferred_element_type=F32,
                            precision=lax.Precision.HIGHEST).astype(BF16)
        sw_ref[g] = jnp.dot(s_ref[...], w, preferred_element_type=F32,
                            precision=lax.Precision.HIGHEST).astype(BF16)


def _dft_weights(c128, s128, w_f):
    shp = jax.ShapeDtypeStruct((C_GROUPS, C_GW, C_GW), BF16)
    return pl.pallas_call(_dftw_kernel, out_shape=[shp, shp], name="dft_weights")(c128, s128, w_f)


def _fourier_pq_kernel(x_ref, cw_ref, sw_ref, pq_ref):
    for g in range(C_GROUPS):
        cols = slice(g * LANES, (g + 1) * LANES)
        x = x_ref[:, cols]
        pq_ref[0, :, cols] = jnp.dot(x, cw_ref[g], preferred_element_type=F32).astype(BF16)
        pq_ref[1, :, cols] = jnp.dot(x, sw_ref[g], preferred_element_type=F32).astype(BF16)


def _fourier_pq(z, cw, sw, tm):
    B, S, _ = z.shape
    W = BRANCH_W
    wspec = pl.BlockSpec((C_GROUPS, C_GW, C_GW), lambda b, i: (0, 0, 0))
    return pl.pallas_call(
        _fourier_pq_kernel,
        grid=(B, S // tm),
        in_specs=[pl.BlockSpec((None, tm, W), lambda b, i: (b, i, COL_FIN)), wspec, wspec],
        out_specs=pl.BlockSpec((None, 2, tm, W), lambda b, i: (b, 0, i, 0)),
        out_shape=jax.ShapeDtypeStruct((B, 2, S, W), BF16),
        compiler_params=_cparams(2),
        name="fourier_pq",
    )(z, cw, sw)


def _fourier_mix_kernel(dft_ref, pq_ref, fz_ref, o_ref, *, scale):
    y = jnp.dot(dft_ref[...], pq_ref[...], preferred_element_type=F32) * scale
    o_ref[...] = (y * _silu(fz_ref[...].astype(F32))).astype(BF16)


def _fourier_mix(dft, pq, z, tm):
    B, S, _ = z.shape
    W = BRANCH_W
    scale = 1.0 / math.sqrt(S * C_GW)
    return pl.pallas_call(
        functools.partial(_fourier_mix_kernel, scale=scale),
        grid=(B, S // tm),
        in_specs=[pl.BlockSpec((tm, 2 * S), lambda b, i: (i, 0)),
                  pl.BlockSpec((None, 2 * S, W), lambda b, i: (b, 0, 0)),
                  pl.BlockSpec((None, tm, W), lambda b, i: (b, i, COL_FZ))],
        out_specs=pl.BlockSpec((None, tm, W), lambda b, i: (b, i, 0)),
        out_shape=jax.ShapeDtypeStruct((B, S, W), BF16),
        compiler_params=_cparams(2),
        name="fourier_mix",
    )(dft, pq, z)


def _lane_masks(n_heads, hd):
    lane = lax.broadcasted_iota(jnp.int32, (1, LANES), 1)
    return [((lane >= g * hd) & (lane < (g + 1) * hd)) for g in range(n_heads)]


def _attend(q, segs, sink_col, n_heads, hd):
    tq = q.shape[0]
    masks = _lane_masks(n_heads, hd)
    zero = jnp.zeros_like(q)
    qx = jnp.concatenate([jnp.where(m, q, zero) for m in masks], axis=0)
    scores = []
    for k, _, bias in segs:
        s = lax.dot_general(qx, k, (((1,), (1,)), ((), ())), preferred_element_type=F32)
        if bias is not None:
            s = s + bias
        scores.append(s)
    m = scores[0].max(axis=-1, keepdims=True)
    for s in scores[1:]:
        m = jnp.maximum(m, s.max(axis=-1, keepdims=True))
    if sink_col is not None:
        m = jnp.maximum(m, sink_col)
    den = None
    acc = None
    for s, (_, v, _) in zip(scores, segs):
        p = jnp.exp(s - m)
        d = p.sum(axis=-1, keepdims=True)
        o = jnp.dot(p.astype(BF16), v, preferred_element_type=F32)
        den = d if den is None else den + d
        acc = o if acc is None else acc + o
    if sink_col is not None:
        den = den + jnp.exp(sink_col - m)
    acc = acc / den
    out = None
    for g, msk in enumerate(masks):
        part = jnp.where(msk, acc[g * tq:(g + 1) * tq], 0.0)
        out = part if out is None else out + part
    return out


def _branch_b_kernel(sink_ref, q_ref, kv_ref, ckv_ref, bz_ref, o_ref, *, seq):
    i = pl.program_id(1)
    nl = 3 * B_BLK
    start = jnp.clip((i - 1) * B_BLK, 0, seq - nl)
    start = pl.multiple_of(start, B_BLK)
    qpos = i * B_BLK + lax.broadcasted_iota(jnp.int32, (B_BLK, nl), 0)
    kpos = start + lax.broadcasted_iota(jnp.int32, (B_BLK, nl), 1)
    bias1 = jnp.where(jnp.abs(kpos - qpos) <= B_WIN, 0.0, NEG).astype(F32)
    bias = jnp.concatenate([bias1, bias1], axis=0)
    half = BRANCH_W // 2
    for grp in range(BRANCH_W // LANES):
        kvh = grp // 2
        kc = slice(kvh * LANES, (kvh + 1) * LANES)
        vc = slice(half + kvh * LANES, half + (kvh + 1) * LANES)
        cols = slice(grp * LANES, (grp + 1) * LANES)
        sink_col = jnp.concatenate(
            [jnp.full((B_BLK, 1), sink_ref[2 * grp + g], F32) for g in range(2)], axis=0)
        segs = [(kv_ref[pl.ds(start, nl), kc], kv_ref[pl.ds(start, nl), vc], bias),
                (ckv_ref[:, kc], ckv_ref[:, vc], None)]
        o = _attend(q_ref[:, cols], segs, sink_col, 2, B_HD)
        o_ref[:, cols] = (o * _silu(bz_ref[:, cols].astype(F32))).astype(BF16)


def _branch_b(z, zc, sink):
    B, S, _ = z.shape
    C = zc.shape[1]
    W = BRANCH_W
    return pl.pallas_call(
        functools.partial(_branch_b_kernel, seq=S),
        grid=(B, S // B_BLK),
        in_specs=[pl.BlockSpec(memory_space=pltpu.SMEM),
                  pl.BlockSpec((None, B_BLK, W), lambda b, i: (b, i, COL_BQ)),
                  pl.BlockSpec((None, S, W), lambda b, i: (b, 0, COL_BKV)),
                  pl.BlockSpec((None, C, W), lambda b, i: (b, 0, COL_BKV)),
                  pl.BlockSpec((None, B_BLK, W), lambda b, i: (b, i, COL_BZ))],
        out_specs=pl.BlockSpec((None, B_BLK, W), lambda b, i: (b, i, 0)),
        out_shape=jax.ShapeDtypeStruct((B, S, W), BF16),
        compiler_params=_cparams(2),
        name="branch_b",
    )(sink, z, z, zc, z)


def _ctx_b_kernel(sink_ref, q_ref, kv_ref, bz_ref, o_ref):
    tq = q_ref.shape[0]
    half = BRANCH_W // 2
    for grp in range(BRANCH_W // LANES):
        kvh = grp // 2
        kc = slice(kvh * LANES, (kvh + 1) * LANES)
        vc = slice(half + kvh * LANES, half + (kvh + 1) * LANES)
        cols = slice(grp * LANES, (grp + 1) * LANES)
        sink_col = jnp.concatenate(
            [jnp.full((tq, 1), sink_ref[2 * grp + g], F32) for g in range(2)], axis=0)
        o = _attend(q_ref[:, cols], [(kv_ref[:, kc], kv_ref[:, vc], None)], sink_col, 2, B_HD)
        o_ref[:, cols] = (o * _silu(bz_ref[:, cols].astype(F32))).astype(BF16)


def _ctx_b(zc, sink):
    B, C, _ = zc.shape
    W = BRANCH_W
    col = lambda cb: pl.BlockSpec((None, C, W), lambda b: (b, 0, cb))
    return pl.pallas_call(
        _ctx_b_kernel,
        grid=(B,),
        in_specs=[pl.BlockSpec(memory_space=pltpu.SMEM), col(COL_BQ), col(COL_BKV), col(COL_BZ)],
        out_specs=pl.BlockSpec((None, C, W), lambda b: (b, 0, 0)),
        out_shape=jax.ShapeDtypeStruct((B, C, W), BF16),
        compiler_params=_cparams(1),
        name="ctx_b",
    )(sink, zc, zc, zc)


def _na_row_start(r, rows):
    return jnp.clip(r - NA_ROWS // 2, 0, rows - NA_ROWS)


def _branch_d_kernel(q_ref, k_ref, v_ref, ck_ref, cv_ref, bias_ref, dz_ref, o_ref, *, rows):
    r = pl.program_id(1)
    nk = NA_ROWS * GRID_W
    start = pl.multiple_of(_na_row_start(r, rows) * GRID_W, GRID_W)
    hpg = LANES // D_HD
    for grp in range(BRANCH_W // LANES):
        cols = slice(grp * LANES, (grp + 1) * LANES)
        bias = bias_ref[grp * hpg:(grp + 1) * hpg].reshape(hpg * GRID_W, nk)
        segs = [(k_ref[pl.ds(start, nk), cols], v_ref[pl.ds(start, nk), cols], bias),
                (ck_ref[:, cols], cv_ref[:, cols], None)]
        o = _attend(q_ref[:, cols], segs, None, hpg, D_HD)
        o_ref[:, cols] = (o * _silu(dz_ref[:, cols].astype(F32))).astype(BF16)


def _branch_d(z, zc, bias_tab):
    B, S, _ = z.shape
    C = zc.shape[1]
    W = BRANCH_W
    rows = S // GRID_W
    nk = NA_ROWS * GRID_W
    tile = lambda cb: pl.BlockSpec((None, GRID_W, W), lambda b, r: (b, r, cb))
    full = lambda cb: pl.BlockSpec((None, S, W), lambda b, r: (b, 0, cb))
    ctx = lambda cb: pl.BlockSpec((None, C, W), lambda b, r: (b, 0, cb))
    return pl.pallas_call(
        functools.partial(_branch_d_kernel, rows=rows),
        grid=(B, rows),
        in_specs=[tile(COL_DQ), full(COL_DK), full(COL_DV), ctx(COL_DK), ctx(COL_DV),
                  pl.BlockSpec((None, D_H, GRID_W, nk),
                               lambda b, r: (r - _na_row_start(r, rows), 0, 0, 0)),
                  tile(COL_DZ)],
        out_specs=pl.BlockSpec((None, GRID_W, W), lambda b, r: (b, r, 0)),
        out_shape=jax.ShapeDtypeStruct((B, S, W), BF16),
        compiler_params=_cparams(2),
        name="branch_d",
    )(z, z, z, zc, zc, bias_tab, z)


def _ctx_d_kernel(q_ref, k_ref, v_ref, dz_ref, o_ref):
    hpg = LANES // D_HD
    for grp in range(BRANCH_W // LANES):
        cols = slice(grp * LANES, (grp + 1) * LANES)
        o = _attend(q_ref[:, cols], [(k_ref[:, cols], v_ref[:, cols], None)], None, hpg, D_HD)
        o_ref[:, cols] = (o * _silu(dz_ref[:, cols].astype(F32))).astype(BF16)


def _ctx_d(zc):
    B, C, _ = zc.shape
    W = BRANCH_W
    col = lambda cb: pl.BlockSpec((None, C, W), lambda b: (b, 0, cb))
    return pl.pallas_call(
        _ctx_d_kernel,
        grid=(B,),
        in_specs=[col(COL_DQ), col(COL_DK), col(COL_DV), col(COL_DZ)],
        out_specs=pl.BlockSpec((None, C, W), lambda b: (b, 0, 0)),
        out_shape=jax.ShapeDtypeStruct((B, C, W), BF16),
        compiler_params=_cparams(1),
        name="ctx_d",
    )(zc, zc, zc, zc)


def _merge_kernel(h_ref, mod_ref, n_ref, ua_ref, ub_ref, uc_ref, ud_ref,
                  wg_ref, wb_ref, wo_ref, o_ref, acc_ref):
    D = h_ref.shape[-1]
    j = pl.program_id(2)

    @pl.when(j == 0)
    def _():
        acc_ref[...] = jnp.zeros_like(acc_ref)

    n = n_ref[...]
    mix = None
    for b, u_ref in enumerate((ua_ref, ub_ref, uc_ref, ud_ref)):
        gate = _sigmoid(jnp.dot(n, wg_ref[b], preferred_element_type=F32))
        t = gate * jnp.dot(u_ref[...], wb_ref[b], preferred_element_type=F32)
        mix = t if mix is None else mix + t
    acc_ref[...] += jnp.dot(mix.astype(BF16), wo_ref[...], preferred_element_type=F32)

    @pl.when(j == pl.num_programs(2) - 1)
    def _():
        o_ref[...] = h_ref[...] + mod_ref[:, 2 * D:3 * D] * acc_ref[...]


def _merge(h, mod_l, mod_row, n, us, wg, wb, wo, tm, cn):
    B, S, D = h.shape
    W = BRANCH_W
    row = lambda width: pl.BlockSpec((None, tm, width), lambda b, i, j: (b, i, 0))
    return pl.pallas_call(
        _merge_kernel,
        grid=(B, S // tm, D // cn),
        in_specs=[row(D),
                  pl.BlockSpec((None, 1, 3 * D), lambda b, i, j: (mod_row(b), 0, 0)),
                  row(D), row(W), row(W), row(W), row(W),
                  pl.BlockSpec((N_BRANCH, D, cn), lambda b, i, j: (0, 0, j)),
                  pl.BlockSpec((N_BRANCH, W, cn), lambda b, i, j: (0, 0, j)),
                  pl.BlockSpec((cn, D), lambda b, i, j: (j, 0))],
        out_specs=row(D),
        out_shape=jax.ShapeDtypeStruct((B, S, D), F32),
        scratch_shapes=[pltpu.VMEM((tm, D), F32)],
        compiler_params=_cparams(3),
        name="merge",
    )(h, mod_l, n, *us, wg, wb, wo)


def _rearrange_w_in(w_in):
    W = BRANCH_W
    o = 0
    parts = {}
    for name, size in (("a_u", W), ("a_v", W), ("a_z", W), ("b_q", W), ("b_k", B_KVH * B_HD),
                       ("b_v", B_KVH * B_HD), ("b_z", W), ("f_in", W), ("f_z", W),
                       ("d_q", W), ("d_k", W), ("d_v", W), ("d_z", W)):
        parts[name] = w_in[:, :, o:o + size]
        o += size

    def dup(p):
        return jnp.concatenate([p[:, :, 0:B_HD], p[:, :, 0:B_HD], p[:, :, B_HD:], p[:, :, B_HD:]], axis=-1)

    cols = [parts["a_u"], parts["a_v"], parts["a_z"], parts["b_q"], dup(parts["b_k"]), dup(parts["b_v"]),
            parts["b_z"], parts["f_in"], parts["f_z"], parts["d_q"], parts["d_k"], parts["d_v"], parts["d_z"]]
    return jnp.concatenate(cols, axis=-1).astype(BF16)


def _rope_tables(S):
    t = np.arange(S)
    pos = np.stack([t // GRID_W, t % GRID_W], axis=1).astype(np.float32)
    lane = np.arange(B_HD)
    which = lane // (B_HD // 2)
    fi = lane % (B_HD // 4)
    inv = (ROPE_BASE ** (-(fi.astype(np.float32)) / (B_HD // 4))).astype(np.float32)
    ang = pos[:, which] * inv[None, :]
    cos, sin = np.cos(ang), np.sin(ang)
    lower = (lane % (B_HD // 2)) < (B_HD // 4)
    sina = np.where(lower[None, :], -sin, 0.0)
    sinb = np.where(lower[None, :], 0.0, sin)
    rep = lambda a: jnp.asarray(np.tile(a.astype(np.float32), (1, LANES // B_HD)))
    return rep(cos), rep(sina), rep(sinb)


def _block_diag_mean(width, hd):
    i = np.arange(width)
    return jnp.asarray(((i[:, None] // hd) == (i[None, :] // hd)).astype(np.float32) / hd, dtype=BF16)


def _dft_cos_sin(n):
    k = jnp.arange(n, dtype=jnp.int32)
    m = (k[:, None] * k[None, :]) % n
    ang = m.astype(F32) * (2.0 * math.pi / n)
    return jnp.cos(ang), jnp.sin(ang)


def _dft_matrix(n):
    c, s = _dft_cos_sin(n)
    return jnp.concatenate([c, -s], axis=1).astype(BF16)


def _na_bias_table(rpb):
    c = np.arange(GRID_W)
    c0 = np.clip(c - NA_COLS // 2, 0, GRID_W - NA_COLS)
    kc = np.arange(GRID_W)
    inwin = (kc[None, :] >= c0[:, None]) & (kc[None, :] < c0[:, None] + NA_COLS)
    dc = np.clip(kc[None, :] - c[:, None] + (NA_COLS - 1), 0, 2 * NA_COLS - 2)
    delta = np.arange(NA_ROWS)
    i = np.arange(NA_ROWS)
    dr = i[None, :] - delta[:, None] + (NA_ROWS - 1)
    g = rpb[:, dr[:, :, None, None], dc[None, None, :, :]]
    g = jnp.where(jnp.asarray(inwin)[None, None, None], g, NEG)
    g = g.transpose(1, 0, 3, 2, 4)
    return g.reshape(NA_ROWS, g.shape[1], GRID_W, NA_ROWS * GRID_W).astype(F32)


def kernel(x, c, ctx, c_ctx, norm_g, w_ada, b_ada, w_in, a_norm_g, a_w_s, a_b_s, b_q_g, b_k_g, b_sink,
           c_w_f, d_q_g, d_k_g, d_rpb, w_gate, w_branch, w_out):
    B, S, D = x.shape
    C = ctx.shape[1]
    L = norm_g.shape[0]
    assert S % 512 == 0 and S // GRID_W >= NA_ROWS and S >= 3 * B_BLK and C % CHUNK == 0
    tm, tmc = 512, C
    W = BRANCH_W

    n_rows = -(-(B + 1) // 8) * 8
    cc = jnp.concatenate([c, c_ctx[None], jnp.zeros((n_rows - B - 1, D), F32)], axis=0)
    mod = _modulation(cc, w_ada, b_ada).reshape(L, n_rows, 1, 3 * D)
    lat_row = lambda b: b
    ctx_row = lambda b: B

    w_in_r = _rearrange_w_in(w_in)
    w_gate_b, w_branch_b, w_out_b = w_gate.astype(BF16), w_branch.astype(BF16), w_out.astype(BF16)
    a_ws_b = a_w_s.astype(BF16)
    gains = jnp.stack([jnp.tile(b_q_g, (1, W // B_HD)),
                       jnp.tile(b_k_g, (1, W // B_HD)),
                       jnp.tile(d_q_g, (1, W // D_HD)),
                       jnp.tile(d_k_g, (1, W // D_HD))] + [jnp.zeros((L, W), F32)] * 4, axis=1)
    bd64, bd32 = _block_diag_mean(W, B_HD), _block_diag_mean(W, D_HD)
    ropes = _rope_tables(S)
    ropes_c = tuple(t[:C] for t in ropes)
    c128, s128 = _dft_cos_sin(C_GW)
    dft_lat, dft_ctx = _dft_matrix(S), _dft_matrix(C)

    h, hc = x, ctx
    for l in range(L):
        last = l == L - 1
        mod_l = mod[l]
        ng = norm_g[l][None]
        n_lat, z = _inproj(h, mod_l, lat_row, ng, w_in_r[l], gains[l], bd64, bd32, ropes, tm, True)
        n_ctx, zc = _inproj(hc, mod_l, ctx_row, ng, w_in_r[l], gains[l], bd64, bd32, ropes_c, tmc, False)
        cw, sw = _dft_weights(c128, s128, c_w_f[l])
        g_v = a_norm_g[l][None]
        b_sT = a_b_s[l].T
        bias_tab = _na_bias_table(d_rpb[l])

        u_lat = [
            _branch_a(z, g_v, a_ws_b[l], b_sT, tm),
            _branch_b(z, zc, b_sink[l]),
            _fourier_mix(dft_lat, _fourier_pq(z, cw, sw, tm).reshape(B, 2 * S, W), z, tm),
            _branch_d(z, zc, bias_tab),
        ]
        h_new = _merge(h, mod_l, lat_row, n_lat, u_lat, w_gate_b[l], w_branch_b[l], w_out_b[l], tm, 256)
        if not last:
            u_ctx = [
                _branch_a(zc, g_v, a_ws_b[l], b_sT, tmc),
                _ctx_b(zc, b_sink[l]),
                _fourier_mix(dft_ctx, _fourier_pq(zc, cw, sw, tmc).reshape(B, 2 * C, W), zc, tmc),
                _ctx_d(zc),
            ]
            hc = _merge(hc, mod_l, ctx_row, n_ctx, u_ctx, w_gate_b[l], w_branch_b[l], w_out_b[l], tmc, 256)
        h = h_new
    return h
```

```python
import functools
import math

import jax
import jax.numpy as jnp
import numpy as np
from jax import lax
from jax.experimental import pallas as pl
from jax.experimental.pallas import tpu as pltpu

F32 = jnp.float32
BF16 = jnp.bfloat16

LANES = 128
V7X_VMEM_LIMIT_BYTES = 56 * 1024 * 1024

GRID_W = 64
N_BRANCH = 4
BRANCH_W = 512
CHUNK = 128
A_GROUPS = 4
B_HD = 64
B_QH = 8
B_KVH = 2
B_WIN = 128
B_BLK = 128
C_GROUPS = 4
C_GW = 128
D_HD = 32
D_H = 16
NA_ROWS = 8
NA_COLS = 16
ROPE_BASE = 10000.0
EPS = 1e-6
NEG = -1e30

Z_CHUNK = 3 * BRANCH_W
(COL_AU, COL_AV, COL_AZ, COL_BQ, COL_BKV, COL_BZ,
 COL_FIN, COL_FZ, COL_DQ, COL_DK, COL_DV, COL_DZ) = range(12)
Z_WIDTH = 12 * BRANCH_W


def _cparams(n_grid):
    return pltpu.CompilerParams(
        dimension_semantics=("arbitrary",) * n_grid,
        vmem_limit_bytes=V7X_VMEM_LIMIT_BYTES)


def _sigmoid(x):
    return 1.0 / (1.0 + jnp.exp(-x))


def _silu(x):
    return x * _sigmoid(x)


def _gelu(x):
    return 0.5 * x * (1.0 + jnp.tanh(math.sqrt(2.0 / math.pi) * (x + 0.044715 * (x * x * x))))


def _mod_kernel(c_ref, w_ref, b_ref, o_ref):
    c = c_ref[...]
    a = _silu(c).astype(BF16)
    o_ref[...] = jnp.dot(a, w_ref[...].astype(BF16), preferred_element_type=F32) + b_ref[...]


def _modulation(cc, w_ada, b_ada):
    L, D, N = w_ada.shape
    R = cc.shape[0]
    tn = 512
    return pl.pallas_call(
        _mod_kernel,
        grid=(L, N // tn),
        in_specs=[pl.BlockSpec((R, D), lambda l, j: (0, 0)),
                  pl.BlockSpec((None, D, tn), lambda l, j: (l, 0, j)),
                  pl.BlockSpec((None, 1, tn), lambda l, j: (l, 0, j))],
        out_specs=pl.BlockSpec((None, R, tn), lambda l, j: (l, 0, j)),
        out_shape=jax.ShapeDtypeStruct((L, R, N), F32),
        compiler_params=_cparams(2),
        name="modulation",
    )(cc, w_ada, b_ada.reshape(L, 1, N))


def _head_rms(x, bd_ref, width):
    sq = x * x
    hi = sq.astype(BF16)
    lo = (sq - hi.astype(F32)).astype(BF16)
    bd = bd_ref[0:width, 0:width]
    return (jnp.dot(hi, bd, preferred_element_type=F32)
            + jnp.dot(lo, bd, preferred_element_type=F32))


def _rope(x, cos, sina, sinb):
    w = x.shape[-1]
    reps = w // LANES
    cos = jnp.concatenate([cos] * reps, axis=1)
    sina = jnp.concatenate([sina] * reps, axis=1)
    sinb = jnp.concatenate([sinb] * reps, axis=1)
    up = pltpu.roll(x, w - 16, 1)
    dn = pltpu.roll(x, 16, 1)
    return x * cos + up * sina + dn * sinb


def _inproj_kernel(h_ref, mod_ref, ng_ref, w_ref, gains_ref, bd64_ref, bd32_ref,
                   cos_ref, sina_ref, sinb_ref, n_ref, z_ref, nscr, *, rope):
    D = h_ref.shape[-1]
    j = pl.program_id(2)

    @pl.when(j == 0)
    def _():
        x = h_ref[...]
        ms = jnp.mean(x * x, axis=-1, keepdims=True)
        y = x * lax.rsqrt(ms + EPS) * ng_ref[...]
        nn = y * (1.0 + mod_ref[:, D:2 * D]) + mod_ref[:, 0:D]
        nb = nn.astype(BF16)
        nscr[...] = nb
        n_ref[...] = nb

    acc = jnp.dot(nscr[...], w_ref[...], preferred_element_type=F32)
    W = BRANCH_W

    def maybe_rope(x):
        if rope:
            return _rope(x, cos_ref[...], sina_ref[...], sinb_ref[...])
        return x

    @pl.when(j == 0)
    def _():
        z_ref[...] = acc.astype(BF16)

    @pl.when(j == 1)
    def _():
        q = acc[:, 0:W]
        q = q * lax.rsqrt(_head_rms(q, bd64_ref, W) + EPS) * gains_ref[0:1, :]
        q = maybe_rope(q) * (B_HD ** -0.5)
        k = acc[:, W:W + W // 2]
        k = k * lax.rsqrt(_head_rms(k, bd64_ref, W // 2) + EPS) * gains_ref[1:2, 0:W // 2]
        k = maybe_rope(k)
        z_ref[:, 0:W] = q.astype(BF16)
        z_ref[:, W:W + W // 2] = k.astype(BF16)
        z_ref[:, W + W // 2:] = acc[:, W + W // 2:].astype(BF16)

    @pl.when(j == 2)
    def _():
        q = acc[:, 2 * W:]
        q = q * lax.rsqrt(_head_rms(q, bd32_ref, W) + EPS) * gains_ref[2:3, :] * (D_HD ** -0.5)
        z_ref[:, 0:2 * W] = acc[:, 0:2 * W].astype(BF16)
        z_ref[:, 2 * W:] = q.astype(BF16)

    @pl.when(j == 3)
    def _():
        k = acc[:, 0:W]
        k = k * lax.rsqrt(_head_rms(k, bd32_ref, W) + EPS) * gains_ref[3:4, :]
        z_ref[:, 0:W] = k.astype(BF16)
        z_ref[:, W:] = acc[:, W:].astype(BF16)


def _inproj(h, mod_l, mod_row, ng, w, gains, bd64, bd32, ropes, tm, rope):
    B, S, D = h.shape
    cos, sina, sinb = ropes
    nj = Z_WIDTH // Z_CHUNK
    rope_spec = pl.BlockSpec((tm, LANES), lambda b, i, j: (i, 0))
    return pl.pallas_call(
        functools.partial(_inproj_kernel, rope=rope),
        grid=(B, S // tm, nj),
        in_specs=[pl.BlockSpec((None, tm, D), lambda b, i, j: (b, i, 0)),
                  pl.BlockSpec((None, 1, 3 * D), lambda b, i, j: (mod_row(b), 0, 0)),
                  pl.BlockSpec((1, D), lambda b, i, j: (0, 0)),
                  pl.BlockSpec((D, Z_CHUNK), lambda b, i, j: (0, j)),
                  pl.BlockSpec((8, BRANCH_W), lambda b, i, j: (0, 0)),
                  pl.BlockSpec((BRANCH_W, BRANCH_W), lambda b, i, j: (0, 0)),
                  pl.BlockSpec((BRANCH_W, BRANCH_W), lambda b, i, j: (0, 0)),
                  rope_spec, rope_spec, rope_spec],
        out_specs=[pl.BlockSpec((None, tm, D), lambda b, i, j: (b, i, 0)),
                   pl.BlockSpec((None, tm, Z_CHUNK), lambda b, i, j: (b, i, j))],
        out_shape=[jax.ShapeDtypeStruct((B, S, D), BF16),
                   jax.ShapeDtypeStruct((B, S, Z_WIDTH), BF16)],
        scratch_shapes=[pltpu.VMEM((tm, D), BF16)],
        compiler_params=_cparams(3),
        name="inproj",
    )(h, mod_l, ng, w, gains, bd64, bd32, cos, sina, sinb)


def _branch_a_kernel(au_ref, av_ref, az_ref, g_ref, ws_ref, bs_ref, o_ref):
    tm = au_ref.shape[0]
    v = _gelu(av_ref[...].astype(F32))
    ms = jnp.mean(v * v, axis=-1, keepdims=True)
    vb = (v * lax.rsqrt(ms + EPS) * g_ref[...]).astype(BF16)
    gate = _gelu(au_ref[...].astype(F32)) * _silu(az_ref[...].astype(F32))
    for c in range(tm // CHUNK):
        rows = slice(c * CHUNK, (c + 1) * CHUNK)
        for g in range(A_GROUPS):
            cols = slice(g * LANES, (g + 1) * LANES)
            mixed = jnp.dot(ws_ref[g], vb[rows, cols], preferred_element_type=F32) + bs_ref[:, g:g + 1]
            o_ref[rows, cols] = (gate[rows, cols] * mixed).astype(BF16)


def _branch_a(z, g_v, w_s, b_sT, tm):
    B, S, _ = z.shape
    W = BRANCH_W
    col = lambda cb: pl.BlockSpec((None, tm, W), lambda b, i: (b, i, cb))
    return pl.pallas_call(
        _branch_a_kernel,
        grid=(B, S // tm),
        in_specs=[col(COL_AU), col(COL_AV), col(COL_AZ),
                  pl.BlockSpec((1, W), lambda b, i: (0, 0)),
                  pl.BlockSpec((A_GROUPS, CHUNK, CHUNK), lambda b, i: (0, 0, 0)),
                  pl.BlockSpec((CHUNK, A_GROUPS), lambda b, i: (0, 0))],
        out_specs=pl.BlockSpec((None, tm, W), lambda b, i: (b, i, 0)),
        out_shape=jax.ShapeDtypeStruct((B, S, W), BF16),
        compiler_params=_cparams(2),
        name="branch_a",
    )(z, z, z, g_v, w_s, b_sT)


def _dftw_kernel(c_ref, s_ref, w_ref, cw_ref, sw_ref):
    for g in range(C_GROUPS):
        w = w_ref[g]
        cw_ref[g] = jnp.dot(c_ref[...], w, preferred_element_type=F32,
                            precision=lax.Precision.HIGHEST).astype(BF16)
        sw_ref[g] = jnp.dot(s_ref[...], w, preferred_element_type=F32,
                            precision=lax.Precision.HIGHEST).astype(BF16)


def _dft_weights(c128, s128, w_f):
    shp = jax.ShapeDtypeStruct((C_GROUPS, C_GW, C_GW), BF16)
    return pl.pallas_call(_dftw_kernel, out_shape=[shp, shp], name="dft_weights")(c128, s128, w_f)


def _fourier_pq_kernel(x_ref, cw_ref, sw_ref, pq_ref):
    for g in range(C_GROUPS):
        cols = slice(g * LANES, (g + 1) * LANES)
        x = x_ref[:, cols]
        pq_ref[0, :, cols] = jnp.dot(x, cw_ref[g], preferred_element_type=F32).astype(BF16)
        pq_ref[1, :, cols] = jnp.dot(x, sw_ref[g], preferred_element_type=F32).astype(BF16)


def _fourier_pq(z, cw, sw, tm):
    B, S, _ = z.shape
    W = BRANCH_W
    wspec = pl.BlockSpec((C_GROUPS, C_GW, C_GW), lambda b, i: (0, 0, 0))
    return pl.pallas_call(
        _fourier_pq_kernel,
        grid=(B, S // tm),
        in_specs=[pl.BlockSpec((None, tm, W), lambda b, i: (b, i, COL_FIN)), wspec, wspec],
        out_specs=pl.BlockSpec((None, 2, tm, W), lambda b, i: (b, 0, i, 0)),
        out_shape=jax.ShapeDtypeStruct((B, 2, S, W), BF16),
        compiler_params=_cparams(2),
        name="fourier_pq",
    )(z, cw, sw)


def _fourier_mix_kernel(dft_ref, pq_ref, fz_ref, o_ref, *, scale):
    y = jnp.dot(dft_ref[...], pq_ref[...], preferred_element_type=F32) * scale
    o_ref[...] = (y * _silu(fz_ref[...].astype(F32))).astype(BF16)


def _fourier_mix(dft, pq, z, tm):
    B, S, _ = z.shape
    W = BRANCH_W
    scale = 1.0 / math.sqrt(S * C_GW)
    return pl.pallas_call(
        functools.partial(_fourier_mix_kernel, scale=scale),
        grid=(B, S // tm),
        in_specs=[pl.BlockSpec((tm, 2 * S), lambda b, i: (i, 0)),
                  pl.BlockSpec((None, 2 * S, W), lambda b, i: (b, 0, 0)),
                  pl.BlockSpec((None, tm, W), lambda b, i: (b, i, COL_FZ))],
        out_specs=pl.BlockSpec((None, tm, W), lambda b, i: (b, i, 0)),
        out_shape=jax.ShapeDtypeStruct((B, S, W), BF16),
        compiler_params=_cparams(2),
        name="fourier_mix",
    )(dft, pq, z)


def _lane_masks(n_heads, hd):
    lane = lax.broadcasted_iota(jnp.int32, (1, LANES), 1)
    return [((lane >= g * hd) & (lane < (g + 1) * hd)) for g in range(n_heads)]


def _attend(q, segs, sink_col, n_heads, hd):
    tq = q.shape[0]
    masks = _lane_masks(n_heads, hd)
    zero = jnp.zeros_like(q)
    qx = jnp.concatenate([jnp.where(m, q, zero) for m in masks], axis=0)
    scores = []
    for k, _, bias in segs:
        s = lax.dot_general(qx, k, (((1,), (1,)), ((), ())), preferred_element_type=F32)
        if bias is not None:
            s = s + bias
        scores.append(s)
    m = scores[0].max(axis=-1, keepdims=True)
    for s in scores[1:]:
        m = jnp.maximum(m, s.max(axis=-1, keepdims=True))
    if sink_col is not None:
        m = jnp.maximum(m, sink_col)
    den = None
    acc = None
    for s, (_, v, _) in zip(scores, segs):
        p = jnp.exp(s - m)
        d = p.sum(axis=-1, keepdims=True)
        o = jnp.dot(p.astype(BF16), v, preferred_element_type=F32)
        den = d if den is None else den + d
        acc = o if acc is None else acc + o
    if sink_col is not None:
        den = den + jnp.exp(sink_col - m)
    acc = acc / den
    out = None
    for g, msk in enumerate(masks):
        part = jnp.where(msk, acc[g * tq:(g + 1) * tq], 0.0)
        out = part if out is None else out + part
    return out


def _branch_b_kernel(sink_ref, q_ref, kv_ref, ckv_ref, bz_ref, o_ref, *, seq):
    i = pl.program_id(1)
    nl = 3 * B_BLK
    start = jnp.clip((i - 1) * B_BLK, 0, seq - nl)
    start = pl.multiple_of(start, B_BLK)
    qpos = i * B_BLK + lax.broadcasted_iota(jnp.int32, (B_BLK, nl), 0)
    kpos = start + lax.broadcasted_iota(jnp.int32, (B_BLK, nl), 1)
    bias1 = jnp.where(jnp.abs(kpos - qpos) <= B_WIN, 0.0, NEG).astype(F32)
    bias = jnp.concatenate([bias1, bias1], axis=0)
    half = BRANCH_W // 2
    for grp in range(BRANCH_W // LANES):
        kvh = grp // 2
        kc = slice(kvh * LANES, (kvh + 1) * LANES)
        vc = slice(half + kvh * LANES, half + (kvh + 1) * LANES)
        cols = slice(grp * LANES, (grp + 1) * LANES)
        sink_col = jnp.concatenate(
            [jnp.full((B_BLK, 1), sink_ref[2 * grp + g], F32) for g in range(2)], axis=0)
        segs = [(kv_ref[pl.ds(start, nl), kc], kv_ref[pl.ds(start, nl), vc], bias),
                (ckv_ref[:, kc], ckv_ref[:, vc], None)]
        o = _attend(q_ref[:, cols], segs, sink_col, 2, B_HD)
        o_ref[:, cols] = (o * _silu(bz_ref[:, cols].astype(F32))).astype(BF16)


def _branch_b(z, zc, sink):
    B, S, _ = z.shape
    C = zc.shape[1]
    W = BRANCH_W
    return pl.pallas_call(
        functools.partial(_branch_b_kernel, seq=S),
        grid=(B, S // B_BLK),
        in_specs=[pl.BlockSpec(memory_space=pltpu.SMEM),
                  pl.BlockSpec((None, B_BLK, W), lambda b, i: (b, i, COL_BQ)),
                  pl.BlockSpec((None, S, W), lambda b, i: (b, 0, COL_BKV)),
                  pl.BlockSpec((None, C, W), lambda b, i: (b, 0, COL_BKV)),
                  pl.BlockSpec((None, B_BLK, W), lambda b, i: (b, i, COL_BZ))],
        out_specs=pl.BlockSpec((None, B_BLK, W), lambda b, i: (b, i, 0)),
        out_shape=jax.ShapeDtypeStruct((B, S, W), BF16),
        compiler_params=_cparams(2),
        name="branch_b",
    )(sink, z, z, zc, z)


def _ctx_b_kernel(sink_ref, q_ref, kv_ref, bz_ref, o_ref):
    tq = q_ref.shape[0]
    half = BRANCH_W // 2
    for grp in range(BRANCH_W // LANES):
        kvh = grp // 2
        kc = slice(kvh * LANES, (kvh + 1) * LANES)
        vc = slice(half + kvh * LANES, half + (kvh + 1) * LANES)
        cols = slice(grp * LANES, (grp + 1) * LANES)
        sink_col = jnp.concatenate(
            [jnp.full((tq, 1), sink_ref[2 * grp + g], F32) for g in range(2)], axis=0)
        o = _attend(q_ref[:, cols], [(kv_ref[:, kc], kv_ref[:, vc], None)], sink_col, 2, B_HD)
        o_ref[:, cols] = (o * _silu(bz_ref[:, cols].astype(F32))).astype(BF16)


def _ctx_b(zc, sink):
    B, C, _ = zc.shape
    W = BRANCH_W
    col = lambda cb: pl.BlockSpec((None, C, W), lambda b: (b, 0, cb))
    return pl.pallas_call(
        _ctx_b_kernel,
        grid=(B,),
        in_specs=[pl.BlockSpec(memory_space=pltpu.SMEM), col(COL_BQ), col(COL_BKV), col(COL_BZ)],
        out_specs=pl.BlockSpec((None, C, W), lambda b: (b, 0, 0)),
        out_shape=jax.ShapeDtypeStruct((B, C, W), BF16),
        compiler_params=_cparams(1),
        name="ctx_b",
    )(sink, zc, zc, zc)


def _na_row_start(r, rows):
    return jnp.clip(r - NA_ROWS // 2, 0, rows - NA_ROWS)


def _branch_d_kernel(q_ref, k_ref, v_ref, ck_ref, cv_ref, bias_ref, dz_ref, o_ref, *, rows):
    r = pl.program_id(1)
    nk = NA_ROWS * GRID_W
    start = pl.multiple_of(_na_row_start(r, rows) * GRID_W, GRID_W)
    hpg = LANES // D_HD
    for grp in range(BRANCH_W // LANES):
        cols = slice(grp * LANES, (grp + 1) * LANES)
        bias = bias_ref[grp * hpg:(grp + 1) * hpg].reshape(hpg * GRID_W, nk)
        segs = [(k_ref[pl.ds(start, nk), cols], v_ref[pl.ds(start, nk), cols], bias),
                (ck_ref[:, cols], cv_ref[:, cols], None)]
        o = _attend(q_ref[:, cols], segs, None, hpg, D_HD)
        o_ref[:, cols] = (o * _silu(dz_ref[:, cols].astype(F32))).astype(BF16)


def _branch_d(z, zc, bias_tab):
    B, S, _ = z.shape
    C = zc.shape[1]
    W = BRANCH_W
    rows = S // GRID_W
    nk = NA_ROWS * GRID_W
    tile = lambda cb: pl.BlockSpec((None, GRID_W, W), lambda b, r: (b, r, cb))
    full = lambda cb: pl.BlockSpec((None, S, W), lambda b, r: (b, 0, cb))
    ctx = lambda cb: pl.BlockSpec((None, C, W), lambda b, r: (b, 0, cb))
    return pl.pallas_call(
        functools.partial(_branch_d_kernel, rows=rows),
        grid=(B, rows),
        in_specs=[tile(COL_DQ), full(COL_DK), full(COL_DV), ctx(COL_DK), ctx(COL_DV),
                  pl.BlockSpec((None, D_H, GRID_W, nk),
                               lambda b, r: (r - _na_row_start(r, rows), 0, 0, 0)),
                  tile(COL_DZ)],
        out_specs=pl.BlockSpec((None, GRID_W, W), lambda b, r: (b, r, 0)),
        out_shape=jax.ShapeDtypeStruct((B, S, W), BF16),
        compiler_params=_cparams(2),
        name="branch_d",
    )(z, z, z, zc, zc, bias_tab, z)


def _ctx_d_kernel(q_ref, k_ref, v_ref, dz_ref, o_ref):
    hpg = LANES // D_HD
    for grp in range(BRANCH_W // LANES):
        cols = slice(grp * LANES, (grp + 1) * LANES)
        o = _attend(q_ref[:, cols], [(k_ref[:, cols], v_ref[:, cols], None)], None, hpg, D_HD)
        o_ref[:, cols] = (o * _silu(dz_ref[:, cols].astype(F32))).astype(BF16)


def _ctx_d(zc):
    B, C, _ = zc.shape
    W = BRANCH_W
    col = lambda cb: pl.BlockSpec((None, C, W), lambda b: (b, 0, cb))
    return pl.pallas_call(
        _ctx_d_kernel,
        grid=(B,),
        in_specs=[col(COL_DQ), col(COL_DK), col(COL_DV), col(COL_DZ)],
        out_specs=pl.BlockSpec((None, C, W), lambda b: (b, 0, 0)),
        out_shape=jax.ShapeDtypeStruct((B, C, W), BF16),
        compiler_params=_cparams(1),
        name="ctx_d",
    )(zc, zc, zc, zc)


def _merge_kernel(h_ref, mod_ref, n_ref, ua_ref, ub_ref, uc_ref, ud_ref,
                  wg_ref, wb_ref, wo_ref, o_ref, acc_ref):
    D = h_ref.shape[-1]
    j = pl.program_id(2)

    @pl.when(j == 0)
    def _():
        acc_ref[...] = jnp.zeros_like(acc_ref)

    n = n_ref[...]
    mix = None
    for b, u_ref in enumerate((ua_ref, ub_ref, uc_ref, ud_ref)):
        gate = _sigmoid(jnp.dot(n, wg_ref[b], preferred_element_type=F32))
        t = gate * jnp.dot(u_ref[...], wb_ref[b], preferred_element_type=F32)
        mix = t if mix is None else mix + t
    acc_ref[...] += jnp.dot(mix.astype(BF16), wo_ref[...], preferred_element_type=F32)

    @pl.when(j == pl.num_programs(2) - 1)
    def _():
        o_ref[...] = h_ref[...] + mod_ref[:, 2 * D:3 * D] * acc_ref[...]


def _merge(h, mod_l, mod_row, n, us, wg, wb, wo, tm, cn):
    B, S, D = h.shape
    W = BRANCH_W
    row = lambda width: pl.BlockSpec((None, tm, width), lambda b, i, j: (b, i, 0))
    return pl.pallas_call(
        _merge_kernel,
        grid=(B, S // tm, D // cn),
        in_specs=[row(D),
                  pl.BlockSpec((None, 1, 3 * D), lambda b, i, j: (mod_row(b), 0, 0)),
                  row(D), row(W), row(W), row(W), row(W),
                  pl.BlockSpec((N_BRANCH, D, cn), lambda b, i, j: (0, 0, j)),
                  pl.BlockSpec((N_BRANCH, W, cn), lambda b, i, j: (0, 0, j)),
                  pl.BlockSpec((cn, D), lambda b, i, j: (j, 0))],
        out_specs=row(D),
        out_shape=jax.ShapeDtypeStruct((B, S, D), F32),
        scratch_shapes=[pltpu.VMEM((tm, D), F32)],
        compiler_params=_cparams(3),
        name="merge",
    )(h, mod_l, n, *us, wg, wb, wo)


def _rearrange_w_in(w_in):
    W = BRANCH_W
    o = 0
    parts = {}
    for name, size in (("a_u", W), ("a_v", W), ("a_z", W), ("b_q", W), ("b_k", B_KVH * B_HD),
                       ("b_v", B_KVH * B_HD), ("b_z", W), ("f_in", W), ("f_z", W),
                       ("d_q", W), ("d_k", W), ("d_v", W), ("d_z", W)):
        parts[name] = w_in[:, :, o:o + size]
        o += size

    def dup(p):
        return jnp.concatenate([p[:, :, 0:B_HD], p[:, :, 0:B_HD], p[:, :, B_HD:], p[:, :, B_HD:]], axis=-1)

    cols = [parts["a_u"], parts["a_v"], parts["a_z"], parts["b_q"], dup(parts["b_k"]), dup(parts["b_v"]),
            parts["b_z"], parts["f_in"], parts["f_z"], parts["d_q"], parts["d_k"], parts["d_v"], parts["d_z"]]
    return jnp.concatenate(cols, axis=-1).astype(BF16)


def _rope_tables(S):
    t = np.arange(S)
    pos = np.stack([t // GRID_W, t % GRID_W], axis=1).astype(np.float32)
    lane = np.arange(B_HD)
    which = lane // (B_HD // 2)
    fi = lane % (B_HD // 4)
    inv = (ROPE_BASE ** (-(fi.astype(np.float32)) / (B_HD // 4))).astype(np.float32)
    ang = pos[:, which] * inv[None, :]
    cos, sin = np.cos(ang), np.sin(ang)
    lower = (lane % (B_HD // 2)) < (B_HD // 4)
    sina = np.where(lower[None, :], -sin, 0.0)
    sinb = np.where(lower[None, :], 0.0, sin)
    rep = lambda a: jnp.asarray(np.tile(a.astype(np.float32), (1, LANES // B_HD)))
    return rep(cos), rep(sina), rep(sinb)


def _block_diag_mean(width, hd):
    i = np.arange(width)
    return jnp.asarray(((i[:, None] // hd) == (i[None, :] // hd)).astype(np.float32) / hd, dtype=BF16)


def _dft_cos_sin(n):
    k = jnp.arange(n, dtype=jnp.int32)
    m = (k[:, None] * k[None, :]) % n
    ang = m.astype(F32) * (2.0 * math.pi / n)
    return jnp.cos(ang), jnp.sin(ang)


def _dft_matrix(n):
    c, s = _dft_cos_sin(n)
    return jnp.concatenate([c, -s], axis=1).astype(BF16)


def _bias_expand_kernel(rpb_ref, e_ref, o_ref):
    o_ref[...] = jnp.dot(rpb_ref[...], e_ref[...], preferred_element_type=F32,
                         precision=lax.Precision.HIGHEST)


def _na_bias_tables(rpb):
    L, H, n_dr, n_dc = rpb.shape
    c = np.arange(GRID_W)
    c0 = np.clip(c - NA_COLS // 2, 0, GRID_W - NA_COLS)
    kc = np.arange(GRID_W)
    inwin = (kc[None, :] >= c0[:, None]) & (kc[None, :] < c0[:, None] + NA_COLS)
    dc = kc[None, :] - c[:, None] + (NA_COLS - 1)
    n_dc_pad = 32
    onehot = (np.arange(n_dc_pad)[:, None, None] == dc[None]) & inwin[None]
    e2d = jnp.asarray(onehot.reshape(n_dc_pad, GRID_W * GRID_W).astype(np.float32))
    rpb2d = jnp.pad(rpb.reshape(L, H * n_dr, n_dc), ((0, 0), (0, 0), (0, n_dc_pad - n_dc)))
    t = pl.pallas_call(
        _bias_expand_kernel,
        grid=(L,),
        in_specs=[pl.BlockSpec((None, H * n_dr, n_dc_pad), lambda l: (l, 0, 0)),
                  pl.BlockSpec((n_dc_pad, GRID_W * GRID_W), lambda l: (0, 0))],
        out_specs=pl.BlockSpec((None, H * n_dr, GRID_W * GRID_W), lambda l: (l, 0, 0)),
        out_shape=jax.ShapeDtypeStruct((L, H * n_dr, GRID_W * GRID_W), F32),
        compiler_params=_cparams(1),
        name="bias_expand",
    )(rpb2d, e2d)
    t = t.reshape(L, H, n_dr, GRID_W, GRID_W)
    t = jnp.where(jnp.asarray(inwin)[None, None, None], t, NEG)
    t = t.transpose(0, 1, 3, 2, 4).reshape(L, H, GRID_W, n_dr * GRID_W)
    nk = NA_ROWS * GRID_W
    cases = [t[..., (NA_ROWS - 1 - d) * GRID_W:(NA_ROWS - 1 - d) * GRID_W + nk] for d in range(NA_ROWS)]
    return jnp.stack(cases, axis=1)


def kernel(x, c, ctx, c_ctx, norm_g, w_ada, b_ada, w_in, a_norm_g, a_w_s, a_b_s, b_q_g, b_k_g, b_sink,
           c_w_f, d_q_g, d_k_g, d_rpb, w_gate, w_branch, w_out):
    B, S, D = x.shape
    C = ctx.shape[1]
    L = norm_g.shape[0]
    assert S % 512 == 0 and S // GRID_W >= NA_ROWS and S >= 3 * B_BLK and C % CHUNK == 0
    tm, tmc = 512, C
    W = BRANCH_W

    n_rows = -(-(B + 1) // 8) * 8
    cc = jnp.concatenate([c, c_ctx[None], jnp.zeros((n_rows - B - 1, D), F32)], axis=0)
    mod = _modulation(cc, w_ada, b_ada).reshape(L, n_rows, 1, 3 * D)
    lat_row = lambda b: b
    ctx_row = lambda b: B

    w_in_r = _rearrange_w_in(w_in)
    w_gate_b, w_branch_b, w_out_b = w_gate.astype(BF16), w_branch.astype(BF16), w_out.astype(BF16)
    a_ws_b = a_w_s.astype(BF16)
    gains = jnp.stack([jnp.tile(b_q_g, (1, W // B_HD)),
                       jnp.tile(b_k_g, (1, W // B_HD)),
                       jnp.tile(d_q_g, (1, W // D_HD)),
                       jnp.tile(d_k_g, (1, W // D_HD))] + [jnp.zeros((L, W), F32)] * 4, axis=1)
    bd64, bd32 = _block_diag_mean(W, B_HD), _block_diag_mean(W, D_HD)
    ropes = _rope_tables(S)
    ropes_c = tuple(t[:C] for t in ropes)
    c128, s128 = _dft_cos_sin(C_GW)
    dft_lat, dft_ctx = _dft_matrix(S), _dft_matrix(C)
    bias_tabs = _na_bias_tables(d_rpb)

    h, hc = x, ctx
    for l in range(L):
        last = l == L - 1
        mod_l = mod[l]
        ng = norm_g[l][None]
        n_lat, z = _inproj(h, mod_l, lat_row, ng, w_in_r[l], gains[l], bd64, bd32, ropes, tm, True)
        n_ctx, zc = _inproj(hc, mod_l, ctx_row, ng, w_in_r[l], gains[l], bd64, bd32, ropes_c, tmc, False)
        cw, sw = _dft_weights(c128, s128, c_w_f[l])
        g_v = a_norm_g[l][None]
        b_sT = a_b_s[l].T
        bias_tab = bias_tabs[l]

        u_lat = [
            _branch_a(z, g_v, a_ws_b[l], b_sT, tm),
            _branch_b(z, zc, b_sink[l]),
            _fourier_mix(dft_lat, _fourier_pq(z, cw, sw, tm).reshape(B, 2 * S, W), z, tm),
            _branch_d(z, zc, bias_tab),
        ]
        h_new = _merge(h, mod_l, lat_row, n_lat, u_lat, w_gate_b[l], w_branch_b[l], w_out_b[l], tm, 256)
        if not last:
            u_ctx = [
                _branch_a(zc, g_v, a_ws_b[l], b_sT, tmc),
                _ctx_b(zc, b_sink[l]),
                _fourier_mix(dft_ctx, _fourier_pq(zc, cw, sw, tmc).reshape(B, 2 * C, W), zc, tmc),
                _ctx_d(zc),
            ]
            hc = _merge(hc, mod_l, ctx_row, n_ctx, u_ctx, w_gate_b[l], w_branch_b[l], w_out_b[l], tmc, 256)
        h = h_new
    return h
```

```python
import functools
import math

import jax
import jax.numpy as jnp
import numpy as np
from jax import lax
from jax.experimental import pallas as pl
from jax.experimental.pallas import tpu as pltpu

F32 = jnp.float32
BF16 = jnp.bfloat16

LANES = 128
V7X_VMEM_LIMIT_BYTES = 56 * 1024 * 1024

GRID_W = 64
N_BRANCH = 4
BRANCH_W = 512
CHUNK = 128
A_GROUPS = 4
B_HD = 64
B_QH = 8
B_KVH = 2
B_WIN = 128
B_BLK = 128
C_GROUPS = 4
C_GW = 128
D_HD = 32
D_H = 16
NA_ROWS = 8
NA_COLS = 16
ROPE_BASE = 10000.0
EPS = 1e-6
NEG = -1e30
LOG2E = math.log2(math.e)

(COL_AU, COL_AV, COL_AZ, COL_BQ, COL_BKV, COL_BZ,
 COL_FIN, COL_FZ, COL_DQ, COL_DK, COL_DV, COL_DZ) = range(12)
Z_WIDTH = 12 * BRANCH_W


def _cparams(n_grid, flags=None):
    return pltpu.CompilerParams(
        dimension_semantics=("arbitrary",) * n_grid,
        vmem_limit_bytes=V7X_VMEM_LIMIT_BYTES,
        flags=flags)


def _sigmoid(x):
    return 1.0 / (1.0 + jnp.exp(-x))


def _silu(x):
    return x * _sigmoid(x)


def _gelu(x):
    return 0.5 * x * (1.0 + jnp.tanh(math.sqrt(2.0 / math.pi) * (x + 0.044715 * (x * x * x))))


def _mod_kernel(c_ref, w_ref, b_ref, o_ref):
    c = c_ref[...]
    a = _silu(c).astype(BF16)
    o_ref[...] = jnp.dot(a, w_ref[...].astype(BF16), preferred_element_type=F32) + b_ref[...]


def _modulation(cc, w_ada, b_ada):
    L, D, N = w_ada.shape
    R = cc.shape[0]
    tn = 512
    return pl.pallas_call(
        _mod_kernel,
        grid=(L, N // tn),
        in_specs=[pl.BlockSpec((R, D), lambda l, j: (0, 0)),
                  pl.BlockSpec((None, D, tn), lambda l, j: (l, 0, j)),
                  pl.BlockSpec((None, 1, tn), lambda l, j: (l, 0, j))],
        out_specs=pl.BlockSpec((None, R, tn), lambda l, j: (l, 0, j)),
        out_shape=jax.ShapeDtypeStruct((L, R, N), F32),
        compiler_params=_cparams(2),
        name="modulation",
    )(cc, w_ada, b_ada.reshape(L, 1, N))


def _head_rms(x, bd_ref, width):
    sq = x * x
    hi = sq.astype(BF16)
    lo = (sq - hi.astype(F32)).astype(BF16)
    bd = bd_ref[0:width, 0:width]
    return (jnp.dot(hi, bd, preferred_element_type=F32)
            + jnp.dot(lo, bd, preferred_element_type=F32))


def _rope(x, cos, sina, sinb):
    w = x.shape[-1]
    reps = w // LANES
    cos = jnp.concatenate([cos] * reps, axis=1)
    sina = jnp.concatenate([sina] * reps, axis=1)
    sinb = jnp.concatenate([sinb] * reps, axis=1)
    up = pltpu.roll(x, w - 16, 1)
    dn = pltpu.roll(x, 16, 1)
    return x * cos + up * sina + dn * sinb


def _inproj_kernel(h_ref, mod_ref, ng_ref, w_ref, gains_ref, bd64_ref, bd32_ref,
                   cos_ref, sina_ref, sinb_ref, n_ref, z_ref, *, rope):
    D = h_ref.shape[-1]
    W = BRANCH_W
    x = h_ref[...]
    ms = jnp.mean(x * x, axis=-1, keepdims=True)
    y = x * lax.rsqrt(ms + EPS) * ng_ref[...]
    nb = (y * (1.0 + mod_ref[:, D:2 * D]) + mod_ref[:, 0:D]).astype(BF16)
    n_ref[...] = nb

    def maybe_rope(t):
        if rope:
            return _rope(t, cos_ref[...], sina_ref[...], sinb_ref[...])
        return t

    def head_norm(t, bd_ref, gain):
        return t * lax.rsqrt(_head_rms(t, bd_ref, t.shape[-1]) + EPS) * gain

    for cb in range(Z_WIDTH // W):
        cols = slice(cb * W, (cb + 1) * W)
        acc = jnp.dot(nb, w_ref[:, cols], preferred_element_type=F32)
        if cb == COL_BQ:
            acc = maybe_rope(head_norm(acc, bd64_ref, gains_ref[0:1, :])) * (B_HD ** -0.5 * LOG2E)
        elif cb == COL_BKV:
            k = maybe_rope(head_norm(acc[:, 0:W // 2], bd64_ref, gains_ref[1:2, 0:W // 2]))
            acc = jnp.concatenate([k, acc[:, W // 2:]], axis=1)
        elif cb == COL_DQ:
            acc = head_norm(acc, bd32_ref, gains_ref[2:3, :]) * (D_HD ** -0.5 * LOG2E)
        elif cb == COL_DK:
            acc = head_norm(acc, bd32_ref, gains_ref[3:4, :])
        z_ref[:, cols] = acc.astype(BF16)


def _inproj(h, mod_l, mod_row, ng, w, gains, bd64, bd32, ropes, tm, rope):
    B, S, D = h.shape
    cos, sina, sinb = ropes
    const = lambda shape: pl.BlockSpec(shape, lambda b, i: (0,) * len(shape), pipeline_mode=pl.Buffered(1))
    rope_spec = pl.BlockSpec((tm, LANES), lambda b, i: (i if rope else 0, 0))
    return pl.pallas_call(
        functools.partial(_inproj_kernel, rope=rope),
        grid=(B, S // tm),
        in_specs=[pl.BlockSpec((None, tm, D), lambda b, i: (b, i, 0)),
                  pl.BlockSpec((None, 1, 3 * D), lambda b, i: (mod_row(b), 0, 0)),
                  const((1, D)),
                  const((D, Z_WIDTH)),
                  const((8, BRANCH_W)),
                  const((BRANCH_W, BRANCH_W)),
                  const((BRANCH_W, BRANCH_W)),
                  rope_spec, rope_spec, rope_spec],
        out_specs=[pl.BlockSpec((None, tm, D), lambda b, i: (b, i, 0)),
                   pl.BlockSpec((None, tm, Z_WIDTH), lambda b, i: (b, i, 0))],
        out_shape=[jax.ShapeDtypeStruct((B, S, D), BF16),
                   jax.ShapeDtypeStruct((B, S, Z_WIDTH), BF16)],
        compiler_params=_cparams(2),
        name="inproj",
    )(h, mod_l, ng, w, gains, bd64, bd32, cos, sina, sinb)


def _branch_a_kernel(au_ref, av_ref, az_ref, g_ref, ws_ref, bs_ref, o_ref):
    tm = au_ref.shape[0]
    v = _gelu(av_ref[...].astype(F32))
    ms = jnp.mean(v * v, axis=-1, keepdims=True)
    vb = (v * lax.rsqrt(ms + EPS) * g_ref[...]).astype(BF16)
    gate = _gelu(au_ref[...].astype(F32)) * _silu(az_ref[...].astype(F32))
    for c in range(tm // CHUNK):
        rows = slice(c * CHUNK, (c + 1) * CHUNK)
        for g in range(A_GROUPS):
            cols = slice(g * LANES, (g + 1) * LANES)
            mixed = jnp.dot(ws_ref[g], vb[rows, cols], preferred_element_type=F32) + bs_ref[:, g:g + 1]
            o_ref[rows, cols] = (gate[rows, cols] * mixed).astype(BF16)


def _branch_a(z, g_v, w_s, b_sT, tm):
    B, S, _ = z.shape
    W = BRANCH_W
    col = lambda cb: pl.BlockSpec((None, tm, W), lambda b, i: (b, i, cb))
    return pl.pallas_call(
        _branch_a_kernel,
        grid=(B, S // tm),
        in_specs=[col(COL_AU), col(COL_AV), col(COL_AZ),
                  pl.BlockSpec((1, W), lambda b, i: (0, 0)),
                  pl.BlockSpec((A_GROUPS, CHUNK, CHUNK), lambda b, i: (0, 0, 0)),
                  pl.BlockSpec((CHUNK, A_GROUPS), lambda b, i: (0, 0))],
        out_specs=pl.BlockSpec((None, tm, W), lambda b, i: (b, i, 0)),
        out_shape=jax.ShapeDtypeStruct((B, S, W), BF16),
        compiler_params=_cparams(2),
        name="branch_a",
    )(z, z, z, g_v, w_s, b_sT)


def _dftw_kernel(c_ref, s_ref, w_ref, cw_ref, sw_ref):
    for g in range(C_GROUPS):
        w = w_ref[g]
        cw_ref[g] = jnp.dot(c_ref[...], w, preferred_element_type=F32,
                            precision=lax.Precision.HIGHEST).astype(BF16)
        sw_ref[g] = jnp.dot(s_ref[...], w, preferred_element_type=F32,
                            precision=lax.Precision.HIGHEST).astype(BF16)


def _dft_weights(c128, s128, w_f):
    shp = jax.ShapeDtypeStruct((C_GROUPS, C_GW, C_GW), BF16)
    return pl.pallas_call(_dftw_kernel, out_shape=[shp, shp], name="dft_weights")(c128, s128, w_f)


def _fourier_pq_kernel(x_ref, cw_ref, sw_ref, pq_ref):
    for g in range(C_GROUPS):
        cols = slice(g * LANES, (g + 1) * LANES)
        x = x_ref[:, cols]
        pq_ref[0, :, cols] = jnp.dot(x, cw_ref[g], preferred_element_type=F32).astype(BF16)
        pq_ref[1, :, cols] = jnp.dot(x, sw_ref[g], preferred_element_type=F32).astype(BF16)


def _fourier_pq(z, cw, sw, tm):
    B, S, _ = z.shape
    W = BRANCH_W
    wspec = pl.BlockSpec((C_GROUPS, C_GW, C_GW), lambda b, i: (0, 0, 0))
    return pl.pallas_call(
        _fourier_pq_kernel,
        grid=(B, S // tm),
        in_specs=[pl.BlockSpec((None, tm, W), lambda b, i: (b, i, COL_FIN)), wspec, wspec],
        out_specs=pl.BlockSpec((None, 2, tm, W), lambda b, i: (b, 0, i, 0)),
        out_shape=jax.ShapeDtypeStruct((B, 2, S, W), BF16),
        compiler_params=_cparams(2),
        name="fourier_pq",
    )(z, cw, sw)


def _fourier_mix_kernel(dft_ref, pq_ref, fz_ref, o_ref, *, scale):
    y = jnp.dot(dft_ref[...], pq_ref[...], preferred_element_type=F32) * scale
    o_ref[...] = (y * _silu(fz_ref[...].astype(F32))).astype(BF16)


def _fourier_mix(dft, pq, z, tm):
    B, S, _ = z.shape
    W = BRANCH_W
    scale = 1.0 / math.sqrt(S * C_GW)
    return pl.pallas_call(
        functools.partial(_fourier_mix_kernel, scale=scale),
        grid=(B, S // tm),
        in_specs=[pl.BlockSpec((tm, 2 * S), lambda b, i: (i, 0)),
                  pl.BlockSpec((None, 2 * S, W), lambda b, i: (b, 0, 0)),
                  pl.BlockSpec((None, tm, W), lambda b, i: (b, i, COL_FZ))],
        out_specs=pl.BlockSpec((None, tm, W), lambda b, i: (b, i, 0)),
        out_shape=jax.ShapeDtypeStruct((B, S, W), BF16),
        compiler_params=_cparams(2),
        name="fourier_mix",
    )(dft, pq, z)


def _lane_masks(n_heads, hd):
    lane = lax.broadcasted_iota(jnp.int32, (1, LANES), 1)
    return [((lane >= g * hd) & (lane < (g + 1) * hd)) for g in range(n_heads)]


def _attend(q, segs, sink_col, n_heads, hd):
    tq = q.shape[0]
    masks = _lane_masks(n_heads, hd)
    zero = jnp.zeros_like(q)
    qx = jnp.concatenate([jnp.where(m, q, zero) for m in masks], axis=0)
    scores = []
    for k, _, bias in segs:
        s = lax.dot_general(qx, k, (((1,), (1,)), ((), ())), preferred_element_type=F32)
        if bias is not None:
            s = s + bias
        scores.append(s)
    m = scores[0].max(axis=-1, keepdims=True)
    for s in scores[1:]:
        m = jnp.maximum(m, s.max(axis=-1, keepdims=True))
    if sink_col is not None:
        m = jnp.maximum(m, sink_col)
    den = None
    acc = None
    for s, (_, v, _) in zip(scores, segs):
        p = jnp.exp2(s - m)
        d = p.sum(axis=-1, keepdims=True)
        o = jnp.dot(p.astype(BF16), v, preferred_element_type=F32)
        den = d if den is None else den + d
        acc = o if acc is None else acc + o
    if sink_col is not None:
        den = den + jnp.exp2(sink_col - m)
    acc = acc / den
    out = None
    for g, msk in enumerate(masks):
        part = jnp.where(msk, acc[g * tq:(g + 1) * tq], 0.0)
        out = part if out is None else out + part
    return out


def _branch_b_kernel(sink_ref, q_ref, kv_ref, ckv_ref, bz_ref, o_ref, *, seq):
    i = pl.program_id(1)
    nl = 3 * B_BLK
    start = jnp.clip((i - 1) * B_BLK, 0, seq - nl)
    start = pl.multiple_of(start, B_BLK)
    qpos = i * B_BLK + lax.broadcasted_iota(jnp.int32, (B_BLK, nl), 0)
    kpos = start + lax.broadcasted_iota(jnp.int32, (B_BLK, nl), 1)
    bias1 = jnp.where(jnp.abs(kpos - qpos) <= B_WIN, 0.0, NEG).astype(F32)
    bias = jnp.concatenate([bias1, bias1], axis=0)
    half = BRANCH_W // 2
    for grp in range(BRANCH_W // LANES):
        kvh = grp // 2
        kc = slice(kvh * LANES, (kvh + 1) * LANES)
        vc = slice(half + kvh * LANES, half + (kvh + 1) * LANES)
        cols = slice(grp * LANES, (grp + 1) * LANES)
        sink_col = jnp.concatenate(
            [jnp.full((B_BLK, 1), sink_ref[2 * grp + g] * LOG2E, F32) for g in range(2)], axis=0)
        segs = [(kv_ref[pl.ds(start, nl), kc], kv_ref[pl.ds(start, nl), vc], bias),
                (ckv_ref[:, kc], ckv_ref[:, vc], None)]
        o = _attend(q_ref[:, cols], segs, sink_col, 2, B_HD)
        o_ref[:, cols] = (o * _silu(bz_ref[:, cols].astype(F32))).astype(BF16)


def _branch_b(z, zc, sink):
    B, S, _ = z.shape
    C = zc.shape[1]
    W = BRANCH_W
    return pl.pallas_call(
        functools.partial(_branch_b_kernel, seq=S),
        grid=(B, S // B_BLK),
        in_specs=[pl.BlockSpec(memory_space=pltpu.SMEM),
                  pl.BlockSpec((None, B_BLK, W), lambda b, i: (b, i, COL_BQ)),
                  pl.BlockSpec((None, S, W), lambda b, i: (b, 0, COL_BKV)),
                  pl.BlockSpec((None, C, W), lambda b, i: (b, 0, COL_BKV)),
                  pl.BlockSpec((None, B_BLK, W), lambda b, i: (b, i, COL_BZ))],
        out_specs=pl.BlockSpec((None, B_BLK, W), lambda b, i: (b, i, 0)),
        out_shape=jax.ShapeDtypeStruct((B, S, W), BF16),
        compiler_params=_cparams(2),
        name="branch_b",
    )(sink, z, z, zc, z)


def _ctx_b_kernel(sink_ref, q_ref, kv_ref, bz_ref, o_ref):
    tq = q_ref.shape[0]
    half = BRANCH_W // 2
    for grp in range(BRANCH_W // LANES):
        kvh = grp // 2
        kc = slice(kvh * LANES, (kvh + 1) * LANES)
        vc = slice(half + kvh * LANES, half + (kvh + 1) * LANES)
        cols = slice(grp * LANES, (grp + 1) * LANES)
        sink_col = jnp.concatenate(
            [jnp.full((tq, 1), sink_ref[2 * grp + g] * LOG2E, F32) for g in range(2)], axis=0)
        o = _attend(q_ref[:, cols], [(kv_ref[:, kc], kv_ref[:, vc], None)], sink_col, 2, B_HD)
        o_ref[:, cols] = (o * _silu(bz_ref[:, cols].astype(F32))).astype(BF16)


def _ctx_b(zc, sink):
    B, C, _ = zc.shape
    W = BRANCH_W
    col = lambda cb: pl.BlockSpec((None, C, W), lambda b: (b, 0, cb))
    return pl.pallas_call(
        _ctx_b_kernel,
        grid=(B,),
        in_specs=[pl.BlockSpec(memory_space=pltpu.SMEM), col(COL_BQ), col(COL_BKV), col(COL_BZ)],
        out_specs=pl.BlockSpec((None, C, W), lambda b: (b, 0, 0)),
        out_shape=jax.ShapeDtypeStruct((B, C, W), BF16),
        compiler_params=_cparams(1),
        name="ctx_b",
    )(sink, zc, zc, zc)


def _na_row_start(r, rows):
    return jnp.clip(r - NA_ROWS // 2, 0, rows - NA_ROWS)


def _branch_d_kernel(q_ref, k_ref, v_ref, ck_ref, cv_ref, bias_ref, dz_ref, o_ref, *, rows):
    r = pl.program_id(1)
    nk = NA_ROWS * GRID_W
    start = pl.multiple_of(_na_row_start(r, rows) * GRID_W, GRID_W)
    hpg = LANES // D_HD
    for grp in range(BRANCH_W // LANES):
        cols = slice(grp * LANES, (grp + 1) * LANES)
        bias = bias_ref[grp * hpg:(grp + 1) * hpg].reshape(hpg * GRID_W, nk)
        segs = [(k_ref[pl.ds(start, nk), cols], v_ref[pl.ds(start, nk), cols], bias),
                (ck_ref[:, cols], cv_ref[:, cols], None)]
        o = _attend(q_ref[:, cols], segs, None, hpg, D_HD)
        o_ref[:, cols] = (o * _silu(dz_ref[:, cols].astype(F32))).astype(BF16)


def _branch_d(z, zc, bias_tab):
    B, S, _ = z.shape
    C = zc.shape[1]
    W = BRANCH_W
    rows = S // GRID_W
    nk = NA_ROWS * GRID_W
    tile = lambda cb: pl.BlockSpec((None, GRID_W, W), lambda b, r: (b, r, cb))
    full = lambda cb: pl.BlockSpec((None, S, W), lambda b, r: (b, 0, cb))
    ctx = lambda cb: pl.BlockSpec((None, C, W), lambda b, r: (b, 0, cb))
    return pl.pallas_call(
        functools.partial(_branch_d_kernel, rows=rows),
        grid=(B, rows),
        in_specs=[tile(COL_DQ), full(COL_DK), full(COL_DV), ctx(COL_DK), ctx(COL_DV),
                  pl.BlockSpec((None, D_H, GRID_W, nk),
                               lambda b, r: (r - _na_row_start(r, rows), 0, 0, 0)),
                  tile(COL_DZ)],
        out_specs=pl.BlockSpec((None, GRID_W, W), lambda b, r: (b, r, 0)),
        out_shape=jax.ShapeDtypeStruct((B, S, W), BF16),
        compiler_params=_cparams(2),
        name="branch_d",
    )(z, z, z, zc, zc, bias_tab, z)


def _ctx_d_kernel(q_ref, k_ref, v_ref, dz_ref, o_ref):
    hpg = LANES // D_HD
    for grp in range(BRANCH_W // LANES):
        cols = slice(grp * LANES, (grp + 1) * LANES)
        o = _attend(q_ref[:, cols], [(k_ref[:, cols], v_ref[:, cols], None)], None, hpg, D_HD)
        o_ref[:, cols] = (o * _silu(dz_ref[:, cols].astype(F32))).astype(BF16)


def _ctx_d(zc):
    B, C, _ = zc.shape
    W = BRANCH_W
    col = lambda cb: pl.BlockSpec((None, C, W), lambda b: (b, 0, cb))
    return pl.pallas_call(
        _ctx_d_kernel,
        grid=(B,),
        in_specs=[col(COL_DQ), col(COL_DK), col(COL_DV), col(COL_DZ)],
        out_specs=pl.BlockSpec((None, C, W), lambda b: (b, 0, 0)),
        out_shape=jax.ShapeDtypeStruct((B, C, W), BF16),
        compiler_params=_cparams(1),
        name="ctx_d",
    )(zc, zc, zc, zc)


def _merge_kernel(h_ref, mod_ref, n_ref, ua_ref, ub_ref, uc_ref, ud_ref,
                  wg_ref, wb_ref, wo_ref, o_ref, acc_ref):
    D = h_ref.shape[-1]
    j = pl.program_id(2)

    @pl.when(j == 0)
    def _():
        acc_ref[...] = jnp.zeros_like(acc_ref)

    n = n_ref[...]
    mix = None
    for b, u_ref in enumerate((ua_ref, ub_ref, uc_ref, ud_ref)):
        gate = _sigmoid(jnp.dot(n, wg_ref[b], preferred_element_type=F32))
        t = gate * jnp.dot(u_ref[...], wb_ref[b], preferred_element_type=F32)
        mix = t if mix is None else mix + t
    acc_ref[...] += jnp.dot(mix.astype(BF16), wo_ref[...], preferred_element_type=F32)

    @pl.when(j == pl.num_programs(2) - 1)
    def _():
        o_ref[...] = h_ref[...] + mod_ref[:, 2 * D:3 * D] * acc_ref[...]


def _merge(h, mod_l, mod_row, n, us, wg, wb, wo, tm, cn):
    B, S, D = h.shape
    W = BRANCH_W
    row = lambda width: pl.BlockSpec((None, tm, width), lambda b, i, j: (b, i, 0))
    row1 = lambda width: pl.BlockSpec((None, tm, width), lambda b, i, j: (b, i, 0),
                                      pipeline_mode=pl.Buffered(1))
    return pl.pallas_call(
        _merge_kernel,
        grid=(B, S // tm, D // cn),
        in_specs=[row1(D),
                  pl.BlockSpec((None, 1, 3 * D), lambda b, i, j: (mod_row(b), 0, 0)),
                  row1(D), row1(W), row1(W), row1(W), row1(W),
                  pl.BlockSpec((N_BRANCH, D, cn), lambda b, i, j: (0, 0, j)),
                  pl.BlockSpec((N_BRANCH, W, cn), lambda b, i, j: (0, 0, j)),
                  pl.BlockSpec((cn, D), lambda b, i, j: (j, 0))],
        out_specs=row(D),
        out_shape=jax.ShapeDtypeStruct((B, S, D), F32),
        scratch_shapes=[pltpu.VMEM((tm, D), F32)],
        compiler_params=_cparams(3),
        name="merge",
    )(h, mod_l, n, *us, wg, wb, wo)


def _rearrange_w_in(w_in):
    W = BRANCH_W
    o = 0
    parts = {}
    for name, size in (("a_u", W), ("a_v", W), ("a_z", W), ("b_q", W), ("b_k", B_KVH * B_HD),
                       ("b_v", B_KVH * B_HD), ("b_z", W), ("f_in", W), ("f_z", W),
                       ("d_q", W), ("d_k", W), ("d_v", W), ("d_z", W)):
        parts[name] = w_in[:, :, o:o + size]
        o += size

    def dup(p):
        return jnp.concatenate([p[:, :, 0:B_HD], p[:, :, 0:B_HD], p[:, :, B_HD:], p[:, :, B_HD:]], axis=-1)

    cols = [parts["a_u"], parts["a_v"], parts["a_z"], parts["b_q"], dup(parts["b_k"]), dup(parts["b_v"]),
            parts["b_z"], parts["f_in"], parts["f_z"], parts["d_q"], parts["d_k"], parts["d_v"], parts["d_z"]]
    return jnp.concatenate(cols, axis=-1).astype(BF16)


def _rope_tables(S):
    t = np.arange(S)
    pos = np.stack([t // GRID_W, t % GRID_W], axis=1).astype(np.float32)
    lane = np.arange(B_HD)
    which = lane // (B_HD // 2)
    fi = lane % (B_HD // 4)
    inv = (ROPE_BASE ** (-(fi.astype(np.float32)) / (B_HD // 4))).astype(np.float32)
    ang = pos[:, which] * inv[None, :]
    cos, sin = np.cos(ang), np.sin(ang)
    lower = (lane % (B_HD // 2)) < (B_HD // 4)
    sina = np.where(lower[None, :], -sin, 0.0)
    sinb = np.where(lower[None, :], 0.0, sin)
    rep = lambda a: jnp.asarray(np.tile(a.astype(np.float32), (1, LANES // B_HD)))
    return rep(cos), rep(sina), rep(sinb)


def _block_diag_mean(width, hd):
    i = np.arange(width)
    return jnp.asarray(((i[:, None] // hd) == (i[None, :] // hd)).astype(np.float32) / hd, dtype=BF16)


def _dft_cos_sin(n):
    k = jnp.arange(n, dtype=jnp.int32)
    m = (k[:, None] * k[None, :]) % n
    ang = m.astype(F32) * (2.0 * math.pi / n)
    return jnp.cos(ang), jnp.sin(ang)


def _dft_matrix(n):
    c, s = _dft_cos_sin(n)
    return jnp.concatenate([c, -s], axis=1).astype(BF16)


def _bias_expand_kernel(rpb_ref, e_ref, o_ref):
    o_ref[...] = LOG2E * jnp.dot(rpb_ref[...], e_ref[...], preferred_element_type=F32,
                                 precision=lax.Precision.HIGHEST)


def _na_bias_tables(rpb):
    L, H, n_dr, n_dc = rpb.shape
    c = np.arange(GRID_W)
    c0 = np.clip(c - NA_COLS // 2, 0, GRID_W - NA_COLS)
    kc = np.arange(GRID_W)
    inwin = (kc[None, :] >= c0[:, None]) & (kc[None, :] < c0[:, None] + NA_COLS)
    dc = kc[None, :] - c[:, None] + (NA_COLS - 1)
    n_dc_pad = 32
    onehot = (np.arange(n_dc_pad)[:, None, None] == dc[None]) & inwin[None]
    e2d = jnp.asarray(onehot.reshape(n_dc_pad, GRID_W * GRID_W).astype(np.float32))
    rpb2d = jnp.pad(rpb.reshape(L, H * n_dr, n_dc), ((0, 0), (0, 0), (0, n_dc_pad - n_dc)))
    t = pl.pallas_call(
        _bias_expand_kernel,
        grid=(L,),
        in_specs=[pl.BlockSpec((None, H * n_dr, n_dc_pad), lambda l: (l, 0, 0)),
                  pl.BlockSpec((n_dc_pad, GRID_W * GRID_W), lambda l: (0, 0))],
        out_specs=pl.BlockSpec((None, H * n_dr, GRID_W * GRID_W), lambda l: (l, 0, 0)),
        out_shape=jax.ShapeDtypeStruct((L, H * n_dr, GRID_W * GRID_W), F32),
        compiler_params=_cparams(1),
        name="bias_expand",
    )(rpb2d, e2d)
    t = t.reshape(L, H, n_dr, GRID_W, GRID_W)
    t = jnp.where(jnp.asarray(inwin)[None, None, None], t, NEG)
    t = t.transpose(0, 1, 3, 2, 4).reshape(L, H, GRID_W, n_dr * GRID_W)
    nk = NA_ROWS * GRID_W
    cases = [t[..., (NA_ROWS - 1 - d) * GRID_W:(NA_ROWS - 1 - d) * GRID_W + nk] for d in range(NA_ROWS)]
    return jnp.stack(cases, axis=1)


def kernel(x, c, ctx, c_ctx, norm_g, w_ada, b_ada, w_in, a_norm_g, a_w_s, a_b_s, b_q_g, b_k_g, b_sink,
           c_w_f, d_q_g, d_k_g, d_rpb, w_gate, w_branch, w_out):
    B, S, D = x.shape
    C = ctx.shape[1]
    L = norm_g.shape[0]
    assert S % 512 == 0 and S // GRID_W >= NA_ROWS and S >= 3 * B_BLK and C % CHUNK == 0
    tm, tmc = 512, C
    tm_ctx = math.gcd(B * C, 512)
    cn = 512
    tm_in = 512
    W = BRANCH_W

    n_rows = -(-(B + 1) // 8) * 8
    cc = jnp.concatenate([c, c_ctx[None], jnp.zeros((n_rows - B - 1, D), F32)], axis=0)
    mod = _modulation(cc, w_ada, b_ada).reshape(L, n_rows, 1, 3 * D)
    lat_row = lambda b: b
    ctx_row = lambda b: B

    w_in_r = _rearrange_w_in(w_in)
    w_gate_b, w_branch_b, w_out_b = w_gate.astype(BF16), w_branch.astype(BF16), w_out.astype(BF16)
    a_ws_b = a_w_s.astype(BF16)
    gains = jnp.stack([jnp.tile(b_q_g, (1, W // B_HD)),
                       jnp.tile(b_k_g, (1, W // B_HD)),
                       jnp.tile(d_q_g, (1, W // D_HD)),
                       jnp.tile(d_k_g, (1, W // D_HD))] + [jnp.zeros((L, W), F32)] * 4, axis=1)
    bd64, bd32 = _block_diag_mean(W, B_HD), _block_diag_mean(W, D_HD)
    ropes = _rope_tables(S)
    c128, s128 = _dft_cos_sin(C_GW)
    dft_lat, dft_ctx = _dft_matrix(S), _dft_matrix(C)
    bias_tabs = _na_bias_tables(d_rpb)

    h, hc = x, ctx
    for l in range(L):
        last = l == L - 1
        mod_l = mod[l]
        ng = norm_g[l][None]
        n_lat, z = _inproj(h, mod_l, lat_row, ng, w_in_r[l], gains[l], bd64, bd32, ropes, tm_in, True)
        n_ctx, zc = _inproj(hc.reshape(1, B * C, D), mod_l, ctx_row, ng, w_in_r[l], gains[l], bd64, bd32,
                            ropes, tm_ctx, False)
        zc = zc.reshape(B, C, Z_WIDTH)
        cw, sw = _dft_weights(c128, s128, c_w_f[l])
        g_v = a_norm_g[l][None]
        b_sT = a_b_s[l].T
        bias_tab = bias_tabs[l]

        u_lat = [
            _branch_a(z, g_v, a_ws_b[l], b_sT, tm),
            _branch_b(z, zc, b_sink[l]),
            _fourier_mix(dft_lat, _fourier_pq(z, cw, sw, tm).reshape(B, 2 * S, W), z, tm),
            _branch_d(z, zc, bias_tab),
        ]
        h_new = _merge(h, mod_l, lat_row, n_lat, u_lat, w_gate_b[l], w_branch_b[l], w_out_b[l], tm, cn)
        if not last:
            u_ctx = [
                _branch_a(zc, g_v, a_ws_b[l], b_sT, tmc),
                _ctx_b(zc, b_sink[l]),
                _fourier_mix(dft_ctx, _fourier_pq(zc, cw, sw, tmc).reshape(B, 2 * C, W), zc, tmc),
                _ctx_d(zc),
            ]
            u_ctx = [u.reshape(1, B * C, W) for u in u_ctx]
            hc = _merge(hc.reshape(1, B * C, D), mod_l, ctx_row, n_ctx, u_ctx,
                        w_gate_b[l], w_branch_b[l], w_out_b[l], tm_ctx, cn).reshape(B, C, D)
        h = h_new
    return h
```

```python
import functools
import math

import jax
import jax.numpy as jnp
import numpy as np
from jax import lax
from jax.experimental import pallas as pl
from jax.experimental.pallas import tpu as pltpu

F32 = jnp.float32
BF16 = jnp.bfloat16

LANES = 128
V7X_VMEM_LIMIT_BYTES = 56 * 1024 * 1024

GRID_W = 64
N_BRANCH = 4
BRANCH_W = 512
CHUNK = 128
A_GROUPS = 4
B_HD = 64
B_QH = 8
B_KVH = 2
B_WIN = 128
B_BLK = 128
C_GROUPS = 4
C_GW = 128
D_HD = 32
D_H = 16
NA_ROWS = 8
NA_COLS = 16
ROPE_BASE = 10000.0
EPS = 1e-6
NEG = -1e30
LOG2E = math.log2(math.e)

(COL_AU, COL_AV, COL_AZ, COL_BQ, COL_BKV, COL_BZ,
 COL_FIN, COL_FZ, COL_DQ, COL_DK, COL_DV, COL_DZ) = range(12)
Z_WIDTH = 12 * BRANCH_W


def _cparams(n_grid, flags=None):
    return pltpu.CompilerParams(
        dimension_semantics=("arbitrary",) * n_grid,
        vmem_limit_bytes=V7X_VMEM_LIMIT_BYTES,
        flags=flags)


def _sigmoid(x):
    return 1.0 / (1.0 + jnp.exp(-x))


def _silu(x):
    return x * _sigmoid(x)


def _gelu(x):
    return 0.5 * x * (1.0 + jnp.tanh(math.sqrt(2.0 / math.pi) * (x + 0.044715 * (x * x * x))))


def _mod_kernel(c_ref, w_ref, b_ref, o_ref):
    c = c_ref[...]
    a = _silu(c).astype(BF16)
    o_ref[...] = jnp.dot(a, w_ref[...].astype(BF16), preferred_element_type=F32) + b_ref[...]


def _modulation(cc, w_ada, b_ada):
    L, D, N = w_ada.shape
    R = cc.shape[0]
    tn = 512
    return pl.pallas_call(
        _mod_kernel,
        grid=(L, N // tn),
        in_specs=[pl.BlockSpec((R, D), lambda l, j: (0, 0)),
                  pl.BlockSpec((None, D, tn), lambda l, j: (l, 0, j)),
                  pl.BlockSpec((None, 1, tn), lambda l, j: (l, 0, j))],
        out_specs=pl.BlockSpec((None, R, tn), lambda l, j: (l, 0, j)),
        out_shape=jax.ShapeDtypeStruct((L, R, N), F32),
        compiler_params=_cparams(2),
        name="modulation",
    )(cc, w_ada, b_ada.reshape(L, 1, N))


def _head_rms(x, bd_ref, width):
    sq = x * x
    hi = sq.astype(BF16)
    lo = (sq - hi.astype(F32)).astype(BF16)
    bd = bd_ref[0:width, 0:width]
    return (jnp.dot(hi, bd, preferred_element_type=F32)
            + jnp.dot(lo, bd, preferred_element_type=F32))


def _rope(x, cos, sina, sinb):
    w = x.shape[-1]
    reps = w // LANES
    cos = jnp.concatenate([cos] * reps, axis=1)
    sina = jnp.concatenate([sina] * reps, axis=1)
    sinb = jnp.concatenate([sinb] * reps, axis=1)
    up = pltpu.roll(x, w - 16, 1)
    dn = pltpu.roll(x, 16, 1)
    return x * cos + up * sina + dn * sinb


def _inproj_kernel(h_ref, mod_ref, ng_ref, w_ref, gains_ref, bd64_ref, bd32_ref,
                   cos_ref, sina_ref, sinb_ref, n_ref, z_ref, *, rope):
    D = h_ref.shape[-1]
    W = BRANCH_W
    x = h_ref[...]
    ms = jnp.mean(x * x, axis=-1, keepdims=True)
    y = x * lax.rsqrt(ms + EPS) * ng_ref[...]
    nb = (y * (1.0 + mod_ref[:, D:2 * D]) + mod_ref[:, 0:D]).astype(BF16)
    n_ref[...] = nb

    def maybe_rope(t):
        if rope:
            return _rope(t, cos_ref[...], sina_ref[...], sinb_ref[...])
        return t

    def head_norm(t, bd_ref, gain):
        return t * lax.rsqrt(_head_rms(t, bd_ref, t.shape[-1]) + EPS) * gain

    for cb in range(Z_WIDTH // W):
        cols = slice(cb * W, (cb + 1) * W)
        acc = jnp.dot(nb, w_ref[:, cols], preferred_element_type=F32)
        if cb == COL_BQ:
            acc = maybe_rope(head_norm(acc, bd64_ref, gains_ref[0:1, :])) * (B_HD ** -0.5 * LOG2E)
        elif cb == COL_BKV:
            k = maybe_rope(head_norm(acc[:, 0:W // 2], bd64_ref, gains_ref[1:2, 0:W // 2]))
            acc = jnp.concatenate([k, acc[:, W // 2:]], axis=1)
        elif cb == COL_DQ:
            acc = head_norm(acc, bd32_ref, gains_ref[2:3, :]) * (D_HD ** -0.5 * LOG2E)
        elif cb == COL_DK:
            acc = head_norm(acc, bd32_ref, gains_ref[3:4, :])
        z_ref[:, cols] = acc.astype(BF16)


def _inproj(h, mod_l, mod_row, ng, w, gains, bd64, bd32, ropes, tm, rope):
    B, S, D = h.shape
    cos, sina, sinb = ropes
    const = lambda shape: pl.BlockSpec(shape, lambda b, i: (0,) * len(shape), pipeline_mode=pl.Buffered(1))
    rope_spec = pl.BlockSpec((tm, LANES), lambda b, i: (i if rope else 0, 0))
    return pl.pallas_call(
        functools.partial(_inproj_kernel, rope=rope),
        grid=(B, S // tm),
        in_specs=[pl.BlockSpec((None, tm, D), lambda b, i: (b, i, 0)),
                  pl.BlockSpec((None, 1, 3 * D), lambda b, i: (mod_row(b), 0, 0)),
                  const((1, D)),
                  const((D, Z_WIDTH)),
                  const((8, BRANCH_W)),
                  const((BRANCH_W, BRANCH_W)),
                  const((BRANCH_W, BRANCH_W)),
                  rope_spec, rope_spec, rope_spec],
        out_specs=[pl.BlockSpec((None, tm, D), lambda b, i: (b, i, 0)),
                   pl.BlockSpec((None, tm, Z_WIDTH), lambda b, i: (b, i, 0))],
        out_shape=[jax.ShapeDtypeStruct((B, S, D), BF16),
                   jax.ShapeDtypeStruct((B, S, Z_WIDTH), BF16)],
        compiler_params=_cparams(2),
        name="inproj",
    )(h, mod_l, ng, w, gains, bd64, bd32, cos, sina, sinb)


def _branch_a_kernel(au_ref, av_ref, az_ref, g_ref, ws_ref, bs_ref, o_ref):
    tm = au_ref.shape[0]
    v = _gelu(av_ref[...].astype(F32))
    ms = jnp.mean(v * v, axis=-1, keepdims=True)
    vb = (v * lax.rsqrt(ms + EPS) * g_ref[...]).astype(BF16)
    gate = _gelu(au_ref[...].astype(F32)) * _silu(az_ref[...].astype(F32))
    for c in range(tm // CHUNK):
        rows = slice(c * CHUNK, (c + 1) * CHUNK)
        for g in range(A_GROUPS):
            cols = slice(g * LANES, (g + 1) * LANES)
            mixed = jnp.dot(ws_ref[g], vb[rows, cols], preferred_element_type=F32) + bs_ref[:, g:g + 1]
            o_ref[rows, cols] = (gate[rows, cols] * mixed).astype(BF16)


def _branch_a(z, g_v, w_s, b_sT, tm):
    B, S, _ = z.shape
    W = BRANCH_W
    col = lambda cb: pl.BlockSpec((None, tm, W), lambda b, i: (b, i, cb))
    return pl.pallas_call(
        _branch_a_kernel,
        grid=(B, S // tm),
        in_specs=[col(COL_AU), col(COL_AV), col(COL_AZ),
                  pl.BlockSpec((1, W), lambda b, i: (0, 0)),
                  pl.BlockSpec((A_GROUPS, CHUNK, CHUNK), lambda b, i: (0, 0, 0)),
                  pl.BlockSpec((CHUNK, A_GROUPS), lambda b, i: (0, 0))],
        out_specs=pl.BlockSpec((None, tm, W), lambda b, i: (b, i, 0)),
        out_shape=jax.ShapeDtypeStruct((B, S, W), BF16),
        compiler_params=_cparams(2),
        name="branch_a",
    )(z, z, z, g_v, w_s, b_sT)


def _dftw_kernel(c_ref, s_ref, w_ref, cw_ref, sw_ref):
    for g in range(C_GROUPS):
        w = w_ref[g]
        cw_ref[g] = jnp.dot(c_ref[...], w, preferred_element_type=F32,
                            precision=lax.Precision.HIGHEST).astype(BF16)
        sw_ref[g] = jnp.dot(s_ref[...], w, preferred_element_type=F32,
                            precision=lax.Precision.HIGHEST).astype(BF16)


def _dft_weights(c128, s128, w_f):
    shp = jax.ShapeDtypeStruct((C_GROUPS, C_GW, C_GW), BF16)
    return pl.pallas_call(_dftw_kernel, out_shape=[shp, shp], name="dft_weights")(c128, s128, w_f)


def _fourier_pq_kernel(xa_ref, xb_ref, xh_ref, sgn_ref, cw_ref, sw_ref, pq_ref, aux_ref, alt_ref):
    i = pl.program_id(1)
    xa = xa_ref[...].astype(F32)
    xb = xb_ref[...].astype(F32)
    xe = (xa + xb).astype(BF16)
    xo = (xa - xb).astype(BF16)
    for g in range(C_GROUPS):
        cols = slice(g * LANES, (g + 1) * LANES)
        pq_ref[0, :, cols] = jnp.dot(xe[:, cols], cw_ref[g], preferred_element_type=F32).astype(BF16)
        pq_ref[1, :, cols] = jnp.dot(xo[:, cols], sw_ref[g], preferred_element_type=F32).astype(BF16)
    part = jnp.dot(sgn_ref[...], xe, preferred_element_type=F32)

    @pl.when(i == 0)
    def _():
        alt_ref[...] = part

    @pl.when(i > 0)
    def _():
        alt_ref[...] += part

    @pl.when(i == pl.num_programs(1) - 1)
    def _():
        xh0 = xh_ref[0:1, :].astype(F32)
        row = lax.broadcasted_iota(jnp.int32, (8, 1), 0)
        lhs = jnp.where(row == 0, xh0, jnp.where(row == 1, alt_ref[0:1, :] + xh0, 0.0)).astype(BF16)
        for g in range(C_GROUPS):
            cols = slice(g * LANES, (g + 1) * LANES)
            aux_ref[:, cols] = jnp.dot(lhs[:, cols], cw_ref[g], preferred_element_type=F32)


def _fourier_pq(z, xf, sgn, cw, sw, tm):
    B, L, _ = z.shape
    H = L // 2
    W = BRANCH_W
    wspec = pl.BlockSpec((C_GROUPS, C_GW, C_GW), lambda b, i: (0, 0, 0))
    return pl.pallas_call(
        _fourier_pq_kernel,
        grid=(B, H // tm),
        in_specs=[pl.BlockSpec((None, tm, W), lambda b, i: (b, i, COL_FIN)),
                  pl.BlockSpec((None, tm, W), lambda b, i: (b, i, 0)),
                  pl.BlockSpec((None, 8, W), lambda b, i: (b, H // 8, COL_FIN)),
                  pl.BlockSpec((8, tm), lambda b, i: (0, 0)),
                  wspec, wspec],
        out_specs=[pl.BlockSpec((None, 2, tm, W), lambda b, i: (b, 0, i, 0)),
                   pl.BlockSpec((None, 8, W), lambda b, i: (b, 0, 0))],
        out_shape=[jax.ShapeDtypeStruct((B, 2, H, W), BF16),
                   jax.ShapeDtypeStruct((B, 8, W), F32)],
        scratch_shapes=[pltpu.VMEM((8, W), F32)],
        compiler_params=_cparams(2),
        name="fourier_pq",
    )(z, xf, z, sgn, cw, sw)


def _fourier_mix_kernel(dft_ref, pq_ref, aux_ref, fza_ref, fzb_ref, u1_ref, u2_ref, *, scale, half):
    tm = dft_ref.shape[0]
    ev = jnp.dot(dft_ref[:, 0:half], pq_ref[0:half, :], preferred_element_type=F32)
    od = jnp.dot(dft_ref[:, half:], pq_ref[half:, :], preferred_element_type=F32)
    k = pl.program_id(1) * tm + lax.broadcasted_iota(jnp.int32, (tm, 1), 0)
    nyq = (1 - 2 * (k & 1)).astype(F32) * aux_ref[0:1, :]
    y1 = ev + od + nyq
    y2 = jnp.where(k == 0, aux_ref[1:2, :], ev - od + nyq)
    u1_ref[...] = (y1 * scale * _silu(fza_ref[...].astype(F32))).astype(BF16)
    u2_ref[...] = (y2 * scale * _silu(fzb_ref[...].astype(F32))).astype(BF16)


def _fourier_mix(dft, pq, aux, z, xf, tm):
    B, L, _ = z.shape
    H = L // 2
    W = BRANCH_W
    scale = 1.0 / math.sqrt(L * C_GW)
    half_out = jax.ShapeDtypeStruct((B, H, W), BF16)
    return pl.pallas_call(
        functools.partial(_fourier_mix_kernel, scale=scale, half=H),
        grid=(B, H // tm),
        in_specs=[pl.BlockSpec((tm, L), lambda b, i: (i, 0)),
                  pl.BlockSpec((None, L, W), lambda b, i: (b, 0, 0)),
                  pl.BlockSpec((None, 8, W), lambda b, i: (b, 0, 0)),
                  pl.BlockSpec((None, tm, W), lambda b, i: (b, i, COL_FZ)),
                  pl.BlockSpec((None, tm, W), lambda b, i: (b, i, 1))],
        out_specs=[pl.BlockSpec((None, tm, W), lambda b, i: (b, i, 0)),
                   pl.BlockSpec((None, tm, W), lambda b, i: (b, i, 0))],
        out_shape=[half_out, half_out],
        compiler_params=_cparams(2),
        name="fourier_mix",
    )(dft, pq, aux, z, xf)


def _fourier_branch(z, cw, sw, dft, sgn, tm):
    B, L, _ = z.shape
    H = L // 2
    W = BRANCH_W
    c0, c1 = COL_FIN * W, (COL_FZ + 1) * W
    row0 = jnp.concatenate([jnp.zeros((B, 1, W), BF16), z[:, H:H + 1, COL_FZ * W:c1]], axis=2)
    xf = jnp.concatenate([row0, jnp.flip(z[:, H + 1:, c0:c1], axis=1)], axis=1)
    pq, aux = _fourier_pq(z, xf, sgn, cw, sw, tm)
    u1, u2 = _fourier_mix(dft, pq.reshape(B, L, W), aux, z, xf, tm)
    return jnp.concatenate([u1, u2[:, 0:1], jnp.flip(u2[:, 1:], axis=1)], axis=1)


def _lane_masks(n_heads, hd):
    lane = lax.broadcasted_iota(jnp.int32, (1, LANES), 1)
    return [((lane >= g * hd) & (lane < (g + 1) * hd)) for g in range(n_heads)]


def _attend(q, segs, sink_col, n_heads, hd):
    tq = q.shape[0]
    masks = _lane_masks(n_heads, hd)
    zero = jnp.zeros_like(q)
    qx = jnp.concatenate([jnp.where(m, q, zero) for m in masks], axis=0)
    scores = []
    for k, _, bias in segs:
        s = lax.dot_general(qx, k, (((1,), (1,)), ((), ())), preferred_element_type=F32)
        if bias is not None:
            s = s + bias
        scores.append(s)
    m = scores[0].max(axis=-1, keepdims=True)
    for s in scores[1:]:
        m = jnp.maximum(m, s.max(axis=-1, keepdims=True))
    if sink_col is not None:
        m = jnp.maximum(m, sink_col)
    den = None
    acc = None
    for s, (_, v, _) in zip(scores, segs):
        p = jnp.exp2(s - m)
        d = p.sum(axis=-1, keepdims=True)
        o = jnp.dot(p.astype(BF16), v, preferred_element_type=F32)
        den = d if den is None else den + d
        acc = o if acc is None else acc + o
    if sink_col is not None:
        den = den + jnp.exp2(sink_col - m)
    acc = acc / den
    out = None
    for g, msk in enumerate(masks):
        part = jnp.where(msk, acc[g * tq:(g + 1) * tq], 0.0)
        out = part if out is None else out + part
    return out


def _branch_b_kernel(sink_ref, q_ref, kv_ref, ckv_ref, bz_ref, o_ref, *, seq):
    i = pl.program_id(1)
    nl = 3 * B_BLK
    start = jnp.clip((i - 1) * B_BLK, 0, seq - nl)
    start = pl.multiple_of(start, B_BLK)
    qpos = i * B_BLK + lax.broadcasted_iota(jnp.int32, (B_BLK, nl), 0)
    kpos = start + lax.broadcasted_iota(jnp.int32, (B_BLK, nl), 1)
    bias1 = jnp.where(jnp.abs(kpos - qpos) <= B_WIN, 0.0, NEG).astype(F32)
    bias = jnp.concatenate([bias1, bias1], axis=0)
    half = BRANCH_W // 2
    for grp in range(BRANCH_W // LANES):
        kvh = grp // 2
        kc = slice(kvh * LANES, (kvh + 1) * LANES)
        vc = slice(half + kvh * LANES, half + (kvh + 1) * LANES)
        cols = slice(grp * LANES, (grp + 1) * LANES)
        sink_col = jnp.concatenate(
            [jnp.full((B_BLK, 1), sink_ref[2 * grp + g] * LOG2E, F32) for g in range(2)], axis=0)
        segs = [(kv_ref[pl.ds(start, nl), kc], kv_ref[pl.ds(start, nl), vc], bias),
                (ckv_ref[:, kc], ckv_ref[:, vc], None)]
        o = _attend(q_ref[:, cols], segs, sink_col, 2, B_HD)
        o_ref[:, cols] = (o * _silu(bz_ref[:, cols].astype(F32))).astype(BF16)


def _branch_b(z, zc, sink):
    B, S, _ = z.shape
    C = zc.shape[1]
    W = BRANCH_W
    return pl.pallas_call(
        functools.partial(_branch_b_kernel, seq=S),
        grid=(B, S // B_BLK),
        in_specs=[pl.BlockSpec(memory_space=pltpu.SMEM),
                  pl.BlockSpec((None, B_BLK, W), lambda b, i: (b, i, COL_BQ)),
                  pl.BlockSpec((None, S, W), lambda b, i: (b, 0, COL_BKV)),
                  pl.BlockSpec((None, C, W), lambda b, i: (b, 0, COL_BKV)),
                  pl.BlockSpec((None, B_BLK, W), lambda b, i: (b, i, COL_BZ))],
        out_specs=pl.BlockSpec((None, B_BLK, W), lambda b, i: (b, i, 0)),
        out_shape=jax.ShapeDtypeStruct((B, S, W), BF16),
        compiler_params=_cparams(2),
        name="branch_b",
    )(sink, z, z, zc, z)


def _ctx_b_kernel(sink_ref, q_ref, kv_ref, bz_ref, o_ref):
    tq = q_ref.shape[0]
    half = BRANCH_W // 2
    for grp in range(BRANCH_W // LANES):
        kvh = grp // 2
        kc = slice(kvh * LANES, (kvh + 1) * LANES)
        vc = slice(half + kvh * LANES, half + (kvh + 1) * LANES)
        cols = slice(grp * LANES, (grp + 1) * LANES)
        sink_col = jnp.concatenate(
            [jnp.full((tq, 1), sink_ref[2 * grp + g] * LOG2E, F32) for g in range(2)], axis=0)
        o = _attend(q_ref[:, cols], [(kv_ref[:, kc], kv_ref[:, vc], None)], sink_col, 2, B_HD)
        o_ref[:, cols] = (o * _silu(bz_ref[:, cols].astype(F32))).astype(BF16)


def _ctx_b(zc, sink):
    B, C, _ = zc.shape
    W = BRANCH_W
    col = lambda cb: pl.BlockSpec((None, C, W), lambda b: (b, 0, cb))
    return pl.pallas_call(
        _ctx_b_kernel,
        grid=(B,),
        in_specs=[pl.BlockSpec(memory_space=pltpu.SMEM), col(COL_BQ), col(COL_BKV), col(COL_BZ)],
        out_specs=pl.BlockSpec((None, C, W), lambda b: (b, 0, 0)),
        out_shape=jax.ShapeDtypeStruct((B, C, W), BF16),
        compiler_params=_cparams(1),
        name="ctx_b",
    )(sink, zc, zc, zc)


def _na_row_start(r, rows):
    return jnp.clip(r - NA_ROWS // 2, 0, rows - NA_ROWS)


def _branch_d_kernel(q_ref, k_ref, v_ref, ck_ref, cv_ref, bias_ref, dz_ref, o_ref, *, rows):
    r = pl.program_id(1)
    nk = NA_ROWS * GRID_W
    start = pl.multiple_of(_na_row_start(r, rows) * GRID_W, GRID_W)
    hpg = LANES // D_HD
    for grp in range(BRANCH_W // LANES):
        cols = slice(grp * LANES, (grp + 1) * LANES)
        bias = bias_ref[grp * hpg:(grp + 1) * hpg].reshape(hpg * GRID_W, nk)
        segs = [(k_ref[pl.ds(start, nk), cols], v_ref[pl.ds(start, nk), cols], bias),
                (ck_ref[:, cols], cv_ref[:, cols], None)]
        o = _attend(q_ref[:, cols], segs, None, hpg, D_HD)
        o_ref[:, cols] = (o * _silu(dz_ref[:, cols].astype(F32))).astype(BF16)


def _branch_d(z, zc, bias_tab):
    B, S, _ = z.shape
    C = zc.shape[1]
    W = BRANCH_W
    rows = S // GRID_W
    nk = NA_ROWS * GRID_W
    tile = lambda cb: pl.BlockSpec((None, GRID_W, W), lambda b, r: (b, r, cb))
    full = lambda cb: pl.BlockSpec((None, S, W), lambda b, r: (b, 0, cb))
    ctx = lambda cb: pl.BlockSpec((None, C, W), lambda b, r: (b, 0, cb))
    return pl.pallas_call(
        functools.partial(_branch_d_kernel, rows=rows),
        grid=(B, rows),
        in_specs=[tile(COL_DQ), full(COL_DK), full(COL_DV), ctx(COL_DK), ctx(COL_DV),
                  pl.BlockSpec((None, D_H, GRID_W, nk),
                               lambda b, r: (r - _na_row_start(r, rows), 0, 0, 0)),
                  tile(COL_DZ)],
        out_specs=pl.BlockSpec((None, GRID_W, W), lambda b, r: (b, r, 0)),
        out_shape=jax.ShapeDtypeStruct((B, S, W), BF16),
        compiler_params=_cparams(2),
        name="branch_d",
    )(z, z, z, zc, zc, bias_tab, z)


def _ctx_d_kernel(q_ref, k_ref, v_ref, dz_ref, o_ref):
    hpg = LANES // D_HD
    for grp in range(BRANCH_W // LANES):
        cols = slice(grp * LANES, (grp + 1) * LANES)
        o = _attend(q_ref[:, cols], [(k_ref[:, cols], v_ref[:, cols], None)], None, hpg, D_HD)
        o_ref[:, cols] = (o * _silu(dz_ref[:, cols].astype(F32))).astype(BF16)


def _ctx_d(zc):
    B, C, _ = zc.shape
    W = BRANCH_W
    col = lambda cb: pl.BlockSpec((None, C, W), lambda b: (b, 0, cb))
    return pl.pallas_call(
        _ctx_d_kernel,
        grid=(B,),
        in_specs=[col(COL_DQ), col(COL_DK), col(COL_DV), col(COL_DZ)],
        out_specs=pl.BlockSpec((None, C, W), lambda b: (b, 0, 0)),
        out_shape=jax.ShapeDtypeStruct((B, C, W), BF16),
        compiler_params=_cparams(1),
        name="ctx_d",
    )(zc, zc, zc, zc)


def _merge_kernel(h_ref, mod_ref, n_ref, ua_ref, ub_ref, uc_ref, ud_ref,
                  wg_ref, wb_ref, wo_ref, o_ref):
    D = h_ref.shape[-1]
    j = pl.program_id(2)
    last = pl.num_programs(2) - 1

    n = n_ref[...]
    mix = None
    for b, u_ref in enumerate((ua_ref, ub_ref, uc_ref, ud_ref)):
        gate = _sigmoid(jnp.dot(n, wg_ref[b], preferred_element_type=F32))
        t = gate * jnp.dot(u_ref[...], wb_ref[b], preferred_element_type=F32)
        mix = t if mix is None else mix + t
    part = jnp.dot(mix.astype(BF16), wo_ref[...], preferred_element_type=F32)

    @pl.when(j == 0)
    def _():
        o_ref[...] = part

    @pl.when((j > 0) & (j < last))
    def _():
        o_ref[...] += part

    @pl.when(j == last)
    def _():
        o_ref[...] = h_ref[...] + mod_ref[:, 2 * D:3 * D] * (o_ref[...] + part)


def _merge(h, mod_l, mod_row, n, us, wg, wb, wo, tm, cn):
    B, S, D = h.shape
    W = BRANCH_W
    row = lambda width: pl.BlockSpec((None, tm, width), lambda b, i, j: (b, i, 0))
    assert D // cn >= 2
    return pl.pallas_call(
        _merge_kernel,
        grid=(B, S // tm, D // cn),
        in_specs=[row(D),
                  pl.BlockSpec((None, 1, 3 * D), lambda b, i, j: (mod_row(b), 0, 0)),
                  row(D), row(W), row(W), row(W), row(W),
                  pl.BlockSpec((N_BRANCH, D, cn), lambda b, i, j: (0, 0, j)),
                  pl.BlockSpec((N_BRANCH, W, cn), lambda b, i, j: (0, 0, j)),
                  pl.BlockSpec((cn, D), lambda b, i, j: (j, 0))],
        out_specs=row(D),
        out_shape=jax.ShapeDtypeStruct((B, S, D), F32),
        compiler_params=_cparams(3),
        name="merge",
    )(h, mod_l, n, *us, wg, wb, wo)


def _rearrange_w_in(w_in):
    W = BRANCH_W
    o = 0
    parts = {}
    for name, size in (("a_u", W), ("a_v", W), ("a_z", W), ("b_q", W), ("b_k", B_KVH * B_HD),
                       ("b_v", B_KVH * B_HD), ("b_z", W), ("f_in", W), ("f_z", W),
                       ("d_q", W), ("d_k", W), ("d_v", W), ("d_z", W)):
        parts[name] = w_in[:, :, o:o + size]
        o += size

    def dup(p):
        return jnp.concatenate([p[:, :, 0:B_HD], p[:, :, 0:B_HD], p[:, :, B_HD:], p[:, :, B_HD:]], axis=-1)

    cols = [parts["a_u"], parts["a_v"], parts["a_z"], parts["b_q"], dup(parts["b_k"]), dup(parts["b_v"]),
            parts["b_z"], parts["f_in"], parts["f_z"], parts["d_q"], parts["d_k"], parts["d_v"], parts["d_z"]]
    return jnp.concatenate(cols, axis=-1).astype(BF16)


def _rope_tables(S):
    t = np.arange(S)
    pos = np.stack([t // GRID_W, t % GRID_W], axis=1).astype(np.float32)
    lane = np.arange(B_HD)
    which = lane // (B_HD // 2)
    fi = lane % (B_HD // 4)
    inv = (ROPE_BASE ** (-(fi.astype(np.float32)) / (B_HD // 4))).astype(np.float32)
    ang = pos[:, which] * inv[None, :]
    cos, sin = np.cos(ang), np.sin(ang)
    lower = (lane % (B_HD // 2)) < (B_HD // 4)
    sina = np.where(lower[None, :], -sin, 0.0)
    sinb = np.where(lower[None, :], 0.0, sin)
    rep = lambda a: jnp.asarray(np.tile(a.astype(np.float32), (1, LANES // B_HD)))
    return rep(cos), rep(sina), rep(sinb)


def _block_diag_mean(width, hd):
    i = np.arange(width)
    return jnp.asarray(((i[:, None] // hd) == (i[None, :] // hd)).astype(np.float32) / hd, dtype=BF16)


def _dft_cos_sin(n):
    k = jnp.arange(n, dtype=jnp.int32)
    m = (k[:, None] * k[None, :]) % n
    ang = m.astype(F32) * (2.0 * math.pi / n)
    return jnp.cos(ang), jnp.sin(ang)


def _dft_half_matrix(n):
    k = jnp.arange(n // 2, dtype=jnp.int32)
    ang = ((k[:, None] * k[None, :]) % n).astype(F32) * (2.0 * math.pi / n)
    return jnp.concatenate([jnp.cos(ang), -jnp.sin(ang)], axis=1).astype(BF16)


def _alt_sign_rows(tm):
    s = np.zeros((8, tm), np.float32)
    s[0] = 1.0 - 2.0 * (np.arange(tm) % 2)
    return jnp.asarray(s, dtype=BF16)


def _bias_expand_kernel(rpb_ref, e_ref, o_ref):
    o_ref[...] = LOG2E * jnp.dot(rpb_ref[...], e_ref[...], preferred_element_type=F32,
                                 precision=lax.Precision.HIGHEST)


def _na_bias_tables(rpb):
    L, H, n_dr, n_dc = rpb.shape
    c = np.arange(GRID_W)
    c0 = np.clip(c - NA_COLS // 2, 0, GRID_W - NA_COLS)
    kc = np.arange(GRID_W)
    inwin = (kc[None, :] >= c0[:, None]) & (kc[None, :] < c0[:, None] + NA_COLS)
    dc = kc[None, :] - c[:, None] + (NA_COLS - 1)
    n_dc_pad = 32
    onehot = (np.arange(n_dc_pad)[:, None, None] == dc[None]) & inwin[None]
    e2d = jnp.asarray(onehot.reshape(n_dc_pad, GRID_W * GRID_W).astype(np.float32))
    rpb2d = jnp.pad(rpb.reshape(L, H * n_dr, n_dc), ((0, 0), (0, 0), (0, n_dc_pad - n_dc)))
    t = pl.pallas_call(
        _bias_expand_kernel,
        grid=(L,),
        in_specs=[pl.BlockSpec((None, H * n_dr, n_dc_pad), lambda l: (l, 0, 0)),
                  pl.BlockSpec((n_dc_pad, GRID_W * GRID_W), lambda l: (0, 0))],
        out_specs=pl.BlockSpec((None, H * n_dr, GRID_W * GRID_W), lambda l: (l, 0, 0)),
        out_shape=jax.ShapeDtypeStruct((L, H * n_dr, GRID_W * GRID_W), F32),
        compiler_params=_cparams(1),
        name="bias_expand",
    )(rpb2d, e2d)
    t = t.reshape(L, H, n_dr, GRID_W, GRID_W)
    t = jnp.where(jnp.asarray(inwin)[None, None, None], t, NEG)
    t = t.transpose(0, 1, 3, 2, 4).reshape(L, H, GRID_W, n_dr * GRID_W)
    nk = NA_ROWS * GRID_W
    cases = [t[..., (NA_ROWS - 1 - d) * GRID_W:(NA_ROWS - 1 - d) * GRID_W + nk] for d in range(NA_ROWS)]
    return jnp.stack(cases, axis=1)


def kernel(x, c, ctx, c_ctx, norm_g, w_ada, b_ada, w_in, a_norm_g, a_w_s, a_b_s, b_q_g, b_k_g, b_sink,
           c_w_f, d_q_g, d_k_g, d_rpb, w_gate, w_branch, w_out):
    B, S, D = x.shape
    C = ctx.shape[1]
    L = norm_g.shape[0]
    assert S % 512 == 0 and S // GRID_W >= NA_ROWS and S >= 3 * B_BLK and C % CHUNK == 0
    tm, tmc = 512, C
    tm_ctx = math.gcd(B * C, 512)
    cn = 512
    tm_in = 512
    W = BRANCH_W

    n_rows = -(-(B + 1) // 8) * 8
    cc = jnp.concatenate([c, c_ctx[None], jnp.zeros((n_rows - B - 1, D), F32)], axis=0)
    mod = _modulation(cc, w_ada, b_ada).reshape(L, n_rows, 1, 3 * D)
    lat_row = lambda b: b
    ctx_row = lambda b: B

    w_in_r = _rearrange_w_in(w_in)
    w_gate_b, w_branch_b, w_out_b = w_gate.astype(BF16), w_branch.astype(BF16), w_out.astype(BF16)
    a_ws_b = a_w_s.astype(BF16)
    gains = jnp.stack([jnp.tile(b_q_g, (1, W // B_HD)),
                       jnp.tile(b_k_g, (1, W // B_HD)),
                       jnp.tile(d_q_g, (1, W // D_HD)),
                       jnp.tile(d_k_g, (1, W // D_HD))] + [jnp.zeros((L, W), F32)] * 4, axis=1)
    bd64, bd32 = _block_diag_mean(W, B_HD), _block_diag_mean(W, D_HD)
    ropes = _rope_tables(S)
    c128, s128 = _dft_cos_sin(C_GW)
    tm_f, tm_fc = min(512, S // 2), min(512, C // 2)
    dft_lat, dft_ctx = _dft_half_matrix(S), _dft_half_matrix(C)
    sgn_lat, sgn_ctx = _alt_sign_rows(tm_f), _alt_sign_rows(tm_fc)
    bias_tabs = _na_bias_tables(d_rpb)

    h, hc = x, ctx
    for l in range(L):
        last = l == L - 1
        mod_l = mod[l]
        ng = norm_g[l][None]
        n_lat, z = _inproj(h, mod_l, lat_row, ng, w_in_r[l], gains[l], bd64, bd32, ropes, tm_in, True)
        n_ctx, zc = _inproj(hc.reshape(1, B * C, D), mod_l, ctx_row, ng, w_in_r[l], gains[l], bd64, bd32,
                            ropes, tm_ctx, False)
        zc = zc.reshape(B, C, Z_WIDTH)
        cw, sw = _dft_weights(c128, s128, c_w_f[l])
        g_v = a_norm_g[l][None]
        b_sT = a_b_s[l].T
        bias_tab = bias_tabs[l]

        u_lat = [
            _branch_a(z, g_v, a_ws_b[l], b_sT, tm),
            _branch_b(z, zc, b_sink[l]),
            _fourier_branch(z, cw, sw, dft_lat, sgn_lat, tm_f),
            _branch_d(z, zc, bias_tab),
        ]
        h_new = _merge(h, mod_l, lat_row, n_lat, u_lat, w_gate_b[l], w_branch_b[l], w_out_b[l], tm, cn)
        if not last:
            u_ctx = [
                _branch_a(zc, g_v, a_ws_b[l], b_sT, tmc),
                _ctx_b(zc, b_sink[l]),
                _fourier_branch(zc, cw, sw, dft_ctx, sgn_ctx, tm_fc),
                _ctx_d(zc),
            ]
            u_ctx = [u.reshape(1, B * C, W) for u in u_ctx]
            hc = _merge(hc.reshape(1, B * C, D), mod_l, ctx_row, n_ctx, u_ctx,
                        w_gate_b[l], w_branch_b[l], w_out_b[l], tm_ctx, cn).reshape(B, C, D)
        h = h_new
    return h
```

```python
import functools
import math

import jax
import jax.numpy as jnp
import numpy as np
from jax import lax
from jax.experimental import pallas as pl
from jax.experimental.pallas import tpu as pltpu

F32 = jnp.float32
BF16 = jnp.bfloat16

LANES = 128
V7X_VMEM_LIMIT_BYTES = 56 * 1024 * 1024

GRID_W = 64
N_BRANCH = 4
BRANCH_W = 512
CHUNK = 128
A_GROUPS = 4
B_HD = 64
B_QH = 8
B_KVH = 2
B_WIN = 128
B_BLK = 128
C_GROUPS = 4
C_GW = 128
D_HD = 32
D_H = 16
NA_ROWS = 8
NA_COLS = 16
ROPE_BASE = 10000.0
EPS = 1e-6
NEG = -1e30
LOG2E = math.log2(math.e)

(COL_AU, COL_AV, COL_AZ, COL_BQ, COL_BKV, COL_BZ,
 COL_FIN, COL_FZ, COL_DQ, COL_DK, COL_DV, COL_DZ) = range(12)
Z_WIDTH = 12 * BRANCH_W


def _cparams(n_grid, flags=None):
    return pltpu.CompilerParams(
        dimension_semantics=("arbitrary",) * n_grid,
        vmem_limit_bytes=V7X_VMEM_LIMIT_BYTES,
        flags=flags)


def _sigmoid(x):
    return 1.0 / (1.0 + jnp.exp(-x))


def _silu(x):
    return x * _sigmoid(x)


def _gelu(x):
    return 0.5 * x * (1.0 + jnp.tanh(math.sqrt(2.0 / math.pi) * (x + 0.044715 * (x * x * x))))


def _mod_kernel(c_ref, w_ref, b_ref, o_ref):
    c = c_ref[...]
    a = _silu(c).astype(BF16)
    o_ref[...] = jnp.dot(a, w_ref[...].astype(BF16), preferred_element_type=F32) + b_ref[...]


def _modulation(cc, w_ada, b_ada):
    L, D, N = w_ada.shape
    R = cc.shape[0]
    tn = 512
    return pl.pallas_call(
        _mod_kernel,
        grid=(L, N // tn),
        in_specs=[pl.BlockSpec((R, D), lambda l, j: (0, 0)),
                  pl.BlockSpec((None, D, tn), lambda l, j: (l, 0, j)),
                  pl.BlockSpec((None, 1, tn), lambda l, j: (l, 0, j))],
        out_specs=pl.BlockSpec((None, R, tn), lambda l, j: (l, 0, j)),
        out_shape=jax.ShapeDtypeStruct((L, R, N), F32),
        compiler_params=_cparams(2),
        name="modulation",
    )(cc, w_ada, b_ada.reshape(L, 1, N))


def _head_rms(x, bd_ref, width):
    sq = x * x
    hi = sq.astype(BF16)
    lo = (sq - hi.astype(F32)).astype(BF16)
    bd = bd_ref[0:width, 0:width]
    return (jnp.dot(hi, bd, preferred_element_type=F32)
            + jnp.dot(lo, bd, preferred_element_type=F32))


def _rope(x, cos, sina, sinb):
    w = x.shape[-1]
    reps = w // LANES
    cos = jnp.concatenate([cos] * reps, axis=1)
    sina = jnp.concatenate([sina] * reps, axis=1)
    sinb = jnp.concatenate([sinb] * reps, axis=1)
    up = pltpu.roll(x, w - 16, 1)
    dn = pltpu.roll(x, 16, 1)
    return x * cos + up * sina + dn * sinb


def _inproj_kernel(h_ref, mod_ref, ng_ref, w_ref, gains_ref, bd64_ref, bd32_ref,
                   cos_ref, sina_ref, sinb_ref, n_ref, z_ref, *, rope):
    D = h_ref.shape[-1]
    W = BRANCH_W
    x = h_ref[...]
    ms = jnp.mean(x * x, axis=-1, keepdims=True)
    y = x * lax.rsqrt(ms + EPS) * ng_ref[...]
    nb = (y * (1.0 + mod_ref[:, D:2 * D]) + mod_ref[:, 0:D]).astype(BF16)
    n_ref[...] = nb

    def maybe_rope(t):
        if rope:
            return _rope(t, cos_ref[...], sina_ref[...], sinb_ref[...])
        return t

    def head_norm(t, bd_ref, gain):
        return t * lax.rsqrt(_head_rms(t, bd_ref, t.shape[-1]) + EPS) * gain

    for cb in range(Z_WIDTH // W):
        cols = slice(cb * W, (cb + 1) * W)
        acc = jnp.dot(nb, w_ref[:, cols], preferred_element_type=F32)
        if cb == COL_BQ:
            acc = maybe_rope(head_norm(acc, bd64_ref, gains_ref[0:1, :])) * (B_HD ** -0.5 * LOG2E)
        elif cb == COL_BKV:
            k = maybe_rope(head_norm(acc[:, 0:W // 2], bd64_ref, gains_ref[1:2, 0:W // 2]))
            acc = jnp.concatenate([k, acc[:, W // 2:]], axis=1)
        elif cb == COL_DQ:
            acc = head_norm(acc, bd32_ref, gains_ref[2:3, :]) * (D_HD ** -0.5 * LOG2E)
        elif cb == COL_DK:
            acc = head_norm(acc, bd32_ref, gains_ref[3:4, :])
        z_ref[:, cols] = acc.astype(BF16)


def _inproj(h, mod_l, mod_row, ng, w, gains, bd64, bd32, ropes, tm, rope):
    B, S, D = h.shape
    cos, sina, sinb = ropes
    const = lambda shape: pl.BlockSpec(shape, lambda b, i: (0,) * len(shape), pipeline_mode=pl.Buffered(1))
    rope_spec = pl.BlockSpec((tm, LANES), lambda b, i: (i if rope else 0, 0))
    return pl.pallas_call(
        functools.partial(_inproj_kernel, rope=rope),
        grid=(B, S // tm),
        in_specs=[pl.BlockSpec((None, tm, D), lambda b, i: (b, i, 0)),
                  pl.BlockSpec((None, 1, 3 * D), lambda b, i: (mod_row(b), 0, 0)),
                  const((1, D)),
                  const((D, Z_WIDTH)),
                  const((8, BRANCH_W)),
                  const((BRANCH_W, BRANCH_W)),
                  const((BRANCH_W, BRANCH_W)),
                  rope_spec, rope_spec, rope_spec],
        out_specs=[pl.BlockSpec((None, tm, D), lambda b, i: (b, i, 0)),
                   pl.BlockSpec((None, tm, Z_WIDTH), lambda b, i: (b, i, 0))],
        out_shape=[jax.ShapeDtypeStruct((B, S, D), BF16),
                   jax.ShapeDtypeStruct((B, S, Z_WIDTH), BF16)],
        compiler_params=_cparams(2),
        name="inproj",
    )(h, mod_l, ng, w, gains, bd64, bd32, cos, sina, sinb)


def _branch_a_kernel(au_ref, av_ref, az_ref, g_ref, ws_ref, bs_ref, o_ref):
    tm = au_ref.shape[0]
    v = _gelu(av_ref[...].astype(F32))
    ms = jnp.mean(v * v, axis=-1, keepdims=True)
    vb = (v * lax.rsqrt(ms + EPS) * g_ref[...]).astype(BF16)
    gate = _gelu(au_ref[...].astype(F32)) * _silu(az_ref[...].astype(F32))
    for c in range(tm // CHUNK):
        rows = slice(c * CHUNK, (c + 1) * CHUNK)
        for g in range(A_GROUPS):
            cols = slice(g * LANES, (g + 1) * LANES)
            mixed = jnp.dot(ws_ref[g], vb[rows, cols], preferred_element_type=F32) + bs_ref[:, g:g + 1]
            o_ref[rows, cols] = (gate[rows, cols] * mixed).astype(BF16)


def _branch_a(z, g_v, w_s, b_sT, tm):
    B, S, _ = z.shape
    W = BRANCH_W
    col = lambda cb: pl.BlockSpec((None, tm, W), lambda b, i: (b, i, cb))
    return pl.pallas_call(
        _branch_a_kernel,
        grid=(B, S // tm),
        in_specs=[col(COL_AU), col(COL_AV), col(COL_AZ),
                  pl.BlockSpec((1, W), lambda b, i: (0, 0)),
                  pl.BlockSpec((A_GROUPS, CHUNK, CHUNK), lambda b, i: (0, 0, 0)),
                  pl.BlockSpec((CHUNK, A_GROUPS), lambda b, i: (0, 0))],
        out_specs=pl.BlockSpec((None, tm, W), lambda b, i: (b, i, 0)),
        out_shape=jax.ShapeDtypeStruct((B, S, W), BF16),
        compiler_params=_cparams(2),
        name="branch_a",
    )(z, z, z, g_v, w_s, b_sT)


def _dftw_kernel(c_ref, s_ref, w_ref, cw_ref, sw_ref):
    for g in range(C_GROUPS):
        w = w_ref[g]
        cw_ref[g] = jnp.dot(c_ref[...], w, preferred_element_type=F32,
                            precision=lax.Precision.HIGHEST).astype(BF16)
        sw_ref[g] = jnp.dot(s_ref[...], w, preferred_element_type=F32,
                            precision=lax.Precision.HIGHEST).astype(BF16)


def _dft_weights(c128, s128, w_f):
    shp = jax.ShapeDtypeStruct((C_GROUPS, C_GW, C_GW), BF16)
    return pl.pallas_call(_dftw_kernel, out_shape=[shp, shp], name="dft_weights")(c128, s128, w_f)


def _fourier_pq_kernel(xa_ref, xm_ref, xn_ref, xh_ref, jm_ref, cw_ref, sw_ref, pq_ref, aux_ref):
    i = pl.program_id(1)
    tm = xa_ref.shape[0]
    row = lax.broadcasted_iota(jnp.int32, (tm, 1), 0)
    first = jnp.where(i > 0, xn_ref[0:1, :].astype(F32), 0.0)
    xb = jnp.where(row == 0, first, jnp.dot(jm_ref[...], xm_ref[...], preferred_element_type=F32))
    xa = xa_ref[...].astype(F32)
    xe = (xa + xb).astype(BF16)
    xo = (xa - xb).astype(BF16)
    for g in range(C_GROUPS):
        cols = slice(g * LANES, (g + 1) * LANES)
        pq_ref[0, :, cols] = jnp.dot(xe[:, cols], cw_ref[g], preferred_element_type=F32).astype(BF16)
        pq_ref[1, :, cols] = jnp.dot(xo[:, cols], sw_ref[g], preferred_element_type=F32).astype(BF16)

    @pl.when(i == 0)
    def _():
        for g in range(C_GROUPS):
            cols = slice(g * LANES, (g + 1) * LANES)
            aux_ref[:, cols] = jnp.dot(xh_ref[:, cols], cw_ref[g], preferred_element_type=F32)


def _fourier_pq(z, jm, cw, sw, tm):
    B, L, _ = z.shape
    H = L // 2
    W = BRANCH_W
    nt = H // tm
    wspec = pl.BlockSpec((C_GROUPS, C_GW, C_GW), lambda b, i: (0, 0, 0))
    return pl.pallas_call(
        _fourier_pq_kernel,
        grid=(B, nt),
        in_specs=[pl.BlockSpec((None, tm, W), lambda b, i: (b, i, COL_FIN)),
                  pl.BlockSpec((None, tm, W), lambda b, i: (b, 2 * nt - 1 - i, COL_FIN)),
                  pl.BlockSpec((None, 8, W),
                               lambda b, i: (b, jnp.minimum((L - i * tm) // 8, L // 8 - 1), COL_FIN)),
                  pl.BlockSpec((None, 8, W), lambda b, i: (b, H // 8, COL_FIN)),
                  pl.BlockSpec((tm, tm), lambda b, i: (0, 0)),
                  wspec, wspec],
        out_specs=[pl.BlockSpec((None, 2, tm, W), lambda b, i: (b, 0, i, 0)),
                   pl.BlockSpec((None, 8, W), lambda b, i: (b, 0, 0))],
        out_shape=[jax.ShapeDtypeStruct((B, 2, H, W), BF16),
                   jax.ShapeDtypeStruct((B, 8, W), F32)],
        compiler_params=_cparams(2),
        name="fourier_pq",
    )(z, z, z, z, jm, cw, sw)


def _fourier_mix_kernel(dft_ref, dftx_ref, pq_ref, aux_ref, jm_ref, fza_ref, fzb_ref, u_ref, *, scale, half):
    tm = dft_ref.shape[0]
    pe = pq_ref[0:half, :]
    qo = pq_ref[half:, :]
    ev = jnp.dot(dft_ref[:, 0:half], pe, preferred_element_type=F32)
    od = jnp.dot(dft_ref[:, half:], qo, preferred_element_type=F32)
    row = lax.broadcasted_iota(jnp.int32, (tm, 1), 0)
    nyq = (1 - 2 * (row & 1)).astype(F32) * aux_ref[0:1, :]
    y1 = ev + od + nyq
    y2 = ev - od + nyq
    evx = jnp.dot(dftx_ref[:, 0:half], pe, preferred_element_type=F32)
    odx = jnp.dot(dftx_ref[:, half:], qo, preferred_element_type=F32)
    y2x = evx[0:1, :] - odx[0:1, :] + aux_ref[0:1, :]
    hi = y2.astype(BF16)
    lo = (y2 - hi.astype(F32)).astype(BF16)
    y2r = (jnp.dot(jm_ref[...], hi, preferred_element_type=F32)
           + jnp.dot(jm_ref[...], lo, preferred_element_type=F32))
    y2r = jnp.where(row == 0, y2x, y2r)
    u_ref[0] = (y1 * scale * _silu(fza_ref[...].astype(F32))).astype(BF16)
    u_ref[1] = (y2r * scale * _silu(fzb_ref[...].astype(F32))).astype(BF16)


def _fourier_mix(dft, pq, aux, jm, z, tm):
    B, L, _ = z.shape
    H = L // 2
    W = BRANCH_W
    nt = H // tm
    scale = 1.0 / math.sqrt(L * C_GW)
    return pl.pallas_call(
        functools.partial(_fourier_mix_kernel, scale=scale, half=H),
        grid=(B, nt),
        in_specs=[pl.BlockSpec((tm, L), lambda b, i: (i, 0)),
                  pl.BlockSpec((8, L), lambda b, i: ((i + 1) * (tm // 8), 0)),
                  pl.BlockSpec((None, L, W), lambda b, i: (b, 0, 0)),
                  pl.BlockSpec((None, 8, W), lambda b, i: (b, 0, 0)),
                  pl.BlockSpec((tm, tm), lambda b, i: (0, 0)),
                  pl.BlockSpec((None, tm, W), lambda b, i: (b, i, COL_FZ)),
                  pl.BlockSpec((None, tm, W), lambda b, i: (b, 2 * nt - 1 - i, COL_FZ))],
        out_specs=pl.BlockSpec((None, 2, tm, W), lambda b, i: (b, 0, i, 0)),
        out_shape=jax.ShapeDtypeStruct((B, 2, H, W), BF16),
        compiler_params=_cparams(2),
        name="fourier_mix",
    )(dft, dft, pq, aux, jm, z, z)


def _fourier_branch(z, cw, sw, dft, jm, tm):
    B, L, _ = z.shape
    pq, aux = _fourier_pq(z, jm, cw, sw, tm)
    return _fourier_mix(dft, pq.reshape(B, L, BRANCH_W), aux, jm, z, tm)


def _lane_masks(n_heads, hd):
    lane = lax.broadcasted_iota(jnp.int32, (1, LANES), 1)
    return [((lane >= g * hd) & (lane < (g + 1) * hd)) for g in range(n_heads)]


def _attend(q, segs, sink_col, n_heads, hd):
    tq = q.shape[0]
    masks = _lane_masks(n_heads, hd)
    zero = jnp.zeros_like(q)
    qx = jnp.concatenate([jnp.where(m, q, zero) for m in masks], axis=0)
    scores = []
    for k, _, bias in segs:
        s = lax.dot_general(qx, k, (((1,), (1,)), ((), ())), preferred_element_type=F32)
        if bias is not None:
            s = s + bias
        scores.append(s)
    m = scores[0].max(axis=-1, keepdims=True)
    for s in scores[1:]:
        m = jnp.maximum(m, s.max(axis=-1, keepdims=True))
    if sink_col is not None:
        m = jnp.maximum(m, sink_col)
    den = None
    acc = None
    for s, (_, v, _) in zip(scores, segs):
        p = jnp.exp2(s - m)
        d = p.sum(axis=-1, keepdims=True)
        o = jnp.dot(p.astype(BF16), v, preferred_element_type=F32)
        den = d if den is None else den + d
        acc = o if acc is None else acc + o
    if sink_col is not None:
        den = den + jnp.exp2(sink_col - m)
    acc = acc / den
    out = None
    for g, msk in enumerate(masks):
        part = jnp.where(msk, acc[g * tq:(g + 1) * tq], 0.0)
        out = part if out is None else out + part
    return out


def _branch_b_kernel(sink_ref, q_ref, kv_ref, ckv_ref, bz_ref, o_ref, *, seq):
    i = pl.program_id(1)
    nl = 3 * B_BLK
    start = jnp.clip((i - 1) * B_BLK, 0, seq - nl)
    start = pl.multiple_of(start, B_BLK)
    qpos = i * B_BLK + lax.broadcasted_iota(jnp.int32, (B_BLK, nl), 0)
    kpos = start + lax.broadcasted_iota(jnp.int32, (B_BLK, nl), 1)
    bias1 = jnp.where(jnp.abs(kpos - qpos) <= B_WIN, 0.0, NEG).astype(F32)
    bias = jnp.concatenate([bias1, bias1], axis=0)
    half = BRANCH_W // 2
    for grp in range(BRANCH_W // LANES):
        kvh = grp // 2
        kc = slice(kvh * LANES, (kvh + 1) * LANES)
        vc = slice(half + kvh * LANES, half + (kvh + 1) * LANES)
        cols = slice(grp * LANES, (grp + 1) * LANES)
        sink_col = jnp.concatenate(
            [jnp.full((B_BLK, 1), sink_ref[2 * grp + g] * LOG2E, F32) for g in range(2)], axis=0)
        segs = [(kv_ref[pl.ds(start, nl), kc], kv_ref[pl.ds(start, nl), vc], bias),
                (ckv_ref[:, kc], ckv_ref[:, vc], None)]
        o = _attend(q_ref[:, cols], segs, sink_col, 2, B_HD)
        o_ref[:, cols] = (o * _silu(bz_ref[:, cols].astype(F32))).astype(BF16)


def _branch_b(z, zc, sink):
    B, S, _ = z.shape
    C = zc.shape[1]
    W = BRANCH_W
    return pl.pallas_call(
        functools.partial(_branch_b_kernel, seq=S),
        grid=(B, S // B_BLK),
        in_specs=[pl.BlockSpec(memory_space=pltpu.SMEM),
                  pl.BlockSpec((None, B_BLK, W), lambda b, i: (b, i, COL_BQ)),
                  pl.BlockSpec((None, S, W), lambda b, i: (b, 0, COL_BKV)),
                  pl.BlockSpec((None, C, W), lambda b, i: (b, 0, COL_BKV)),
                  pl.BlockSpec((None, B_BLK, W), lambda b, i: (b, i, COL_BZ))],
        out_specs=pl.BlockSpec((None, B_BLK, W), lambda b, i: (b, i, 0)),
        out_shape=jax.ShapeDtypeStruct((B, S, W), BF16),
        compiler_params=_cparams(2),
        name="branch_b",
    )(sink, z, z, zc, z)


def _ctx_b_kernel(sink_ref, q_ref, kv_ref, bz_ref, o_ref):
    tq = q_ref.shape[0]
    half = BRANCH_W // 2
    for grp in range(BRANCH_W // LANES):
        kvh = grp // 2
        kc = slice(kvh * LANES, (kvh + 1) * LANES)
        vc = slice(half + kvh * LANES, half + (kvh + 1) * LANES)
        cols = slice(grp * LANES, (grp + 1) * LANES)
        sink_col = jnp.concatenate(
            [jnp.full((tq, 1), sink_ref[2 * grp + g] * LOG2E, F32) for g in range(2)], axis=0)
        o = _attend(q_ref[:, cols], [(kv_ref[:, kc], kv_ref[:, vc], None)], sink_col, 2, B_HD)
        o_ref[:, cols] = (o * _silu(bz_ref[:, cols].astype(F32))).astype(BF16)


def _ctx_b(zc, sink):
    B, C, _ = zc.shape
    W = BRANCH_W
    col = lambda cb: pl.BlockSpec((None, C, W), lambda b: (b, 0, cb))
    return pl.pallas_call(
        _ctx_b_kernel,
        grid=(B,),
        in_specs=[pl.BlockSpec(memory_space=pltpu.SMEM), col(COL_BQ), col(COL_BKV), col(COL_BZ)],
        out_specs=pl.BlockSpec((None, C, W), lambda b: (b, 0, 0)),
        out_shape=jax.ShapeDtypeStruct((B, C, W), BF16),
        compiler_params=_cparams(1),
        name="ctx_b",
    )(sink, zc, zc, zc)


def _na_row_start(r, rows):
    return jnp.clip(r - NA_ROWS // 2, 0, rows - NA_ROWS)


def _branch_d_kernel(q_ref, k_ref, v_ref, ck_ref, cv_ref, bias_ref, dz_ref, o_ref, *, rows):
    r = pl.program_id(1)
    nk = NA_ROWS * GRID_W
    start = pl.multiple_of(_na_row_start(r, rows) * GRID_W, GRID_W)
    hpg = LANES // D_HD
    for grp in range(BRANCH_W // LANES):
        cols = slice(grp * LANES, (grp + 1) * LANES)
        bias = bias_ref[grp * hpg:(grp + 1) * hpg].reshape(hpg * GRID_W, nk)
        segs = [(k_ref[pl.ds(start, nk), cols], v_ref[pl.ds(start, nk), cols], bias),
                (ck_ref[:, cols], cv_ref[:, cols], None)]
        o = _attend(q_ref[:, cols], segs, None, hpg, D_HD)
        o_ref[:, cols] = (o * _silu(dz_ref[:, cols].astype(F32))).astype(BF16)


def _branch_d(z, zc, bias_tab):
    B, S, _ = z.shape
    C = zc.shape[1]
    W = BRANCH_W
    rows = S // GRID_W
    nk = NA_ROWS * GRID_W
    tile = lambda cb: pl.BlockSpec((None, GRID_W, W), lambda b, r: (b, r, cb))
    full = lambda cb: pl.BlockSpec((None, S, W), lambda b, r: (b, 0, cb))
    ctx = lambda cb: pl.BlockSpec((None, C, W), lambda b, r: (b, 0, cb))
    return pl.pallas_call(
        functools.partial(_branch_d_kernel, rows=rows),
        grid=(B, rows),
        in_specs=[tile(COL_DQ), full(COL_DK), full(COL_DV), ctx(COL_DK), ctx(COL_DV),
                  pl.BlockSpec((None, D_H, GRID_W, nk),
                               lambda b, r: (r - _na_row_start(r, rows), 0, 0, 0)),
                  tile(COL_DZ)],
        out_specs=pl.BlockSpec((None, GRID_W, W), lambda b, r: (b, r, 0)),
        out_shape=jax.ShapeDtypeStruct((B, S, W), BF16),
        compiler_params=_cparams(2),
        name="branch_d",
    )(z, z, z, zc, zc, bias_tab, z)


def _ctx_d_kernel(q_ref, k_ref, v_ref, dz_ref, o_ref):
    hpg = LANES // D_HD
    for grp in range(BRANCH_W // LANES):
        cols = slice(grp * LANES, (grp + 1) * LANES)
        o = _attend(q_ref[:, cols], [(k_ref[:, cols], v_ref[:, cols], None)], None, hpg, D_HD)
        o_ref[:, cols] = (o * _silu(dz_ref[:, cols].astype(F32))).astype(BF16)


def _ctx_d(zc):
    B, C, _ = zc.shape
    W = BRANCH_W
    col = lambda cb: pl.BlockSpec((None, C, W), lambda b: (b, 0, cb))
    return pl.pallas_call(
        _ctx_d_kernel,
        grid=(B,),
        in_specs=[col(COL_DQ), col(COL_DK), col(COL_DV), col(COL_DZ)],
        out_specs=pl.BlockSpec((None, C, W), lambda b: (b, 0, 0)),
        out_shape=jax.ShapeDtypeStruct((B, C, W), BF16),
        compiler_params=_cparams(1),
        name="ctx_d",
    )(zc, zc, zc, zc)


def _merge_kernel(h_ref, mod_ref, n_ref, ua_ref, ub_ref, uc_ref, ud_ref,
                  wg_ref, wb_ref, wo_ref, o_ref):
    D = h_ref.shape[-1]
    j = pl.program_id(2)
    last = pl.num_programs(2) - 1

    n = n_ref[...]
    mix = None
    for b, u_ref in enumerate((ua_ref, ub_ref, uc_ref, ud_ref)):
        gate = _sigmoid(jnp.dot(n, wg_ref[b], preferred_element_type=F32))
        t = gate * jnp.dot(u_ref[...], wb_ref[b], preferred_element_type=F32)
        mix = t if mix is None else mix + t
    part = jnp.dot(mix.astype(BF16), wo_ref[...], preferred_element_type=F32)

    @pl.when(j == 0)
    def _():
        o_ref[...] = part

    @pl.when((j > 0) & (j < last))
    def _():
        o_ref[...] += part

    @pl.when(j == last)
    def _():
        o_ref[...] = h_ref[...] + mod_ref[:, 2 * D:3 * D] * (o_ref[...] + part)


def _merge(h, mod_l, mod_row, n, us, wg, wb, wo, tm, cn):
    B, S, D = h.shape
    W = BRANCH_W
    row = lambda width: pl.BlockSpec((None, tm, width), lambda b, i, j: (b, i, 0))
    assert D // cn >= 2
    uc_spec = row(W)
    if us[2].ndim == 4:
        nt = S // (2 * tm)
        uc_spec = pl.BlockSpec((None, None, tm, W),
                               lambda b, i, j: (b, i // nt, jnp.where(i < nt, i, 2 * nt - 1 - i), 0))
    return pl.pallas_call(
        _merge_kernel,
        grid=(B, S // tm, D // cn),
        in_specs=[row(D),
                  pl.BlockSpec((None, 1, 3 * D), lambda b, i, j: (mod_row(b), 0, 0)),
                  row(D), row(W), row(W), uc_spec, row(W),
                  pl.BlockSpec((N_BRANCH, D, cn), lambda b, i, j: (0, 0, j)),
                  pl.BlockSpec((N_BRANCH, W, cn), lambda b, i, j: (0, 0, j)),
                  pl.BlockSpec((cn, D), lambda b, i, j: (j, 0))],
        out_specs=row(D),
        out_shape=jax.ShapeDtypeStruct((B, S, D), F32),
        compiler_params=_cparams(3),
        name="merge",
    )(h, mod_l, n, *us, wg, wb, wo)


def _rearrange_w_in(w_in):
    W = BRANCH_W
    o = 0
    parts = {}
    for name, size in (("a_u", W), ("a_v", W), ("a_z", W), ("b_q", W), ("b_k", B_KVH * B_HD),
                       ("b_v", B_KVH * B_HD), ("b_z", W), ("f_in", W), ("f_z", W),
                       ("d_q", W), ("d_k", W), ("d_v", W), ("d_z", W)):
        parts[name] = w_in[:, :, o:o + size]
        o += size

    def dup(p):
        return jnp.concatenate([p[:, :, 0:B_HD], p[:, :, 0:B_HD], p[:, :, B_HD:], p[:, :, B_HD:]], axis=-1)

    cols = [parts["a_u"], parts["a_v"], parts["a_z"], parts["b_q"], dup(parts["b_k"]), dup(parts["b_v"]),
            parts["b_z"], parts["f_in"], parts["f_z"], parts["d_q"], parts["d_k"], parts["d_v"], parts["d_z"]]
    return jnp.concatenate(cols, axis=-1).astype(BF16)


def _rope_tables(S):
    t = np.arange(S)
    pos = np.stack([t // GRID_W, t % GRID_W], axis=1).astype(np.float32)
    lane = np.arange(B_HD)
    which = lane // (B_HD // 2)
    fi = lane % (B_HD // 4)
    inv = (ROPE_BASE ** (-(fi.astype(np.float32)) / (B_HD // 4))).astype(np.float32)
    ang = pos[:, which] * inv[None, :]
    cos, sin = np.cos(ang), np.sin(ang)
    lower = (lane % (B_HD // 2)) < (B_HD // 4)
    sina = np.where(lower[None, :], -sin, 0.0)
    sinb = np.where(lower[None, :], 0.0, sin)
    rep = lambda a: jnp.asarray(np.tile(a.astype(np.float32), (1, LANES // B_HD)))
    return rep(cos), rep(sina), rep(sinb)


def _block_diag_mean(width, hd):
    i = np.arange(width)
    return jnp.asarray(((i[:, None] // hd) == (i[None, :] // hd)).astype(np.float32) / hd, dtype=BF16)


def _dft_cos_sin(n):
    k = jnp.arange(n, dtype=jnp.int32)
    m = (k[:, None] * k[None, :]) % n
    ang = m.astype(F32) * (2.0 * math.pi / n)
    return jnp.cos(ang), jnp.sin(ang)


def _dft_half_matrix(n):
    k = jnp.arange(n // 2 + 8, dtype=jnp.int32)
    m = jnp.arange(n // 2, dtype=jnp.int32)
    ang = ((k[:, None] * m[None, :]) % n).astype(F32) * (2.0 * math.pi / n)
    return jnp.concatenate([jnp.cos(ang), -jnp.sin(ang)], axis=1).astype(BF16)


def _rev_shift_matrix(tm):
    r = np.arange(tm)
    return jnp.asarray(((r[:, None] >= 1) & (r[None, :] == tm - r[:, None])).astype(np.float32), dtype=BF16)


def _bias_expand_kernel(rpb_ref, e_ref, o_ref):
    o_ref[...] = LOG2E * jnp.dot(rpb_ref[...], e_ref[...], preferred_element_type=F32,
                                 precision=lax.Precision.HIGHEST)


def _na_bias_tables(rpb):
    L, H, n_dr, n_dc = rpb.shape
    c = np.arange(GRID_W)
    c0 = np.clip(c - NA_COLS // 2, 0, GRID_W - NA_COLS)
    kc = np.arange(GRID_W)
    inwin = (kc[None, :] >= c0[:, None]) & (kc[None, :] < c0[:, None] + NA_COLS)
    dc = kc[None, :] - c[:, None] + (NA_COLS - 1)
    n_dc_pad = 32
    onehot = (np.arange(n_dc_pad)[:, None, None] == dc[None]) & inwin[None]
    e2d = jnp.asarray(onehot.reshape(n_dc_pad, GRID_W * GRID_W).astype(np.float32))
    rpb2d = jnp.pad(rpb.reshape(L, H * n_dr, n_dc), ((0, 0), (0, 0), (0, n_dc_pad - n_dc)))
    t = pl.pallas_call(
        _bias_expand_kernel,
        grid=(L,),
        in_specs=[pl.BlockSpec((None, H * n_dr, n_dc_pad), lambda l: (l, 0, 0)),
                  pl.BlockSpec((n_dc_pad, GRID_W * GRID_W), lambda l: (0, 0))],
        out_specs=pl.BlockSpec((None, H * n_dr, GRID_W * GRID_W), lambda l: (l, 0, 0)),
        out_shape=jax.ShapeDtypeStruct((L, H * n_dr, GRID_W * GRID_W), F32),
        compiler_params=_cparams(1),
        name="bias_expand",
    )(rpb2d, e2d)
    t = t.reshape(L, H, n_dr, GRID_W, GRID_W)
    t = jnp.where(jnp.asarray(inwin)[None, None, None], t, NEG)
    t = t.transpose(0, 1, 3, 2, 4).reshape(L, H, GRID_W, n_dr * GRID_W)
    nk = NA_ROWS * GRID_W
    cases = [t[..., (NA_ROWS - 1 - d) * GRID_W:(NA_ROWS - 1 - d) * GRID_W + nk] for d in range(NA_ROWS)]
    return jnp.stack(cases, axis=1)


def kernel(x, c, ctx, c_ctx, norm_g, w_ada, b_ada, w_in, a_norm_g, a_w_s, a_b_s, b_q_g, b_k_g, b_sink,
           c_w_f, d_q_g, d_k_g, d_rpb, w_gate, w_branch, w_out):
    B, S, D = x.shape
    C = ctx.shape[1]
    L = norm_g.shape[0]
    assert S % 512 == 0 and S // GRID_W >= NA_ROWS and S >= 3 * B_BLK and C % CHUNK == 0
    tm, tmc = 512, C
    tm_ctx = math.gcd(B * C, 512)
    cn = 512
    tm_in = 512
    W = BRANCH_W

    n_rows = -(-(B + 1) // 8) * 8
    cc = jnp.concatenate([c, c_ctx[None], jnp.zeros((n_rows - B - 1, D), F32)], axis=0)
    mod = _modulation(cc, w_ada, b_ada).reshape(L, n_rows, 1, 3 * D)
    lat_row = lambda b: b
    ctx_row = lambda b: B

    w_in_r = _rearrange_w_in(w_in)
    w_gate_b, w_branch_b, w_out_b = w_gate.astype(BF16), w_branch.astype(BF16), w_out.astype(BF16)
    a_ws_b = a_w_s.astype(BF16)
    gains = jnp.stack([jnp.tile(b_q_g, (1, W // B_HD)),
                       jnp.tile(b_k_g, (1, W // B_HD)),
                       jnp.tile(d_q_g, (1, W // D_HD)),
                       jnp.tile(d_k_g, (1, W // D_HD))] + [jnp.zeros((L, W), F32)] * 4, axis=1)
    bd64, bd32 = _block_diag_mean(W, B_HD), _block_diag_mean(W, D_HD)
    ropes = _rope_tables(S)
    c128, s128 = _dft_cos_sin(C_GW)
    tm_f, tm_fc = min(512, S // 2), min(512, C // 2)
    dft_lat, dft_ctx = _dft_half_matrix(S), _dft_half_matrix(C)
    jm_lat, jm_ctx = _rev_shift_matrix(tm_f), _rev_shift_matrix(tm_fc)
    assert tm_f == tm and tm_fc == C // 2
    bias_tabs = _na_bias_tables(d_rpb)

    h, hc = x, ctx
    for l in range(L):
        last = l == L - 1
        mod_l = mod[l]
        ng = norm_g[l][None]
        n_lat, z = _inproj(h, mod_l, lat_row, ng, w_in_r[l], gains[l], bd64, bd32, ropes, tm_in, True)
        n_ctx, zc = _inproj(hc.reshape(1, B * C, D), mod_l, ctx_row, ng, w_in_r[l], gains[l], bd64, bd32,
                            ropes, tm_ctx, False)
        zc = zc.reshape(B, C, Z_WIDTH)
        cw, sw = _dft_weights(c128, s128, c_w_f[l])
        g_v = a_norm_g[l][None]
        b_sT = a_b_s[l].T
        bias_tab = bias_tabs[l]

        u_lat = [
            _branch_a(z, g_v, a_ws_b[l], b_sT, tm),
            _branch_b(z, zc, b_sink[l]),
            _fourier_branch(z, cw, sw, dft_lat, jm_lat, tm_f),
            _branch_d(z, zc, bias_tab),
        ]
        h_new = _merge(h, mod_l, lat_row, n_lat, u_lat, w_gate_b[l], w_branch_b[l], w_out_b[l], tm, cn)
        if not last:
            u_ctx = [
                _branch_a(zc, g_v, a_ws_b[l], b_sT, tmc),
                _ctx_b(zc, b_sink[l]),
                _fourier_branch(zc, cw, sw, dft_ctx, jm_ctx, tm_fc).reshape(B, C, W),
                _ctx_d(zc),
            ]
            u_ctx = [u.reshape(1, B * C, W) for u in u_ctx]
            hc = _merge(hc.reshape(1, B * C, D), mod_l, ctx_row, n_ctx, u_ctx,
                        w_gate_b[l], w_branch_b[l], w_out_b[l], tm_ctx, cn).reshape(B, C, D)
        h = h_new
    return h
```

```python
import functools
import math

import jax
import jax.numpy as jnp
import numpy as np
from jax import lax
from jax.experimental import pallas as pl
from jax.experimental.pallas import tpu as pltpu

F32 = jnp.float32
BF16 = jnp.bfloat16

LANES = 128
V7X_VMEM_LIMIT_BYTES = 56 * 1024 * 1024

GRID_W = 64
N_BRANCH = 4
BRANCH_W = 512
CHUNK = 128
A_GROUPS = 4
B_HD = 64
B_QH = 8
B_KVH = 2
B_WIN = 128
B_BLK = 128
C_GROUPS = 4
C_GW = 128
D_HD = 32
D_H = 16
NA_ROWS = 8
NA_COLS = 16
ROPE_BASE = 10000.0
EPS = 1e-6
NEG = -1e30
LOG2E = math.log2(math.e)

(COL_AU, COL_AV, COL_AZ, COL_BQ, COL_BKV, COL_BZ,
 COL_FIN, COL_FZ, COL_DQ, COL_DK, COL_DV, COL_DZ) = range(12)
Z_WIDTH = 12 * BRANCH_W


def _cparams(n_grid, flags=None):
    return pltpu.CompilerParams(
        dimension_semantics=("arbitrary",) * n_grid,
        vmem_limit_bytes=V7X_VMEM_LIMIT_BYTES,
        flags=flags)


def _sigmoid(x):
    return 1.0 / (1.0 + jnp.exp(-x))


def _silu(x):
    return x * _sigmoid(x)


def _gelu(x):
    return 0.5 * x * (1.0 + jnp.tanh(math.sqrt(2.0 / math.pi) * (x + 0.044715 * (x * x * x))))


def _mod_kernel(c_ref, w_ref, b_ref, o_ref):
    c = c_ref[...]
    a = _silu(c).astype(BF16)
    o_ref[...] = jnp.dot(a, w_ref[...].astype(BF16), preferred_element_type=F32) + b_ref[...]


def _modulation(cc, w_ada, b_ada):
    L, D, N = w_ada.shape
    R = cc.shape[0]
    tn = 512
    return pl.pallas_call(
        _mod_kernel,
        grid=(L, N // tn),
        in_specs=[pl.BlockSpec((R, D), lambda l, j: (0, 0)),
                  pl.BlockSpec((None, D, tn), lambda l, j: (l, 0, j)),
                  pl.BlockSpec((None, 1, tn), lambda l, j: (l, 0, j))],
        out_specs=pl.BlockSpec((None, R, tn), lambda l, j: (l, 0, j)),
        out_shape=jax.ShapeDtypeStruct((L, R, N), F32),
        compiler_params=_cparams(2),
        name="modulation",
    )(cc, w_ada, b_ada.reshape(L, 1, N))


def _head_rms(x, bd_ref, width):
    sq = (x * x).astype(BF16)
    c = bd_ref.shape[0]
    parts = [jnp.dot(sq[:, o:o + c], bd_ref[...], preferred_element_type=F32) for o in range(0, width, c)]
    return parts[0] if len(parts) == 1 else jnp.concatenate(parts, axis=1)


def _rope(x, cos, sina, sinb):
    w = x.shape[-1]
    reps = w // LANES
    cos = jnp.concatenate([cos] * reps, axis=1)
    sina = jnp.concatenate([sina] * reps, axis=1)
    sinb = jnp.concatenate([sinb] * reps, axis=1)
    up = pltpu.roll(x, w - 16, 1)
    dn = pltpu.roll(x, 16, 1)
    return x * cos + up * sina + dn * sinb


def _ada_norm(x, mod_ref, ng_ref):
    D = x.shape[-1]
    ms = jnp.mean(x * x, axis=-1, keepdims=True)
    y = x * lax.rsqrt(ms + EPS) * ng_ref[...]
    return (y * (1.0 + mod_ref[:, D:2 * D]) + mod_ref[:, 0:D]).astype(BF16)


def _inproj_kernel(h_ref, mod_ref, ng_ref, w_ref, gains_ref, bd64_ref, bd32_ref,
                   cos_ref, sina_ref, sinb_ref, n_ref, z_ref, *, rope):
    W = BRANCH_W
    nb = _ada_norm(h_ref[...], mod_ref, ng_ref)
    n_ref[...] = nb

    def maybe_rope(t):
        if rope:
            return _rope(t, cos_ref[...], sina_ref[...], sinb_ref[...])
        return t

    def head_norm(t, bd_ref, gain):
        return t * lax.rsqrt(_head_rms(t, bd_ref, t.shape[-1]) + EPS) * gain

    for cb in range(Z_WIDTH // W):
        cols = slice(cb * W, (cb + 1) * W)
        acc = jnp.dot(nb, w_ref[:, cols], preferred_element_type=F32)
        if cb == COL_BQ:
            acc = maybe_rope(head_norm(acc, bd64_ref, gains_ref[0:1, :])) * (B_HD ** -0.5 * LOG2E)
        elif cb == COL_BKV:
            k = maybe_rope(head_norm(acc[:, 0:W // 2], bd64_ref, gains_ref[1:2, 0:W // 2]))
            acc = jnp.concatenate([k, acc[:, W // 2:]], axis=1)
        elif cb == COL_DQ:
            acc = head_norm(acc, bd32_ref, gains_ref[2:3, :]) * (D_HD ** -0.5 * LOG2E)
        elif cb == COL_DK:
            acc = head_norm(acc, bd32_ref, gains_ref[3:4, :])
        z_ref[:, cols] = acc.astype(BF16)


def _inproj(h, mod_l, mod_row, ng, w, gains, bd64, bd32, ropes, tm, rope):
    B, S, D = h.shape
    cos, sina, sinb = ropes
    const = lambda shape: pl.BlockSpec(shape, lambda b, i: (0,) * len(shape), pipeline_mode=pl.Buffered(1))
    rope_spec = pl.BlockSpec((tm, LANES), lambda b, i: (i if rope else 0, 0))
    return pl.pallas_call(
        functools.partial(_inproj_kernel, rope=rope),
        grid=(B, S // tm),
        in_specs=[pl.BlockSpec((None, tm, D), lambda b, i: (b, i, 0)),
                  pl.BlockSpec((None, 1, 3 * D), lambda b, i: (mod_row(b), 0, 0)),
                  const((1, D)),
                  const((D, Z_WIDTH)),
                  const((8, BRANCH_W)),
                  const(bd64.shape),
                  const(bd32.shape),
                  rope_spec, rope_spec, rope_spec],
        out_specs=[pl.BlockSpec((None, tm, D), lambda b, i: (b, i, 0)),
                   pl.BlockSpec((None, tm, Z_WIDTH), lambda b, i: (b, i, 0))],
        out_shape=[jax.ShapeDtypeStruct((B, S, D), BF16),
                   jax.ShapeDtypeStruct((B, S, Z_WIDTH), BF16)],
        compiler_params=_cparams(2),
        name="inproj",
    )(h, mod_l, ng, w, gains, bd64, bd32, cos, sina, sinb)


def _branch_a_kernel(au_ref, av_ref, az_ref, g_ref, ws_ref, bs_ref, o_ref):
    tm = au_ref.shape[0]
    v = _gelu(av_ref[...].astype(F32))
    ms = jnp.mean(v * v, axis=-1, keepdims=True)
    vb = (v * lax.rsqrt(ms + EPS) * g_ref[...]).astype(BF16)
    gate = _gelu(au_ref[...].astype(F32)) * _silu(az_ref[...].astype(F32))
    for c in range(tm // CHUNK):
        rows = slice(c * CHUNK, (c + 1) * CHUNK)
        for g in range(A_GROUPS):
            cols = slice(g * LANES, (g + 1) * LANES)
            mixed = jnp.dot(ws_ref[g], vb[rows, cols], preferred_element_type=F32) + bs_ref[:, g:g + 1]
            o_ref[rows, cols] = (gate[rows, cols] * mixed).astype(BF16)


def _branch_a(z, g_v, w_s, b_sT, tm):
    B, S, _ = z.shape
    W = BRANCH_W
    col = lambda cb: pl.BlockSpec((None, tm, W), lambda b, i: (b, i, cb))
    return pl.pallas_call(
        _branch_a_kernel,
        grid=(B, S // tm),
        in_specs=[col(COL_AU), col(COL_AV), col(COL_AZ),
                  pl.BlockSpec((1, W), lambda b, i: (0, 0)),
                  pl.BlockSpec((A_GROUPS, CHUNK, CHUNK), lambda b, i: (0, 0, 0)),
                  pl.BlockSpec((CHUNK, A_GROUPS), lambda b, i: (0, 0))],
        out_specs=pl.BlockSpec((None, tm, W), lambda b, i: (b, i, 0)),
        out_shape=jax.ShapeDtypeStruct((B, S, W), BF16),
        compiler_params=_cparams(2),
        name="branch_a",
    )(z, z, z, g_v, w_s, b_sT)


def _dftw_kernel(c_ref, s_ref, w_ref, cw_ref, sw_ref):
    for g in range(C_GROUPS):
        w = w_ref[g]
        cw_ref[g] = jnp.dot(c_ref[...], w, preferred_element_type=F32,
                            precision=lax.Precision.HIGHEST).astype(BF16)
        sw_ref[g] = jnp.dot(s_ref[...], w, preferred_element_type=F32,
                            precision=lax.Precision.HIGHEST).astype(BF16)


def _dft_weights(c128, s128, w_f):
    shp = jax.ShapeDtypeStruct((C_GROUPS, C_GW, C_GW), BF16)
    return pl.pallas_call(_dftw_kernel, out_shape=[shp, shp], name="dft_weights")(c128, s128, w_f)


def _fourier_pq_kernel(xa_ref, xm_ref, xn_ref, xh_ref, jm_ref, cw_ref, sw_ref, pq_ref, aux_ref):
    i = pl.program_id(1)
    tm = xa_ref.shape[0]
    row = lax.broadcasted_iota(jnp.int32, (tm, 1), 0)
    first = jnp.where(i > 0, xn_ref[0:1, :].astype(F32), 0.0)
    xb = jnp.where(row == 0, first, jnp.dot(jm_ref[...], xm_ref[...], preferred_element_type=F32))
    xa = xa_ref[...].astype(F32)
    xe = (xa + xb).astype(BF16)
    xo = (xa - xb).astype(BF16)
    for g in range(C_GROUPS):
        cols = slice(g * LANES, (g + 1) * LANES)
        pq_ref[0, :, cols] = jnp.dot(xe[:, cols], cw_ref[g], preferred_element_type=F32).astype(BF16)
        pq_ref[1, :, cols] = jnp.dot(xo[:, cols], sw_ref[g], preferred_element_type=F32).astype(BF16)

    @pl.when(i == 0)
    def _():
        for g in range(C_GROUPS):
            cols = slice(g * LANES, (g + 1) * LANES)
            aux_ref[:, cols] = jnp.dot(xh_ref[:, cols], cw_ref[g], preferred_element_type=F32)


def _fourier_pq(z, jm, cw, sw, tm):
    B, L, _ = z.shape
    H = L // 2
    W = BRANCH_W
    nt = H // tm
    wspec = pl.BlockSpec((C_GROUPS, C_GW, C_GW), lambda b, i: (0, 0, 0))
    return pl.pallas_call(
        _fourier_pq_kernel,
        grid=(B, nt),
        in_specs=[pl.BlockSpec((None, tm, W), lambda b, i: (b, i, COL_FIN)),
                  pl.BlockSpec((None, tm, W), lambda b, i: (b, 2 * nt - 1 - i, COL_FIN)),
                  pl.BlockSpec((None, 8, W),
                               lambda b, i: (b, jnp.minimum((L - i * tm) // 8, L // 8 - 1), COL_FIN)),
                  pl.BlockSpec((None, 8, W), lambda b, i: (b, H // 8, COL_FIN)),
                  pl.BlockSpec((tm, tm), lambda b, i: (0, 0)),
                  wspec, wspec],
        out_specs=[pl.BlockSpec((None, 2, tm, W), lambda b, i: (b, 0, i, 0)),
                   pl.BlockSpec((None, 8, W), lambda b, i: (b, 0, 0))],
        out_shape=[jax.ShapeDtypeStruct((B, 2, H, W), BF16),
                   jax.ShapeDtypeStruct((B, 8, W), F32)],
        compiler_params=_cparams(2),
        name="fourier_pq",
    )(z, z, z, z, jm, cw, sw)


def _fourier_mix_kernel(dft_ref, dftx_ref, pq_ref, aux_ref, jm_ref, fza_ref, fzb_ref, u_ref, *, scale, half):
    tm = dft_ref.shape[0]
    pe = pq_ref[0:half, :]
    qo = pq_ref[half:, :]
    ev = jnp.dot(dft_ref[:, 0:half], pe, preferred_element_type=F32)
    od = jnp.dot(dft_ref[:, half:], qo, preferred_element_type=F32)
    row = lax.broadcasted_iota(jnp.int32, (tm, 1), 0)
    nyq = (1 - 2 * (row & 1)).astype(F32) * aux_ref[0:1, :]
    y1 = ev + od + nyq
    y2 = ev - od + nyq
    evx = jnp.dot(dftx_ref[:, 0:half], pe, preferred_element_type=F32)
    odx = jnp.dot(dftx_ref[:, half:], qo, preferred_element_type=F32)
    y2x = evx[0:1, :] - odx[0:1, :] + aux_ref[0:1, :]
    hi = y2.astype(BF16)
    lo = (y2 - hi.astype(F32)).astype(BF16)
    y2r = (jnp.dot(jm_ref[...], hi, preferred_element_type=F32)
           + jnp.dot(jm_ref[...], lo, preferred_element_type=F32))
    y2r = jnp.where(row == 0, y2x, y2r)
    u_ref[0] = (y1 * scale * _silu(fza_ref[...].astype(F32))).astype(BF16)
    u_ref[1] = (y2r * scale * _silu(fzb_ref[...].astype(F32))).astype(BF16)


def _fourier_mix(dft, pq, aux, jm, z, tm):
    B, L, _ = z.shape
    H = L // 2
    W = BRANCH_W
    nt = H // tm
    scale = 1.0 / math.sqrt(L * C_GW)
    return pl.pallas_call(
        functools.partial(_fourier_mix_kernel, scale=scale, half=H),
        grid=(B, nt),
        in_specs=[pl.BlockSpec((tm, L), lambda b, i: (i, 0)),
                  pl.BlockSpec((8, L), lambda b, i: ((i + 1) * (tm // 8), 0)),
                  pl.BlockSpec((None, L, W), lambda b, i: (b, 0, 0)),
                  pl.BlockSpec((None, 8, W), lambda b, i: (b, 0, 0)),
                  pl.BlockSpec((tm, tm), lambda b, i: (0, 0)),
                  pl.BlockSpec((None, tm, W), lambda b, i: (b, i, COL_FZ)),
                  pl.BlockSpec((None, tm, W), lambda b, i: (b, 2 * nt - 1 - i, COL_FZ))],
        out_specs=pl.BlockSpec((None, 2, tm, W), lambda b, i: (b, 0, i, 0)),
        out_shape=jax.ShapeDtypeStruct((B, 2, H, W), BF16),
        compiler_params=_cparams(2),
        name="fourier_mix",
    )(dft, dft, pq, aux, jm, z, z)


def _fourier_branch(z, cw, sw, dft, jm, tm):
    B, L, _ = z.shape
    pq, aux = _fourier_pq(z, jm, cw, sw, tm)
    return _fourier_mix(dft, pq.reshape(B, L, BRANCH_W), aux, jm, z, tm)


def _lane_masks(n_heads, hd):
    lane = lax.broadcasted_iota(jnp.int32, (1, LANES), 1)
    return [((lane >= g * hd) & (lane < (g + 1) * hd)) for g in range(n_heads)]


def _attend(q, segs, sink_col, n_heads, hd):
    tq = q.shape[0]
    masks = _lane_masks(n_heads, hd)
    zero = jnp.zeros_like(q)
    qx = jnp.concatenate([jnp.where(m, q, zero) for m in masks], axis=0)
    scores = []
    for k, _, bias in segs:
        s = lax.dot_general(qx, k, (((1,), (1,)), ((), ())), preferred_element_type=F32)
        if bias is not None:
            s = s + bias
        scores.append(s)
    m = scores[0].max(axis=-1, keepdims=True)
    for s in scores[1:]:
        m = jnp.maximum(m, s.max(axis=-1, keepdims=True))
    if sink_col is not None:
        m = jnp.maximum(m, sink_col)
    den = None
    acc = None
    for s, (_, v, _) in zip(scores, segs):
        p = jnp.exp2(s - m)
        d = p.sum(axis=-1, keepdims=True)
        o = jnp.dot(p.astype(BF16), v, preferred_element_type=F32)
        den = d if den is None else den + d
        acc = o if acc is None else acc + o
    if sink_col is not None:
        den = den + jnp.exp2(sink_col - m)
    acc = acc / den
    out = None
    for g, msk in enumerate(masks):
        part = jnp.where(msk, acc[g * tq:(g + 1) * tq], 0.0)
        out = part if out is None else out + part
    return out


def _branch_b_kernel(sink_ref, q_ref, kv_ref, ckv_ref, bz_ref, o_ref, *, seq):
    nl = 3 * B_BLK
    half = BRANCH_W // 2
    for blk in range(q_ref.shape[0] // B_BLK):
        i = pl.program_id(1) * (q_ref.shape[0] // B_BLK) + blk
        rows = slice(blk * B_BLK, (blk + 1) * B_BLK)
        start = jnp.clip((i - 1) * B_BLK, 0, seq - nl)
        start = pl.multiple_of(start, B_BLK)
        qpos = i * B_BLK + lax.broadcasted_iota(jnp.int32, (B_BLK, nl), 0)
        kpos = start + lax.broadcasted_iota(jnp.int32, (B_BLK, nl), 1)
        bias1 = jnp.where(jnp.abs(kpos - qpos) <= B_WIN, 0.0, NEG).astype(F32)
        bias = jnp.concatenate([bias1, bias1], axis=0)
        for grp in range(BRANCH_W // LANES):
            kvh = grp // 2
            kc = slice(kvh * LANES, (kvh + 1) * LANES)
            vc = slice(half + kvh * LANES, half + (kvh + 1) * LANES)
            cols = slice(grp * LANES, (grp + 1) * LANES)
            sink_col = jnp.concatenate(
                [jnp.full((B_BLK, 1), sink_ref[2 * grp + g] * LOG2E, F32) for g in range(2)], axis=0)
            segs = [(kv_ref[pl.ds(start, nl), kc], kv_ref[pl.ds(start, nl), vc], bias),
                    (ckv_ref[:, kc], ckv_ref[:, vc], None)]
            o = _attend(q_ref[rows, cols], segs, sink_col, 2, B_HD)
            o_ref[rows, cols] = (o * _silu(bz_ref[rows, cols].astype(F32))).astype(BF16)


def _branch_b(z, zc, sink, tq):
    B, S, _ = z.shape
    C = zc.shape[1]
    W = BRANCH_W
    return pl.pallas_call(
        functools.partial(_branch_b_kernel, seq=S),
        grid=(B, S // tq),
        in_specs=[pl.BlockSpec(memory_space=pltpu.SMEM),
                  pl.BlockSpec((None, tq, W), lambda b, i: (b, i, COL_BQ)),
                  pl.BlockSpec((None, S, W), lambda b, i: (b, 0, COL_BKV)),
                  pl.BlockSpec((None, C, W), lambda b, i: (b, 0, COL_BKV)),
                  pl.BlockSpec((None, tq, W), lambda b, i: (b, i, COL_BZ))],
        out_specs=pl.BlockSpec((None, tq, W), lambda b, i: (b, i, 0)),
        out_shape=jax.ShapeDtypeStruct((B, S, W), BF16),
        compiler_params=_cparams(2),
        name="branch_b",
    )(sink, z, z, zc, z)


def _ctx_b_kernel(sink_ref, q_ref, kv_ref, bz_ref, o_ref):
    tq = q_ref.shape[0]
    half = BRANCH_W // 2
    for grp in range(BRANCH_W // LANES):
        kvh = grp // 2
        kc = slice(kvh * LANES, (kvh + 1) * LANES)
        vc = slice(half + kvh * LANES, half + (kvh + 1) * LANES)
        cols = slice(grp * LANES, (grp + 1) * LANES)
        sink_col = jnp.concatenate(
            [jnp.full((tq, 1), sink_ref[2 * grp + g] * LOG2E, F32) for g in range(2)], axis=0)
        o = _attend(q_ref[:, cols], [(kv_ref[:, kc], kv_ref[:, vc], None)], sink_col, 2, B_HD)
        o_ref[:, cols] = (o * _silu(bz_ref[:, cols].astype(F32))).astype(BF16)


def _ctx_b(zc, sink):
    B, C, _ = zc.shape
    W = BRANCH_W
    col = lambda cb: pl.BlockSpec((None, C, W), lambda b: (b, 0, cb))
    return pl.pallas_call(
        _ctx_b_kernel,
        grid=(B,),
        in_specs=[pl.BlockSpec(memory_space=pltpu.SMEM), col(COL_BQ), col(COL_BKV), col(COL_BZ)],
        out_specs=pl.BlockSpec((None, C, W), lambda b: (b, 0, 0)),
        out_shape=jax.ShapeDtypeStruct((B, C, W), BF16),
        compiler_params=_cparams(1),
        name="ctx_b",
    )(sink, zc, zc, zc)


def _na_row_start(r, rows):
    return jnp.clip(r - NA_ROWS // 2, 0, rows - NA_ROWS)


def _branch_d_kernel(q_ref, k_ref, v_ref, ck_ref, cv_ref, dz_ref, *rest, rows):
    bias_refs, o_ref = rest[:-1], rest[-1]
    nk = NA_ROWS * GRID_W
    hpg = LANES // D_HD
    for rr, bias_ref in enumerate(bias_refs):
        r = pl.program_id(1) * len(bias_refs) + rr
        qrows = slice(rr * GRID_W, (rr + 1) * GRID_W)
        start = pl.multiple_of(_na_row_start(r, rows) * GRID_W, GRID_W)
        for grp in range(BRANCH_W // LANES):
            cols = slice(grp * LANES, (grp + 1) * LANES)
            bias = bias_ref[grp * hpg:(grp + 1) * hpg].reshape(hpg * GRID_W, nk)
            segs = [(k_ref[pl.ds(start, nk), cols], v_ref[pl.ds(start, nk), cols], bias),
                    (ck_ref[:, cols], cv_ref[:, cols], None)]
            o = _attend(q_ref[qrows, cols], segs, None, hpg, D_HD)
            o_ref[qrows, cols] = (o * _silu(dz_ref[qrows, cols].astype(F32))).astype(BF16)


def _branch_d(z, zc, bias_tab, rps):
    B, S, _ = z.shape
    C = zc.shape[1]
    W = BRANCH_W
    rows = S // GRID_W
    nk = NA_ROWS * GRID_W
    tile = lambda cb: pl.BlockSpec((None, rps * GRID_W, W), lambda b, i: (b, i, cb))
    full = lambda cb: pl.BlockSpec((None, S, W), lambda b, i: (b, 0, cb))
    ctx = lambda cb: pl.BlockSpec((None, C, W), lambda b, i: (b, 0, cb))

    def bias_spec(rr):
        def index(b, i):
            r = i * rps + rr
            return (r - _na_row_start(r, rows), 0, 0, 0)
        return pl.BlockSpec((None, D_H, GRID_W, nk), index)

    return pl.pallas_call(
        functools.partial(_branch_d_kernel, rows=rows),
        grid=(B, rows // rps),
        in_specs=[tile(COL_DQ), full(COL_DK), full(COL_DV), ctx(COL_DK), ctx(COL_DV), tile(COL_DZ)]
                 + [bias_spec(rr) for rr in range(rps)],
        out_specs=pl.BlockSpec((None, rps * GRID_W, W), lambda b, i: (b, i, 0)),
        out_shape=jax.ShapeDtypeStruct((B, S, W), BF16),
        compiler_params=_cparams(2),
        name="branch_d",
    )(z, z, z, zc, zc, z, *([bias_tab] * rps))


def _ctx_d_kernel(q_ref, k_ref, v_ref, dz_ref, o_ref):
    hpg = LANES // D_HD
    for grp in range(BRANCH_W // LANES):
        cols = slice(grp * LANES, (grp + 1) * LANES)
        o = _attend(q_ref[:, cols], [(k_ref[:, cols], v_ref[:, cols], None)], None, hpg, D_HD)
        o_ref[:, cols] = (o * _silu(dz_ref[:, cols].astype(F32))).astype(BF16)


def _ctx_d(zc):
    B, C, _ = zc.shape
    W = BRANCH_W
    col = lambda cb: pl.BlockSpec((None, C, W), lambda b: (b, 0, cb))
    return pl.pallas_call(
        _ctx_d_kernel,
        grid=(B,),
        in_specs=[col(COL_DQ), col(COL_DK), col(COL_DV), col(COL_DZ)],
        out_specs=pl.BlockSpec((None, C, W), lambda b: (b, 0, 0)),
        out_shape=jax.ShapeDtypeStruct((B, C, W), BF16),
        compiler_params=_cparams(1),
        name="ctx_d",
    )(zc, zc, zc, zc)


def _gate_mix_kernel(n_ref, ua_ref, ub_ref, uc_ref, ud_ref, wg_ref, wb_ref, o_ref):
    n = n_ref[...]
    mix = None
    for b, u_ref in enumerate((ua_ref, ub_ref, uc_ref, ud_ref)):
        gate = _sigmoid(jnp.dot(n, wg_ref[b], preferred_element_type=F32))
        t = gate * jnp.dot(u_ref[...], wb_ref[b], preferred_element_type=F32)
        mix = t if mix is None else mix + t
    o_ref[...] = mix.astype(BF16)


def _gate_mix(n, us, wg, wb, tm, cn):
    B, S, D = n.shape
    W = BRANCH_W
    row = lambda width: pl.BlockSpec((None, tm, width), lambda j, b, i: (b, i, 0))
    uc_spec = row(W)
    if us[2].ndim == 4:
        nt = S // (2 * tm)
        uc_spec = pl.BlockSpec((None, None, tm, W),
                               lambda j, b, i: (b, i // nt, jnp.where(i < nt, i, 2 * nt - 1 - i), 0))
    return pl.pallas_call(
        _gate_mix_kernel,
        grid=(D // cn, B, S // tm),
        in_specs=[row(D), row(W), row(W), uc_spec, row(W),
                  pl.BlockSpec((N_BRANCH, D, cn), lambda j, b, i: (0, 0, j)),
                  pl.BlockSpec((N_BRANCH, W, cn), lambda j, b, i: (0, 0, j))],
        out_specs=pl.BlockSpec((None, tm, cn), lambda j, b, i: (b, i, j)),
        out_shape=jax.ShapeDtypeStruct((B, S, D), BF16),
        compiler_params=_cparams(3),
        name="gate_mix",
    )(n, *us, wg, wb)


def _out_proj_kernel(h_ref, mod_ref, mix_ref, wo_ref, o_ref):
    D = h_ref.shape[-1]
    y = jnp.dot(mix_ref[...], wo_ref[...], preferred_element_type=F32)
    o_ref[...] = h_ref[...] + mod_ref[:, 2 * D:3 * D] * y


def _out_proj(h, mod_l, mod_row, mix, wo, tm):
    B, S, D = h.shape
    row = lambda: pl.BlockSpec((None, tm, D), lambda b, i: (b, i, 0))
    return pl.pallas_call(
        _out_proj_kernel,
        grid=(B, S // tm),
        in_specs=[row(),
                  pl.BlockSpec((None, 1, 3 * D), lambda b, i: (mod_row(b), 0, 0)),
                  row(),
                  pl.BlockSpec((D, D), lambda b, i: (0, 0), pipeline_mode=pl.Buffered(1))],
        out_specs=row(),
        out_shape=jax.ShapeDtypeStruct((B, S, D), F32),
        compiler_params=_cparams(2),
        name="out_proj",
    )(h, mod_l, mix, wo)


def _merge(h, mod_l, mod_row, n, us, wg, wb, wo, tm, cn):
    return _out_proj(h, mod_l, mod_row, _gate_mix(n, us, wg, wb, tm, cn), wo, tm)


def _rearrange_w_in(w_in):
    W = BRANCH_W
    o = 0
    parts = {}
    for name, size in (("a_u", W), ("a_v", W), ("a_z", W), ("b_q", W), ("b_k", B_KVH * B_HD),
                       ("b_v", B_KVH * B_HD), ("b_z", W), ("f_in", W), ("f_z", W),
                       ("d_q", W), ("d_k", W), ("d_v", W), ("d_z", W)):
        parts[name] = w_in[:, :, o:o + size]
        o += size

    def dup(p):
        return jnp.concatenate([p[:, :, 0:B_HD], p[:, :, 0:B_HD], p[:, :, B_HD:], p[:, :, B_HD:]], axis=-1)

    cols = [parts["a_u"], parts["a_v"], parts["a_z"], parts["b_q"], dup(parts["b_k"]), dup(parts["b_v"]),
            parts["b_z"], parts["f_in"], parts["f_z"], parts["d_q"], parts["d_k"], parts["d_v"], parts["d_z"]]
    return jnp.concatenate(cols, axis=-1).astype(BF16)


def _rope_tables(S):
    t = np.arange(S)
    pos = np.stack([t // GRID_W, t % GRID_W], axis=1).astype(np.float32)
    lane = np.arange(B_HD)
    which = lane // (B_HD // 2)
    fi = lane % (B_HD // 4)
    inv = (ROPE_BASE ** (-(fi.astype(np.float32)) / (B_HD // 4))).astype(np.float32)
    ang = pos[:, which] * inv[None, :]
    cos, sin = np.cos(ang), np.sin(ang)
    lower = (lane % (B_HD // 2)) < (B_HD // 4)
    sina = np.where(lower[None, :], -sin, 0.0)
    sinb = np.where(lower[None, :], 0.0, sin)
    rep = lambda a: jnp.asarray(np.tile(a.astype(np.float32), (1, LANES // B_HD)))
    return rep(cos), rep(sina), rep(sinb)


def _block_diag_mean(width, hd):
    i = np.arange(width)
    return jnp.asarray(((i[:, None] // hd) == (i[None, :] // hd)).astype(np.float32) / hd, dtype=BF16)


def _dft_cos_sin(n):
    k = jnp.arange(n, dtype=jnp.int32)
    m = (k[:, None] * k[None, :]) % n
    ang = m.astype(F32) * (2.0 * math.pi / n)
    return jnp.cos(ang), jnp.sin(ang)


def _dft_half_matrix(n):
    k = jnp.arange(n // 2 + 8, dtype=jnp.int32)
    m = jnp.arange(n // 2, dtype=jnp.int32)
    ang = ((k[:, None] * m[None, :]) % n).astype(F32) * (2.0 * math.pi / n)
    return jnp.concatenate([jnp.cos(ang), -jnp.sin(ang)], axis=1).astype(BF16)


def _rev_shift_matrix(tm):
    r = np.arange(tm)
    return jnp.asarray(((r[:, None] >= 1) & (r[None, :] == tm - r[:, None])).astype(np.float32), dtype=BF16)


def _bias_expand_kernel(rpb_ref, e_ref, o_ref):
    o_ref[...] = LOG2E * jnp.dot(rpb_ref[...], e_ref[...], preferred_element_type=F32,
                                 precision=lax.Precision.HIGHEST)


def _na_bias_tables(rpb):
    L, H, n_dr, n_dc = rpb.shape
    c = np.arange(GRID_W)
    c0 = np.clip(c - NA_COLS // 2, 0, GRID_W - NA_COLS)
    kc = np.arange(GRID_W)
    inwin = (kc[None, :] >= c0[:, None]) & (kc[None, :] < c0[:, None] + NA_COLS)
    dc = kc[None, :] - c[:, None] + (NA_COLS - 1)
    n_dc_pad = 32
    onehot = (np.arange(n_dc_pad)[:, None, None] == dc[None]) & inwin[None]
    e2d = jnp.asarray(onehot.reshape(n_dc_pad, GRID_W * GRID_W).astype(np.float32))
    rpb2d = jnp.pad(rpb.reshape(L, H * n_dr, n_dc), ((0, 0), (0, 0), (0, n_dc_pad - n_dc)))
    t = pl.pallas_call(
        _bias_expand_kernel,
        grid=(L,),
        in_specs=[pl.BlockSpec((None, H * n_dr, n_dc_pad), lambda l: (l, 0, 0)),
                  pl.BlockSpec((n_dc_pad, GRID_W * GRID_W), lambda l: (0, 0))],
        out_specs=pl.BlockSpec((None, H * n_dr, GRID_W * GRID_W), lambda l: (l, 0, 0)),
        out_shape=jax.ShapeDtypeStruct((L, H * n_dr, GRID_W * GRID_W), F32),
        compiler_params=_cparams(1),
        name="bias_expand",
    )(rpb2d, e2d)
    t = t.reshape(L, H, n_dr, GRID_W, GRID_W)
    t = jnp.where(jnp.asarray(inwin)[None, None, None], t, NEG)
    t = t.transpose(0, 1, 3, 2, 4).reshape(L, H, GRID_W, n_dr * GRID_W)
    nk = NA_ROWS * GRID_W
    cases = [t[..., (NA_ROWS - 1 - d) * GRID_W:(NA_ROWS - 1 - d) * GRID_W + nk] for d in range(NA_ROWS)]
    return jnp.stack(cases, axis=1)


def kernel(x, c, ctx, c_ctx, norm_g, w_ada, b_ada, w_in, a_norm_g, a_w_s, a_b_s, b_q_g, b_k_g, b_sink,
           c_w_f, d_q_g, d_k_g, d_rpb, w_gate, w_branch, w_out):
    B, S, D = x.shape
    C = ctx.shape[1]
    L = norm_g.shape[0]
    assert S % 512 == 0 and S // GRID_W >= NA_ROWS and S >= 3 * B_BLK and C % CHUNK == 0
    tm, tmc = 512, C
    tm_ctx = math.gcd(B * C, 512)
    cn = 512
    tm_in = 512
    tq_b = 4 * B_BLK
    rps_d = 4
    W = BRANCH_W

    n_rows = -(-(B + 1) // 8) * 8
    cc = jnp.concatenate([c, c_ctx[None], jnp.zeros((n_rows - B - 1, D), F32)], axis=0)
    mod = _modulation(cc, w_ada, b_ada).reshape(L, n_rows, 1, 3 * D)
    lat_row = lambda b: b
    ctx_row = lambda b: B

    w_in_r = _rearrange_w_in(w_in)
    w_gate_b, w_branch_b, w_out_b = w_gate.astype(BF16), w_branch.astype(BF16), w_out.astype(BF16)
    a_ws_b = a_w_s.astype(BF16)
    gains = jnp.stack([jnp.tile(b_q_g, (1, W // B_HD)),
                       jnp.tile(b_k_g, (1, W // B_HD)),
                       jnp.tile(d_q_g, (1, W // D_HD)),
                       jnp.tile(d_k_g, (1, W // D_HD))] + [jnp.zeros((L, W), F32)] * 4, axis=1)
    bd64, bd32 = _block_diag_mean(W // 2, B_HD), _block_diag_mean(W // 2, D_HD)
    ropes = _rope_tables(S)
    c128, s128 = _dft_cos_sin(C_GW)
    tm_f, tm_fc = min(512, S // 2), min(512, C // 2)
    dft_lat, dft_ctx = _dft_half_matrix(S), _dft_half_matrix(C)
    jm_lat, jm_ctx = _rev_shift_matrix(tm_f), _rev_shift_matrix(tm_fc)
    assert tm_f == tm and tm_fc == C // 2
    bias_tabs = _na_bias_tables(d_rpb)

    h, hc = x, ctx.reshape(1, B * C, D)
    for l in range(L):
        last = l == L - 1
        mod_l = mod[l]
        ng = norm_g[l][None]
        n_lat, z = _inproj(h, mod_l, lat_row, ng, w_in_r[l], gains[l], bd64, bd32, ropes, tm_in, True)
        n_ctx, zc = _inproj(hc, mod_l, ctx_row, ng, w_in_r[l], gains[l], bd64, bd32, ropes, tm_ctx, False)
        zc = zc.reshape(B, C, Z_WIDTH)
        cw, sw = _dft_weights(c128, s128, c_w_f[l])
        g_v = a_norm_g[l][None]
        b_sT = a_b_s[l].T
        bias_tab = bias_tabs[l]

        u_lat = [
            _branch_a(z, g_v, a_ws_b[l], b_sT, tm),
            _branch_b(z, zc, b_sink[l], tq_b),
            _fourier_branch(z, cw, sw, dft_lat, jm_lat, tm_f),
            _branch_d(z, zc, bias_tab, rps_d),
        ]
        h = _merge(h, mod_l, lat_row, n_lat, u_lat, w_gate_b[l], w_branch_b[l], w_out_b[l], tm, cn)
        if not last:
            u_ctx = [
                _branch_a(zc, g_v, a_ws_b[l], b_sT, tmc),
                _ctx_b(zc, b_sink[l]),
                _fourier_branch(zc, cw, sw, dft_ctx, jm_ctx, tm_fc).reshape(B, C, W),
                _ctx_d(zc),
            ]
            u_ctx = [u.reshape(1, B * C, W) for u in u_ctx]
            hc = _merge(hc, mod_l, ctx_row, n_ctx, u_ctx,
                        w_gate_b[l], w_branch_b[l], w_out_b[l], tm_ctx, cn)
    return h
```

```python
import functools
import math

import jax
import jax.numpy as jnp
import numpy as np
from jax import lax
from jax.experimental import pallas as pl
from jax.experimental.pallas import tpu as pltpu

F32 = jnp.float32
BF16 = jnp.bfloat16

LANES = 128
V7X_VMEM_LIMIT_BYTES = 56 * 1024 * 1024

GRID_W = 64
N_BRANCH = 4
BRANCH_W = 512
CHUNK = 128
A_GROUPS = 4
B_HD = 64
B_QH = 8
B_KVH = 2
B_WIN = 128
B_BLK = 128
C_GROUPS = 4
C_GW = 128
D_HD = 32
D_H = 16
NA_ROWS = 8
NA_COLS = 16
ROPE_BASE = 10000.0
EPS = 1e-6
NEG = -1e30
LOG2E = math.log2(math.e)

(COL_AU, COL_AV, COL_AZ, COL_BQ, COL_BKV, COL_BZ,
 COL_FIN, COL_FZ, COL_DQ, COL_DK, COL_DV, COL_DZ) = range(12)
Z_WIDTH = 12 * BRANCH_W


def _cparams(n_grid, flags=None):
    return pltpu.CompilerParams(
        dimension_semantics=("arbitrary",) * n_grid,
        vmem_limit_bytes=V7X_VMEM_LIMIT_BYTES,
        flags=flags)


def _sigmoid(x):
    return 1.0 / (1.0 + jnp.exp(-x))


def _silu(x):
    return x * _sigmoid(x)


def _gelu(x):
    return 0.5 * x * (1.0 + jnp.tanh(math.sqrt(2.0 / math.pi) * (x + 0.044715 * (x * x * x))))


def _mod_kernel(c_ref, w_ref, b_ref, o_ref):
    c = c_ref[...]
    a = _silu(c).astype(BF16)
    o_ref[...] = jnp.dot(a, w_ref[...].astype(BF16), preferred_element_type=F32) + b_ref[...]


def _modulation(cc, w_ada, b_ada):
    L, D, N = w_ada.shape
    R = cc.shape[0]
    tn = 512
    return pl.pallas_call(
        _mod_kernel,
        grid=(L, N // tn),
        in_specs=[pl.BlockSpec((R, D), lambda l, j: (0, 0)),
                  pl.BlockSpec((None, D, tn), lambda l, j: (l, 0, j)),
                  pl.BlockSpec((None, 1, tn), lambda l, j: (l, 0, j))],
        out_specs=pl.BlockSpec((None, R, tn), lambda l, j: (l, 0, j)),
        out_shape=jax.ShapeDtypeStruct((L, R, N), F32),
        compiler_params=_cparams(2),
        name="modulation",
    )(cc, w_ada, b_ada.reshape(L, 1, N))


def _head_rms(x, bd_ref, width):
    sq = (x * x).astype(BF16)
    c = bd_ref.shape[0]
    parts = [jnp.dot(sq[:, o:o + c], bd_ref[...], preferred_element_type=F32) for o in range(0, width, c)]
    return parts[0] if len(parts) == 1 else jnp.concatenate(parts, axis=1)


def _rope(x, cos, sina, sinb):
    w = x.shape[-1]
    reps = w // LANES
    cos = jnp.concatenate([cos] * reps, axis=1)
    sina = jnp.concatenate([sina] * reps, axis=1)
    sinb = jnp.concatenate([sinb] * reps, axis=1)
    up = pltpu.roll(x, w - 16, 1)
    dn = pltpu.roll(x, 16, 1)
    return x * cos + up * sina + dn * sinb


def _ada_norm(x, mod_ref, ng_ref):
    D = x.shape[-1]
    ms = jnp.mean(x * x, axis=-1, keepdims=True)
    y = x * lax.rsqrt(ms + EPS) * ng_ref[...]
    return (y * (1.0 + mod_ref[:, D:2 * D]) + mod_ref[:, 0:D]).astype(BF16)


def _inproj_kernel(h_ref, mod_ref, ng_ref, w_ref, gains_ref, bd64_ref, bd32_ref,
                   cos_ref, sina_ref, sinb_ref, n_ref, z_ref, *, rope):
    W = BRANCH_W
    nb = _ada_norm(h_ref[...], mod_ref, ng_ref)
    n_ref[...] = nb

    def maybe_rope(t):
        if rope:
            return _rope(t, cos_ref[...], sina_ref[...], sinb_ref[...])
        return t

    def head_norm(t, bd_ref, gain):
        return t * lax.rsqrt(_head_rms(t, bd_ref, t.shape[-1]) + EPS) * gain

    for cb in range(Z_WIDTH // W):
        cols = slice(cb * W, (cb + 1) * W)
        acc = jnp.dot(nb, w_ref[:, cols], preferred_element_type=F32)
        if cb == COL_BQ:
            acc = maybe_rope(head_norm(acc, bd64_ref, gains_ref[0:1, :])) * (B_HD ** -0.5 * LOG2E)
        elif cb == COL_BKV:
            k = maybe_rope(head_norm(acc[:, 0:W // 2], bd64_ref, gains_ref[1:2, 0:W // 2]))
            acc = jnp.concatenate([k, acc[:, W // 2:]], axis=1)
        elif cb == COL_DQ:
            acc = head_norm(acc, bd32_ref, gains_ref[2:3, :]) * (D_HD ** -0.5 * LOG2E)
        elif cb == COL_DK:
            acc = head_norm(acc, bd32_ref, gains_ref[3:4, :])
        z_ref[:, cols] = acc.astype(BF16)


def _inproj(h, mod_l, mod_row, ng, w, gains, bd64, bd32, ropes, tm, rope):
    B, S, D = h.shape
    cos, sina, sinb = ropes
    const = lambda shape: pl.BlockSpec(shape, lambda b, i: (0,) * len(shape), pipeline_mode=pl.Buffered(1))
    rope_spec = pl.BlockSpec((tm, LANES), lambda b, i: (i if rope else 0, 0))
    return pl.pallas_call(
        functools.partial(_inproj_kernel, rope=rope),
        grid=(B, S // tm),
        in_specs=[pl.BlockSpec((None, tm, D), lambda b, i: (b, i, 0)),
                  pl.BlockSpec((None, 1, 3 * D), lambda b, i: (mod_row(b), 0, 0)),
                  const((1, D)),
                  const((D, Z_WIDTH)),
                  const((8, BRANCH_W)),
                  const(bd64.shape),
                  const(bd32.shape),
                  rope_spec, rope_spec, rope_spec],
        out_specs=[pl.BlockSpec((None, tm, D), lambda b, i: (b, i, 0)),
                   pl.BlockSpec((None, tm, Z_WIDTH), lambda b, i: (b, i, 0))],
        out_shape=[jax.ShapeDtypeStruct((B, S, D), BF16),
                   jax.ShapeDtypeStruct((B, S, Z_WIDTH), BF16)],
        compiler_params=_cparams(2),
        name="inproj",
    )(h, mod_l, ng, w, gains, bd64, bd32, cos, sina, sinb)


def _branch_a_kernel(au_ref, av_ref, az_ref, g_ref, ws_ref, bs_ref, o_ref):
    tm = au_ref.shape[0]
    v = _gelu(av_ref[...].astype(F32))
    ms = jnp.mean(v * v, axis=-1, keepdims=True)
    vb = (v * lax.rsqrt(ms + EPS) * g_ref[...]).astype(BF16)
    gate = _gelu(au_ref[...].astype(F32)) * _silu(az_ref[...].astype(F32))
    for c in range(tm // CHUNK):
        rows = slice(c * CHUNK, (c + 1) * CHUNK)
        for g in range(A_GROUPS):
            cols = slice(g * LANES, (g + 1) * LANES)
            mixed = jnp.dot(ws_ref[g], vb[rows, cols], preferred_element_type=F32) + bs_ref[:, g:g + 1]
            o_ref[rows, cols] = (gate[rows, cols] * mixed).astype(BF16)


def _branch_a(z, g_v, w_s, b_sT, tm):
    B, S, _ = z.shape
    W = BRANCH_W
    col = lambda cb: pl.BlockSpec((None, tm, W), lambda b, i: (b, i, cb))
    return pl.pallas_call(
        _branch_a_kernel,
        grid=(B, S // tm),
        in_specs=[col(COL_AU), col(COL_AV), col(COL_AZ),
                  pl.BlockSpec((1, W), lambda b, i: (0, 0)),
                  pl.BlockSpec((A_GROUPS, CHUNK, CHUNK), lambda b, i: (0, 0, 0)),
                  pl.BlockSpec((CHUNK, A_GROUPS), lambda b, i: (0, 0))],
        out_specs=pl.BlockSpec((None, tm, W), lambda b, i: (b, i, 0)),
        out_shape=jax.ShapeDtypeStruct((B, S, W), BF16),
        compiler_params=_cparams(2),
        name="branch_a",
    )(z, z, z, g_v, w_s, b_sT)


def _dftw_kernel(c_ref, s_ref, w_ref, cw_ref, sw_ref):
    for g in range(C_GROUPS):
        w = w_ref[g]
        cw_ref[g] = jnp.dot(c_ref[...], w, preferred_element_type=F32,
                            precision=lax.Precision.HIGHEST).astype(BF16)
        sw_ref[g] = jnp.dot(s_ref[...], w, preferred_element_type=F32,
                            precision=lax.Precision.HIGHEST).astype(BF16)


def _dft_weights(c128, s128, w_f):
    shp = jax.ShapeDtypeStruct((C_GROUPS, C_GW, C_GW), BF16)
    return pl.pallas_call(_dftw_kernel, out_shape=[shp, shp], name="dft_weights")(c128, s128, w_f)


def _fourier_pq_kernel(xa_ref, xm_ref, xn_ref, xh_ref, jm_ref, cw_ref, sw_ref, pq_ref, aux_ref):
    i = pl.program_id(1)
    tm = xa_ref.shape[0]
    row = lax.broadcasted_iota(jnp.int32, (tm, 1), 0)
    first = jnp.where(i > 0, xn_ref[0:1, :].astype(F32), 0.0)
    xb = jnp.where(row == 0, first, jnp.dot(jm_ref[...], xm_ref[...], preferred_element_type=F32))
    xa = xa_ref[...].astype(F32)
    xe = (xa + xb).astype(BF16)
    xo = (xa - xb).astype(BF16)
    for g in range(C_GROUPS):
        cols = slice(g * LANES, (g + 1) * LANES)
        pq_ref[0, :, cols] = jnp.dot(xe[:, cols], cw_ref[g], preferred_element_type=F32).astype(BF16)
        pq_ref[1, :, cols] = jnp.dot(xo[:, cols], sw_ref[g], preferred_element_type=F32).astype(BF16)

    @pl.when(i == 0)
    def _():
        for g in range(C_GROUPS):
            cols = slice(g * LANES, (g + 1) * LANES)
            aux_ref[:, cols] = jnp.dot(xh_ref[:, cols], cw_ref[g], preferred_element_type=F32)


def _fourier_pq(z, jm, cw, sw, tm):
    B, L, _ = z.shape
    H = L // 2
    W = BRANCH_W
    nt = H // tm
    wspec = pl.BlockSpec((C_GROUPS, C_GW, C_GW), lambda b, i: (0, 0, 0))
    return pl.pallas_call(
        _fourier_pq_kernel,
        grid=(B, nt),
        in_specs=[pl.BlockSpec((None, tm, W), lambda b, i: (b, i, COL_FIN)),
                  pl.BlockSpec((None, tm, W), lambda b, i: (b, 2 * nt - 1 - i, COL_FIN)),
                  pl.BlockSpec((None, 8, W),
                               lambda b, i: (b, jnp.minimum((L - i * tm) // 8, L // 8 - 1), COL_FIN)),
                  pl.BlockSpec((None, 8, W), lambda b, i: (b, H // 8, COL_FIN)),
                  pl.BlockSpec((tm, tm), lambda b, i: (0, 0)),
                  wspec, wspec],
        out_specs=[pl.BlockSpec((None, 2, tm, W), lambda b, i: (b, 0, i, 0)),
                   pl.BlockSpec((None, 8, W), lambda b, i: (b, 0, 0))],
        out_shape=[jax.ShapeDtypeStruct((B, 2, H, W), BF16),
                   jax.ShapeDtypeStruct((B, 8, W), F32)],
        compiler_params=_cparams(2),
        name="fourier_pq",
    )(z, z, z, z, jm, cw, sw)


def _fourier_mix_kernel(dft_ref, dftx_ref, pq_ref, aux_ref, jm_ref, fza_ref, fzb_ref, u_ref, *, scale, half):
    tm = dft_ref.shape[0]
    pe = pq_ref[0:half, :]
    qo = pq_ref[half:, :]
    ev = jnp.dot(dft_ref[:, 0:half], pe, preferred_element_type=F32)
    od = jnp.dot(dft_ref[:, half:], qo, preferred_element_type=F32)
    row = lax.broadcasted_iota(jnp.int32, (tm, 1), 0)
    nyq = (1 - 2 * (row & 1)).astype(F32) * aux_ref[0:1, :]
    y1 = ev + od + nyq
    y2 = ev - od + nyq
    evx = jnp.dot(dftx_ref[:, 0:half], pe, preferred_element_type=F32)
    odx = jnp.dot(dftx_ref[:, half:], qo, preferred_element_type=F32)
    y2x = evx[0:1, :] - odx[0:1, :] + aux_ref[0:1, :]
    hi = y2.astype(BF16)
    lo = (y2 - hi.astype(F32)).astype(BF16)
    y2r = (jnp.dot(jm_ref[...], hi, preferred_element_type=F32)
           + jnp.dot(jm_ref[...], lo, preferred_element_type=F32))
    y2r = jnp.where(row == 0, y2x, y2r)
    u_ref[0] = (y1 * scale * _silu(fza_ref[...].astype(F32))).astype(BF16)
    u_ref[1] = (y2r * scale * _silu(fzb_ref[...].astype(F32))).astype(BF16)


def _fourier_mix(dft, pq, aux, jm, z, tm):
    B, L, _ = z.shape
    H = L // 2
    W = BRANCH_W
    nt = H // tm
    scale = 1.0 / math.sqrt(L * C_GW)
    return pl.pallas_call(
        functools.partial(_fourier_mix_kernel, scale=scale, half=H),
        grid=(B, nt),
        in_specs=[pl.BlockSpec((tm, L), lambda b, i: (i, 0)),
                  pl.BlockSpec((8, L), lambda b, i: ((i + 1) * (tm // 8), 0)),
                  pl.BlockSpec((None, L, W), lambda b, i: (b, 0, 0)),
                  pl.BlockSpec((None, 8, W), lambda b, i: (b, 0, 0)),
                  pl.BlockSpec((tm, tm), lambda b, i: (0, 0)),
                  pl.BlockSpec((None, tm, W), lambda b, i: (b, i, COL_FZ)),
                  pl.BlockSpec((None, tm, W), lambda b, i: (b, 2 * nt - 1 - i, COL_FZ))],
        out_specs=pl.BlockSpec((None, 2, tm, W), lambda b, i: (b, 0, i, 0)),
        out_shape=jax.ShapeDtypeStruct((B, 2, H, W), BF16),
        compiler_params=_cparams(2),
        name="fourier_mix",
    )(dft, dft, pq, aux, jm, z, z)


def _fourier_branch(z, cw, sw, dft, jm, tm):
    B, L, _ = z.shape
    pq, aux = _fourier_pq(z, jm, cw, sw, tm)
    return _fourier_mix(dft, pq.reshape(B, L, BRANCH_W), aux, jm, z, tm)


def _lane_masks(heads, hd):
    lane = lax.broadcasted_iota(jnp.int32, (1, LANES), 1)
    return [((lane >= g * hd) & (lane < (g + 1) * hd)) for g in heads]


def _attend(q, segs, sink_col, heads, hd):
    tq = q.shape[0]
    masks = _lane_masks(heads, hd)
    zero = jnp.zeros_like(q)
    qx = jnp.concatenate([jnp.where(m, q, zero) for m in masks], axis=0)
    scores = []
    for k, _, bias in segs:
        s = lax.dot_general(qx, k, (((1,), (1,)), ((), ())), preferred_element_type=F32)
        if bias is not None:
            s = s + bias
        scores.append(s)
    m = scores[0].max(axis=-1, keepdims=True)
    for s in scores[1:]:
        m = jnp.maximum(m, s.max(axis=-1, keepdims=True))
    if sink_col is not None:
        m = jnp.maximum(m, sink_col)
    den = None
    acc = None
    for s, (_, v, _) in zip(scores, segs):
        p = jnp.exp2(s - m)
        d = p.sum(axis=-1, keepdims=True)
        o = jnp.dot(p.astype(BF16), v, preferred_element_type=F32)
        den = d if den is None else den + d
        acc = o if acc is None else acc + o
    if sink_col is not None:
        den = den + jnp.exp2(sink_col - m)
    acc = acc / den
    out = None
    for g, msk in enumerate(masks):
        part = jnp.where(msk, acc[g * tq:(g + 1) * tq], 0.0)
        out = part if out is None else out + part
    return out


def _attend_blocks(q_list, local_list, ctx_kv, sink_col, heads, hd):
    tq = q_list[0].shape[0]
    masks = _lane_masks(heads, hd)
    rows = len(masks) * tq
    nt = (((1,), (1,)), ((), ()))
    qx_list = []
    for q in q_list:
        zero = jnp.zeros_like(q)
        qx_list.append(jnp.concatenate([jnp.where(m, q, zero) for m in masks], axis=0))
    kc, vc = ctx_kv
    s_c = lax.dot_general(jnp.concatenate(qx_list, axis=0), kc, nt, preferred_element_type=F32)
    p_c_list, den_list, o_list = [], [], []
    for j, (qx, (k, v, bias)) in enumerate(zip(qx_list, local_list)):
        s_cj = s_c[j * rows:(j + 1) * rows]
        s = lax.dot_general(qx, k, nt, preferred_element_type=F32) + bias
        m = jnp.maximum(s.max(axis=-1, keepdims=True), s_cj.max(axis=-1, keepdims=True))
        if sink_col is not None:
            m = jnp.maximum(m, sink_col)
        p = jnp.exp2(s - m)
        p_c = jnp.exp2(s_cj - m)
        den = p.sum(axis=-1, keepdims=True) + p_c.sum(axis=-1, keepdims=True)
        if sink_col is not None:
            den = den + jnp.exp2(sink_col - m)
        p_c_list.append(p_c.astype(BF16))
        den_list.append(den)
        o_list.append(jnp.dot(p.astype(BF16), v, preferred_element_type=F32))
    o_c = jnp.dot(jnp.concatenate(p_c_list, axis=0), vc, preferred_element_type=F32)
    outs = []
    for j, (o, den) in enumerate(zip(o_list, den_list)):
        acc = (o + o_c[j * rows:(j + 1) * rows]) / den
        out = None
        for g, msk in enumerate(masks):
            part = jnp.where(msk, acc[g * tq:(g + 1) * tq], 0.0)
            out = part if out is None else out + part
        outs.append(out)
    return outs


def _branch_b_kernel(sink_ref, q_ref, kv_ref, ckv_ref, bz_ref, o_ref, *, seq):
    nl = 3 * B_BLK
    half = BRANCH_W // 2
    n_blk = q_ref.shape[0] // B_BLK
    starts, biases = [], []
    for blk in range(n_blk):
        i = pl.program_id(1) * n_blk + blk
        start = pl.multiple_of(jnp.clip((i - 1) * B_BLK, 0, seq - nl), B_BLK)
        qpos = i * B_BLK + lax.broadcasted_iota(jnp.int32, (B_BLK, nl), 0)
        kpos = start + lax.broadcasted_iota(jnp.int32, (B_BLK, nl), 1)
        bias1 = jnp.where(jnp.abs(kpos - qpos) <= B_WIN, 0.0, NEG).astype(F32)
        starts.append(start)
        biases.append(jnp.concatenate([bias1, bias1], axis=0))
    for grp in range(BRANCH_W // LANES):
        kvh = grp // 2
        kc = slice(kvh * LANES, (kvh + 1) * LANES)
        vc = slice(half + kvh * LANES, half + (kvh + 1) * LANES)
        cols = slice(grp * LANES, (grp + 1) * LANES)
        sink_col = jnp.concatenate(
            [jnp.full((B_BLK, 1), sink_ref[2 * grp + g] * LOG2E, F32) for g in range(2)], axis=0)
        q_list = [q_ref[blk * B_BLK:(blk + 1) * B_BLK, cols] for blk in range(n_blk)]
        local = [(kv_ref[pl.ds(s, nl), kc], kv_ref[pl.ds(s, nl), vc], b) for s, b in zip(starts, biases)]
        outs = _attend_blocks(q_list, local, (ckv_ref[:, kc], ckv_ref[:, vc]), sink_col, range(2), B_HD)
        for blk, o in enumerate(outs):
            rows = slice(blk * B_BLK, (blk + 1) * B_BLK)
            o_ref[rows, cols] = (o * _silu(bz_ref[rows, cols].astype(F32))).astype(BF16)


def _branch_b(z, zc, sink, tq):
    B, S, _ = z.shape
    C = zc.shape[1]
    W = BRANCH_W
    return pl.pallas_call(
        functools.partial(_branch_b_kernel, seq=S),
        grid=(B, S // tq),
        in_specs=[pl.BlockSpec(memory_space=pltpu.SMEM),
                  pl.BlockSpec((None, tq, W), lambda b, i: (b, i, COL_BQ)),
                  pl.BlockSpec((None, S, W), lambda b, i: (b, 0, COL_BKV)),
                  pl.BlockSpec((None, C, W), lambda b, i: (b, 0, COL_BKV)),
                  pl.BlockSpec((None, tq, W), lambda b, i: (b, i, COL_BZ))],
        out_specs=pl.BlockSpec((None, tq, W), lambda b, i: (b, i, 0)),
        out_shape=jax.ShapeDtypeStruct((B, S, W), BF16),
        compiler_params=_cparams(2),
        name="branch_b",
    )(sink, z, z, zc, z)


def _ctx_b_kernel(sink_ref, q_ref, kv_ref, bz_ref, o_ref):
    tq = q_ref.shape[0]
    half = BRANCH_W // 2
    for grp in range(BRANCH_W // LANES):
        kvh = grp // 2
        kc = slice(kvh * LANES, (kvh + 1) * LANES)
        vc = slice(half + kvh * LANES, half + (kvh + 1) * LANES)
        cols = slice(grp * LANES, (grp + 1) * LANES)
        sink_col = jnp.concatenate(
            [jnp.full((tq, 1), sink_ref[2 * grp + g] * LOG2E, F32) for g in range(2)], axis=0)
        o = _attend(q_ref[:, cols], [(kv_ref[:, kc], kv_ref[:, vc], None)], sink_col, range(2), B_HD)
        o_ref[:, cols] = (o * _silu(bz_ref[:, cols].astype(F32))).astype(BF16)


def _ctx_b(zc, sink):
    B, C, _ = zc.shape
    W = BRANCH_W
    col = lambda cb: pl.BlockSpec((None, C, W), lambda b: (b, 0, cb))
    return pl.pallas_call(
        _ctx_b_kernel,
        grid=(B,),
        in_specs=[pl.BlockSpec(memory_space=pltpu.SMEM), col(COL_BQ), col(COL_BKV), col(COL_BZ)],
        out_specs=pl.BlockSpec((None, C, W), lambda b: (b, 0, 0)),
        out_shape=jax.ShapeDtypeStruct((B, C, W), BF16),
        compiler_params=_cparams(1),
        name="ctx_b",
    )(sink, zc, zc, zc)


def _na_row_start(r, rows):
    return jnp.clip(r - NA_ROWS // 2, 0, rows - NA_ROWS)


def _branch_d_kernel(q_ref, k_ref, v_ref, ck_ref, cv_ref, dz_ref, *rest, rows):
    bias_refs, o_ref = rest[:-1], rest[-1]
    nk = NA_ROWS * GRID_W
    hpg = LANES // D_HD
    n_rows = len(bias_refs)
    starts = [pl.multiple_of(_na_row_start(pl.program_id(1) * n_rows + rr, rows) * GRID_W, GRID_W)
              for rr in range(n_rows)]
    for grp in range(BRANCH_W // LANES):
        cols = slice(grp * LANES, (grp + 1) * LANES)
        q_list = [q_ref[rr * GRID_W:(rr + 1) * GRID_W, cols] for rr in range(n_rows)]
        local = [(k_ref[pl.ds(s, nk), cols], v_ref[pl.ds(s, nk), cols],
                  b[grp * hpg:(grp + 1) * hpg].reshape(hpg * GRID_W, nk)) for s, b in zip(starts, bias_refs)]
        outs = _attend_blocks(q_list, local, (ck_ref[:, cols], cv_ref[:, cols]), None, range(hpg), D_HD)
        for rr, o in enumerate(outs):
            qrows = slice(rr * GRID_W, (rr + 1) * GRID_W)
            o_ref[qrows, cols] = (o * _silu(dz_ref[qrows, cols].astype(F32))).astype(BF16)


def _branch_d(z, zc, bias_tab, rps):
    B, S, _ = z.shape
    C = zc.shape[1]
    W = BRANCH_W
    rows = S // GRID_W
    nk = NA_ROWS * GRID_W
    tile = lambda cb: pl.BlockSpec((None, rps * GRID_W, W), lambda b, i: (b, i, cb))
    full = lambda cb: pl.BlockSpec((None, S, W), lambda b, i: (b, 0, cb))
    ctx = lambda cb: pl.BlockSpec((None, C, W), lambda b, i: (b, 0, cb))

    def bias_spec(rr):
        def index(b, i):
            r = i * rps + rr
            return (r - _na_row_start(r, rows), 0, 0, 0)
        return pl.BlockSpec((None, D_H, GRID_W, nk), index)

    return pl.pallas_call(
        functools.partial(_branch_d_kernel, rows=rows),
        grid=(B, rows // rps),
        in_specs=[tile(COL_DQ), full(COL_DK), full(COL_DV), ctx(COL_DK), ctx(COL_DV), tile(COL_DZ)]
                 + [bias_spec(rr) for rr in range(rps)],
        out_specs=pl.BlockSpec((None, rps * GRID_W, W), lambda b, i: (b, i, 0)),
        out_shape=jax.ShapeDtypeStruct((B, S, W), BF16),
        compiler_params=_cparams(2),
        name="branch_d",
    )(z, z, z, zc, zc, z, *([bias_tab] * rps))


def _ctx_d_kernel(q_ref, k_ref, v_ref, dz_ref, o_ref):
    hpg = LANES // D_HD
    for grp in range(BRANCH_W // LANES):
        cols = slice(grp * LANES, (grp + 1) * LANES)
        o = _attend(q_ref[:, cols], [(k_ref[:, cols], v_ref[:, cols], None)], None, range(hpg), D_HD)
        o_ref[:, cols] = (o * _silu(dz_ref[:, cols].astype(F32))).astype(BF16)


def _ctx_d(zc):
    B, C, _ = zc.shape
    W = BRANCH_W
    col = lambda cb: pl.BlockSpec((None, C, W), lambda b: (b, 0, cb))
    return pl.pallas_call(
        _ctx_d_kernel,
        grid=(B,),
        in_specs=[col(COL_DQ), col(COL_DK), col(COL_DV), col(COL_DZ)],
        out_specs=pl.BlockSpec((None, C, W), lambda b: (b, 0, 0)),
        out_shape=jax.ShapeDtypeStruct((B, C, W), BF16),
        compiler_params=_cparams(1),
        name="ctx_d",
    )(zc, zc, zc, zc)


def _gate_mix_kernel(n_ref, ua_ref, ub_ref, uc_ref, ud_ref, wg_ref, wb_ref, o_ref):
    n = n_ref[...]
    mix = None
    for b, u_ref in enumerate((ua_ref, ub_ref, uc_ref, ud_ref)):
        gate = _sigmoid(jnp.dot(n, wg_ref[b], preferred_element_type=F32))
        t = gate * jnp.dot(u_ref[...], wb_ref[b], preferred_element_type=F32)
        mix = t if mix is None else mix + t
    o_ref[...] = mix.astype(BF16)


def _gate_mix(n, us, wg, wb, tm, cn):
    B, S, D = n.shape
    W = BRANCH_W
    row = lambda width: pl.BlockSpec((None, tm, width), lambda j, b, i: (b, i, 0))
    uc_spec = row(W)
    if us[2].ndim == 4:
        nt = S // (2 * tm)
        uc_spec = pl.BlockSpec((None, None, tm, W),
                               lambda j, b, i: (b, i // nt, jnp.where(i < nt, i, 2 * nt - 1 - i), 0))
    return pl.pallas_call(
        _gate_mix_kernel,
        grid=(D // cn, B, S // tm),
        in_specs=[row(D), row(W), row(W), uc_spec, row(W),
                  pl.BlockSpec((N_BRANCH, D, cn), lambda j, b, i: (0, 0, j)),
                  pl.BlockSpec((N_BRANCH, W, cn), lambda j, b, i: (0, 0, j))],
        out_specs=pl.BlockSpec((None, tm, cn), lambda j, b, i: (b, i, j)),
        out_shape=jax.ShapeDtypeStruct((B, S, D), BF16),
        compiler_params=_cparams(3),
        name="gate_mix",
    )(n, *us, wg, wb)


def _out_proj_kernel(h_ref, mod_ref, mix_ref, wo_ref, o_ref):
    D = h_ref.shape[-1]
    y = jnp.dot(mix_ref[...], wo_ref[...], preferred_element_type=F32)
    o_ref[...] = h_ref[...] + mod_ref[:, 2 * D:3 * D] * y


def _out_proj(h, mod_l, mod_row, mix, wo, tm):
    B, S, D = h.shape
    row = lambda: pl.BlockSpec((None, tm, D), lambda b, i: (b, i, 0))
    return pl.pallas_call(
        _out_proj_kernel,
        grid=(B, S // tm),
        in_specs=[row(),
                  pl.BlockSpec((None, 1, 3 * D), lambda b, i: (mod_row(b), 0, 0)),
                  row(),
                  pl.BlockSpec((D, D), lambda b, i: (0, 0), pipeline_mode=pl.Buffered(1))],
        out_specs=row(),
        out_shape=jax.ShapeDtypeStruct((B, S, D), F32),
        compiler_params=_cparams(2),
        name="out_proj",
    )(h, mod_l, mix, wo)


def _merge(h, mod_l, mod_row, n, us, wg, wb, wo, tm, cn):
    return _out_proj(h, mod_l, mod_row, _gate_mix(n, us, wg, wb, tm, cn), wo, tm)


def _rearrange_w_in(w_in):
    W = BRANCH_W
    o = 0
    parts = {}
    for name, size in (("a_u", W), ("a_v", W), ("a_z", W), ("b_q", W), ("b_k", B_KVH * B_HD),
                       ("b_v", B_KVH * B_HD), ("b_z", W), ("f_in", W), ("f_z", W),
                       ("d_q", W), ("d_k", W), ("d_v", W), ("d_z", W)):
        parts[name] = w_in[:, :, o:o + size]
        o += size

    def dup(p):
        return jnp.concatenate([p[:, :, 0:B_HD], p[:, :, 0:B_HD], p[:, :, B_HD:], p[:, :, B_HD:]], axis=-1)

    cols = [parts["a_u"], parts["a_v"], parts["a_z"], parts["b_q"], dup(parts["b_k"]), dup(parts["b_v"]),
            parts["b_z"], parts["f_in"], parts["f_z"], parts["d_q"], parts["d_k"], parts["d_v"], parts["d_z"]]
    return jnp.concatenate(cols, axis=-1).astype(BF16)


def _rope_tables(S):
    t = np.arange(S)
    pos = np.stack([t // GRID_W, t % GRID_W], axis=1).astype(np.float32)
    lane = np.arange(B_HD)
    which = lane // (B_HD // 2)
    fi = lane % (B_HD // 4)
    inv = (ROPE_BASE ** (-(fi.astype(np.float32)) / (B_HD // 4))).astype(np.float32)
    ang = pos[:, which] * inv[None, :]
    cos, sin = np.cos(ang), np.sin(ang)
    lower = (lane % (B_HD // 2)) < (B_HD // 4)
    sina = np.where(lower[None, :], -sin, 0.0)
    sinb = np.where(lower[None, :], 0.0, sin)
    rep = lambda a: jnp.asarray(np.tile(a.astype(np.float32), (1, LANES // B_HD)))
    return rep(cos), rep(sina), rep(sinb)


def _block_diag_mean(width, hd):
    i = np.arange(width)
    return jnp.asarray(((i[:, None] // hd) == (i[None, :] // hd)).astype(np.float32) / hd, dtype=BF16)


def _dft_cos_sin(n):
    k = jnp.arange(n, dtype=jnp.int32)
    m = (k[:, None] * k[None, :]) % n
    ang = m.astype(F32) * (2.0 * math.pi / n)
    return jnp.cos(ang), jnp.sin(ang)


def _dft_half_matrix(n):
    k = jnp.arange(n // 2 + 8, dtype=jnp.int32)
    m = jnp.arange(n // 2, dtype=jnp.int32)
    ang = ((k[:, None] * m[None, :]) % n).astype(F32) * (2.0 * math.pi / n)
    return jnp.concatenate([jnp.cos(ang), -jnp.sin(ang)], axis=1).astype(BF16)


def _rev_shift_matrix(tm):
    r = np.arange(tm)
    return jnp.asarray(((r[:, None] >= 1) & (r[None, :] == tm - r[:, None])).astype(np.float32), dtype=BF16)


def _bias_expand_kernel(rpb_ref, e_ref, o_ref):
    o_ref[...] = LOG2E * jnp.dot(rpb_ref[...], e_ref[...], preferred_element_type=F32,
                                 precision=lax.Precision.HIGHEST)


def _na_bias_tables(rpb):
    L, H, n_dr, n_dc = rpb.shape
    c = np.arange(GRID_W)
    c0 = np.clip(c - NA_COLS // 2, 0, GRID_W - NA_COLS)
    kc = np.arange(GRID_W)
    inwin = (kc[None, :] >= c0[:, None]) & (kc[None, :] < c0[:, None] + NA_COLS)
    dc = kc[None, :] - c[:, None] + (NA_COLS - 1)
    n_dc_pad = 32
    onehot = (np.arange(n_dc_pad)[:, None, None] == dc[None]) & inwin[None]
    e2d = jnp.asarray(onehot.reshape(n_dc_pad, GRID_W * GRID_W).astype(np.float32))
    rpb2d = jnp.pad(rpb.reshape(L, H * n_dr, n_dc), ((0, 0), (0, 0), (0, n_dc_pad - n_dc)))
    t = pl.pallas_call(
        _bias_expand_kernel,
        grid=(L,),
        in_specs=[pl.BlockSpec((None, H * n_dr, n_dc_pad), lambda l: (l, 0, 0)),
                  pl.BlockSpec((n_dc_pad, GRID_W * GRID_W), lambda l: (0, 0))],
        out_specs=pl.BlockSpec((None, H * n_dr, GRID_W * GRID_W), lambda l: (l, 0, 0)),
        out_shape=jax.ShapeDtypeStruct((L, H * n_dr, GRID_W * GRID_W), F32),
        compiler_params=_cparams(1),
        name="bias_expand",
    )(rpb2d, e2d)
    t = t.reshape(L, H, n_dr, GRID_W, GRID_W)
    t = jnp.where(jnp.asarray(inwin)[None, None, None], t, NEG)
    t = t.transpose(0, 1, 3, 2, 4).reshape(L, H, GRID_W, n_dr * GRID_W)
    nk = NA_ROWS * GRID_W
    cases = [t[..., (NA_ROWS - 1 - d) * GRID_W:(NA_ROWS - 1 - d) * GRID_W + nk] for d in range(NA_ROWS)]
    return jnp.stack(cases, axis=1)


def kernel(x, c, ctx, c_ctx, norm_g, w_ada, b_ada, w_in, a_norm_g, a_w_s, a_b_s, b_q_g, b_k_g, b_sink,
           c_w_f, d_q_g, d_k_g, d_rpb, w_gate, w_branch, w_out):
    B, S, D = x.shape
    C = ctx.shape[1]
    L = norm_g.shape[0]
    assert S % 512 == 0 and S // GRID_W >= NA_ROWS and S >= 3 * B_BLK and C % CHUNK == 0
    tm, tmc = 512, C
    tm_ctx = math.gcd(B * C, 512)
    cn = 512
    tm_in = 512
    tq_b = 8 * B_BLK
    rps_d = 4
    W = BRANCH_W

    n_rows = -(-(B + 1) // 8) * 8
    cc = jnp.concatenate([c, c_ctx[None], jnp.zeros((n_rows - B - 1, D), F32)], axis=0)
    mod = _modulation(cc, w_ada, b_ada).reshape(L, n_rows, 1, 3 * D)
    lat_row = lambda b: b
    ctx_row = lambda b: B

    w_in_r = _rearrange_w_in(w_in)
    w_gate_b, w_branch_b, w_out_b = w_gate.astype(BF16), w_branch.astype(BF16), w_out.astype(BF16)
    a_ws_b = a_w_s.astype(BF16)
    gains = jnp.stack([jnp.tile(b_q_g, (1, W // B_HD)),
                       jnp.tile(b_k_g, (1, W // B_HD)),
                       jnp.tile(d_q_g, (1, W // D_HD)),
                       jnp.tile(d_k_g, (1, W // D_HD))] + [jnp.zeros((L, W), F32)] * 4, axis=1)
    bd64, bd32 = _block_diag_mean(W // 2, B_HD), _block_diag_mean(W // 2, D_HD)
    ropes = _rope_tables(S)
    c128, s128 = _dft_cos_sin(C_GW)
    tm_f, tm_fc = min(512, S // 2), min(512, C // 2)
    dft_lat, dft_ctx = _dft_half_matrix(S), _dft_half_matrix(C)
    jm_lat, jm_ctx = _rev_shift_matrix(tm_f), _rev_shift_matrix(tm_fc)
    assert tm_f == tm and tm_fc == C // 2
    bias_tabs = _na_bias_tables(d_rpb)

    h, hc = x, ctx.reshape(1, B * C, D)
    for l in range(L):
        last = l == L - 1
        mod_l = mod[l]
        ng = norm_g[l][None]
        n_lat, z = _inproj(h, mod_l, lat_row, ng, w_in_r[l], gains[l], bd64, bd32, ropes, tm_in, True)
        n_ctx, zc = _inproj(hc, mod_l, ctx_row, ng, w_in_r[l], gains[l], bd64, bd32, ropes, tm_ctx, False)
        zc = zc.reshape(B, C, Z_WIDTH)
        cw, sw = _dft_weights(c128, s128, c_w_f[l])
        g_v = a_norm_g[l][None]
        b_sT = a_b_s[l].T
        bias_tab = bias_tabs[l]

        u_lat = [
            _branch_a(z, g_v, a_ws_b[l], b_sT, tm),
            _branch_b(z, zc, b_sink[l], tq_b),
            _fourier_branch(z, cw, sw, dft_lat, jm_lat, tm_f),
            _branch_d(z, zc, bias_tab, rps_d),
        ]
        h = _merge(h, mod_l, lat_row, n_lat, u_lat, w_gate_b[l], w_branch_b[l], w_out_b[l], tm, cn)
        if not last:
            u_ctx = [
                _branch_a(zc, g_v, a_ws_b[l], b_sT, tmc),
                _ctx_b(zc, b_sink[l]),
                _fourier_branch(zc, cw, sw, dft_ctx, jm_ctx, tm_fc).reshape(B, C, W),
                _ctx_d(zc),
            ]
            u_ctx = [u.reshape(1, B * C, W) for u in u_ctx]
            hc = _merge(hc, mod_l, ctx_row, n_ctx, u_ctx,
                        w_gate_b[l], w_branch_b[l], w_out_b[l], tm_ctx, cn)
    return h
```

```python
import functools
import math

import jax
import jax.numpy as jnp
import numpy as np
from jax import lax
from jax.experimental import pallas as pl
from jax.experimental.pallas import tpu as pltpu

F32 = jnp.float32
BF16 = jnp.bfloat16

LANES = 128
V7X_VMEM_LIMIT_BYTES = 56 * 1024 * 1024

GRID_W = 64
N_BRANCH = 4
BRANCH_W = 512
CHUNK = 128
A_GROUPS = 4
B_HD = 64
B_QH = 8
B_KVH = 2
B_WIN = 128
B_BLK = 128
C_GROUPS = 4
C_GW = 128
D_HD = 32
D_H = 16
NA_ROWS = 8
NA_COLS = 16
ROPE_BASE = 10000.0
EPS = 1e-6
NEG = -1e30
LOG2E = math.log2(math.e)

(COL_AU, COL_AV, COL_AZ, COL_BQ, COL_BKV, COL_BZ,
 COL_FIN, COL_FZ, COL_DQ, COL_DK, COL_DV, COL_DZ) = range(12)
Z_WIDTH = 12 * BRANCH_W
W_IN_SRC = {COL_AU: 0, COL_AV: 512, COL_AZ: 1024, COL_BQ: 1536, COL_BZ: 2304, COL_FIN: 2816, COL_FZ: 3328,
            COL_DQ: 3840, COL_DK: 4352, COL_DV: 4864, COL_DZ: 5376}
W_IN_BK = 2048


def _cparams(n_grid, flags=None):
    return pltpu.CompilerParams(
        dimension_semantics=("arbitrary",) * n_grid,
        vmem_limit_bytes=V7X_VMEM_LIMIT_BYTES,
        flags=flags)


def _sigmoid(x):
    return 1.0 / (1.0 + jnp.exp(-x))


def _silu(x):
    return x * _sigmoid(x)


def _gelu(x):
    return 0.5 * x * (1.0 + jnp.tanh(math.sqrt(2.0 / math.pi) * (x + 0.044715 * (x * x * x))))


def _mod_kernel(c_ref, w_ref, b_ref, o_ref):
    c = c_ref[...]
    a = _silu(c).astype(BF16)
    o_ref[...] = jnp.dot(a, w_ref[...].astype(BF16), preferred_element_type=F32) + b_ref[...]


def _modulation(cc, w_ada, b_ada):
    L, D, N = w_ada.shape
    R = cc.shape[0]
    tn = 512
    return pl.pallas_call(
        _mod_kernel,
        grid=(L, N // tn),
        in_specs=[pl.BlockSpec((R, D), lambda l, j: (0, 0)),
                  pl.BlockSpec((None, D, tn), lambda l, j: (l, 0, j)),
                  pl.BlockSpec((None, 1, tn), lambda l, j: (l, 0, j))],
        out_specs=pl.BlockSpec((None, R, tn), lambda l, j: (l, 0, j)),
        out_shape=jax.ShapeDtypeStruct((L, R, N), F32),
        compiler_params=_cparams(2),
        name="modulation",
    )(cc, w_ada, b_ada.reshape(L, 1, N))


def _head_rms(x, bd_ref, width):
    sq = (x * x).astype(BF16)
    c = min(bd_ref.shape[0], width)
    bd = bd_ref[0:c, 0:c]
    parts = [jnp.dot(sq[:, o:o + c], bd, preferred_element_type=F32) for o in range(0, width, c)]
    return parts[0] if len(parts) == 1 else jnp.concatenate(parts, axis=1)


def _rope(x, cos, sina, sinb):
    w = x.shape[-1]
    reps = w // LANES
    cos = jnp.concatenate([cos] * reps, axis=1)
    sina = jnp.concatenate([sina] * reps, axis=1)
    sinb = jnp.concatenate([sinb] * reps, axis=1)
    up = pltpu.roll(x, w - 16, 1)
    dn = pltpu.roll(x, 16, 1)
    return x * cos + up * sina + dn * sinb


def _ada_norm(x, mod_ref, ng_ref):
    D = x.shape[-1]
    ms = jnp.mean(x * x, axis=-1, keepdims=True)
    y = x * lax.rsqrt(ms + EPS) * ng_ref[...]
    return (y * (1.0 + mod_ref[:, D:2 * D]) + mod_ref[:, 0:D]).astype(BF16)


def _dup_heads(t):
    swapped = pltpu.roll(t, B_HD, 1)
    low = lax.broadcasted_iota(jnp.int32, (1, LANES), 1) < B_HD
    return jnp.concatenate([jnp.where(low, t, swapped), jnp.where(low, swapped, t)], axis=1)


def _inproj_kernel(h_ref, mod_ref, ng_ref, w_ref, gains_ref, bd64_ref, bd32_ref,
                   cos_ref, sina_ref, sinb_ref, n_ref, z_ref, *, rope):
    W = BRANCH_W
    nb = _ada_norm(h_ref[...], mod_ref, ng_ref)
    n_ref[...] = nb

    def maybe_rope(t):
        if rope:
            return _rope(t, cos_ref[...], sina_ref[...], sinb_ref[...])
        return t

    def head_norm(t, bd_ref, gain):
        return t * lax.rsqrt(_head_rms(t, bd_ref, t.shape[-1]) + EPS) * gain

    for cb in range(Z_WIDTH // W):
        if cb == COL_BKV:
            kvw = B_KVH * B_HD
            kv = jnp.dot(nb, w_ref[:, W_IN_BK:W_IN_BK + 2 * kvw], preferred_element_type=F32)
            k = maybe_rope(head_norm(kv[:, 0:kvw], bd64_ref, gains_ref[1:2, 0:kvw]))
            acc = jnp.concatenate([_dup_heads(k), _dup_heads(kv[:, kvw:])], axis=1)
        else:
            src = W_IN_SRC[cb]
            acc = jnp.dot(nb, w_ref[:, src:src + W], preferred_element_type=F32)
            if cb == COL_BQ:
                acc = maybe_rope(head_norm(acc, bd64_ref, gains_ref[0:1, :])) * (B_HD ** -0.5 * LOG2E)
            elif cb == COL_DQ:
                acc = head_norm(acc, bd32_ref, gains_ref[2:3, :]) * (D_HD ** -0.5 * LOG2E)
            elif cb == COL_DK:
                acc = head_norm(acc, bd32_ref, gains_ref[3:4, :])
        z_ref[:, cb * W:(cb + 1) * W] = acc.astype(BF16)


def _inproj(h, mod, mod_row, norm_g, w_in, gains, bd64, bd32, ropes, tm, rope, l):
    B, S, D = h.shape
    cos, sina, sinb = ropes
    layer = lambda shape: pl.BlockSpec((None,) + shape, lambda b, i: (l,) + (0,) * len(shape),
                                       pipeline_mode=pl.Buffered(1))
    const = lambda shape: pl.BlockSpec(shape, lambda b, i: (0,) * len(shape), pipeline_mode=pl.Buffered(1))
    rope_spec = pl.BlockSpec((tm, LANES), lambda b, i: (i if rope else 0, 0))
    return pl.pallas_call(
        functools.partial(_inproj_kernel, rope=rope),
        grid=(B, S // tm),
        in_specs=[pl.BlockSpec((None, tm, D), lambda b, i: (b, i, 0)),
                  pl.BlockSpec((None, None, 1, 3 * D), lambda b, i: (l, mod_row(b), 0, 0)),
                  layer((1, D)),
                  layer(w_in.shape[1:]),
                  layer((8, BRANCH_W)),
                  const(bd64.shape),
                  const(bd32.shape),
                  rope_spec, rope_spec, rope_spec],
        out_specs=[pl.BlockSpec((None, tm, D), lambda b, i: (b, i, 0)),
                   pl.BlockSpec((None, tm, Z_WIDTH), lambda b, i: (b, i, 0))],
        out_shape=[jax.ShapeDtypeStruct((B, S, D), BF16),
                   jax.ShapeDtypeStruct((B, S, Z_WIDTH), BF16)],
        compiler_params=_cparams(2),
        name="inproj",
    )(h, mod, norm_g.reshape(norm_g.shape[0], 1, D), w_in, gains, bd64, bd32, cos, sina, sinb)


def _branch_a_kernel(au_ref, av_ref, az_ref, g_ref, ws_ref, bs_ref, o_ref):
    tm = au_ref.shape[0]
    v = _gelu(av_ref[...].astype(F32))
    ms = jnp.mean(v * v, axis=-1, keepdims=True)
    vb = (v * lax.rsqrt(ms + EPS) * g_ref[...]).astype(BF16)
    gate = _gelu(au_ref[...].astype(F32)) * _silu(az_ref[...].astype(F32))
    for c in range(tm // CHUNK):
        rows = slice(c * CHUNK, (c + 1) * CHUNK)
        for g in range(A_GROUPS):
            cols = slice(g * LANES, (g + 1) * LANES)
            mixed = jnp.dot(ws_ref[g], vb[rows, cols], preferred_element_type=F32) + bs_ref[:, g:g + 1]
            o_ref[rows, cols] = (gate[rows, cols] * mixed).astype(BF16)


def _branch_a(z, g_v, w_s, b_sT, tm):
    B, S, _ = z.shape
    W = BRANCH_W
    col = lambda cb: pl.BlockSpec((None, tm, W), lambda b, i: (b, i, cb))
    return pl.pallas_call(
        _branch_a_kernel,
        grid=(B, S // tm),
        in_specs=[col(COL_AU), col(COL_AV), col(COL_AZ),
                  pl.BlockSpec((1, W), lambda b, i: (0, 0)),
                  pl.BlockSpec((A_GROUPS, CHUNK, CHUNK), lambda b, i: (0, 0, 0)),
                  pl.BlockSpec((CHUNK, A_GROUPS), lambda b, i: (0, 0))],
        out_specs=pl.BlockSpec((None, tm, W), lambda b, i: (b, i, 0)),
        out_shape=jax.ShapeDtypeStruct((B, S, W), BF16),
        compiler_params=_cparams(2),
        name="branch_a",
    )(z, z, z, g_v, w_s, b_sT)


def _dftw_kernel(c_ref, s_ref, w_ref, cw_ref, sw_ref):
    for g in range(C_GROUPS):
        w = w_ref[g]
        cw_ref[g] = jnp.dot(c_ref[...], w, preferred_element_type=F32,
                            precision=lax.Precision.HIGHEST).astype(BF16)
        sw_ref[g] = jnp.dot(s_ref[...], w, preferred_element_type=F32,
                            precision=lax.Precision.HIGHEST).astype(BF16)


def _dft_weights(c128, s128, w_f):
    shp = jax.ShapeDtypeStruct((C_GROUPS, C_GW, C_GW), BF16)
    return pl.pallas_call(_dftw_kernel, out_shape=[shp, shp], name="dft_weights")(c128, s128, w_f)


def _fourier_pq_kernel(xa_ref, xm_ref, xn_ref, xh_ref, jm_ref, cw_ref, sw_ref, pq_ref, aux_ref):
    i = pl.program_id(1)
    tm = xa_ref.shape[0]
    row = lax.broadcasted_iota(jnp.int32, (tm, 1), 0)
    first = jnp.where(i > 0, xn_ref[0:1, :].astype(F32), 0.0)
    xb = jnp.where(row == 0, first, jnp.dot(jm_ref[...], xm_ref[...], preferred_element_type=F32))
    xa = xa_ref[...].astype(F32)
    xe = (xa + xb).astype(BF16)
    xo = (xa - xb).astype(BF16)
    for g in range(C_GROUPS):
        cols = slice(g * LANES, (g + 1) * LANES)
        pq_ref[0, :, cols] = jnp.dot(xe[:, cols], cw_ref[g], preferred_element_type=F32).astype(BF16)
        pq_ref[1, :, cols] = jnp.dot(xo[:, cols], sw_ref[g], preferred_element_type=F32).astype(BF16)

    @pl.when(i == 0)
    def _():
        for g in range(C_GROUPS):
            cols = slice(g * LANES, (g + 1) * LANES)
            aux_ref[:, cols] = jnp.dot(xh_ref[:, cols], cw_ref[g], preferred_element_type=F32)


def _fourier_pq(z, jm, cw, sw, tm):
    B, L, _ = z.shape
    H = L // 2
    W = BRANCH_W
    nt = H // tm
    wspec = pl.BlockSpec((C_GROUPS, C_GW, C_GW), lambda b, i: (0, 0, 0))
    return pl.pallas_call(
        _fourier_pq_kernel,
        grid=(B, nt),
        in_specs=[pl.BlockSpec((None, tm, W), lambda b, i: (b, i, COL_FIN)),
                  pl.BlockSpec((None, tm, W), lambda b, i: (b, 2 * nt - 1 - i, COL_FIN)),
                  pl.BlockSpec((None, 8, W),
                               lambda b, i: (b, jnp.minimum((L - i * tm) // 8, L // 8 - 1), COL_FIN)),
                  pl.BlockSpec((None, 8, W), lambda b, i: (b, H // 8, COL_FIN)),
                  pl.BlockSpec((tm, tm), lambda b, i: (0, 0)),
                  wspec, wspec],
        out_specs=[pl.BlockSpec((None, 2, tm, W), lambda b, i: (b, 0, i, 0)),
                   pl.BlockSpec((None, 8, W), lambda b, i: (b, 0, 0))],
        out_shape=[jax.ShapeDtypeStruct((B, 2, H, W), BF16),
                   jax.ShapeDtypeStruct((B, 8, W), F32)],
        compiler_params=_cparams(2),
        name="fourier_pq",
    )(z, z, z, z, jm, cw, sw)


def _fourier_mix_kernel(dft_ref, dftx_ref, pq_ref, aux_ref, jm_ref, fza_ref, fzb_ref, u_ref, *, scale, half):
    tm = dft_ref.shape[0]
    pe = pq_ref[0:half, :]
    qo = pq_ref[half:, :]
    ev = jnp.dot(dft_ref[:, 0:half], pe, preferred_element_type=F32)
    od = jnp.dot(dft_ref[:, half:], qo, preferred_element_type=F32)
    row = lax.broadcasted_iota(jnp.int32, (tm, 1), 0)
    nyq = (1 - 2 * (row & 1)).astype(F32) * aux_ref[0:1, :]
    y1 = ev + od + nyq
    y2 = ev - od + nyq
    evx = jnp.dot(dftx_ref[:, 0:half], pe, preferred_element_type=F32)
    odx = jnp.dot(dftx_ref[:, half:], qo, preferred_element_type=F32)
    y2x = evx[0:1, :] - odx[0:1, :] + aux_ref[0:1, :]
    hi = y2.astype(BF16)
    lo = (y2 - hi.astype(F32)).astype(BF16)
    y2r = (jnp.dot(jm_ref[...], hi, preferred_element_type=F32)
           + jnp.dot(jm_ref[...], lo, preferred_element_type=F32))
    y2r = jnp.where(row == 0, y2x, y2r)
    u_ref[0] = (y1 * scale * _silu(fza_ref[...].astype(F32))).astype(BF16)
    u_ref[1] = (y2r * scale * _silu(fzb_ref[...].astype(F32))).astype(BF16)


def _fourier_mix(dft, pq, aux, jm, z, tm):
    B, L, _ = z.shape
    H = L // 2
    W = BRANCH_W
    nt = H // tm
    scale = 1.0 / math.sqrt(L * C_GW)
    return pl.pallas_call(
        functools.partial(_fourier_mix_kernel, scale=scale, half=H),
        grid=(B, nt),
        in_specs=[pl.BlockSpec((tm, L), lambda b, i: (i, 0)),
                  pl.BlockSpec((8, L), lambda b, i: ((i + 1) * (tm // 8), 0)),
                  pl.BlockSpec((None, L, W), lambda b, i: (b, 0, 0)),
                  pl.BlockSpec((None, 8, W), lambda b, i: (b, 0, 0)),
                  pl.BlockSpec((tm, tm), lambda b, i: (0, 0)),
                  pl.BlockSpec((None, tm, W), lambda b, i: (b, i, COL_FZ)),
                  pl.BlockSpec((None, tm, W), lambda b, i: (b, 2 * nt - 1 - i, COL_FZ))],
        out_specs=pl.BlockSpec((None, 2, tm, W), lambda b, i: (b, 0, i, 0)),
        out_shape=jax.ShapeDtypeStruct((B, 2, H, W), BF16),
        compiler_params=_cparams(2),
        name="fourier_mix",
    )(dft, dft, pq, aux, jm, z, z)


def _fourier_branch(z, cw, sw, dft, jm, tm):
    B, L, _ = z.shape
    pq, aux = _fourier_pq(z, jm, cw, sw, tm)
    return _fourier_mix(dft, pq.reshape(B, L, BRANCH_W), aux, jm, z, tm)


def _lane_masks(heads, hd):
    lane = lax.broadcasted_iota(jnp.int32, (1, LANES), 1)
    return [((lane >= g * hd) & (lane < (g + 1) * hd)) for g in heads]


def _attend(q, segs, sink_col, heads, hd):
    tq = q.shape[0]
    masks = _lane_masks(heads, hd)
    zero = jnp.zeros_like(q)
    qx = jnp.concatenate([jnp.where(m, q, zero) for m in masks], axis=0)
    scores = []
    for k, _, bias in segs:
        s = lax.dot_general(qx, k, (((1,), (1,)), ((), ())), preferred_element_type=F32)
        if bias is not None:
            s = s + bias
        scores.append(s)
    m = scores[0].max(axis=-1, keepdims=True)
    for s in scores[1:]:
        m = jnp.maximum(m, s.max(axis=-1, keepdims=True))
    if sink_col is not None:
        m = jnp.maximum(m, sink_col)
    den = None
    acc = None
    for s, (_, v, _) in zip(scores, segs):
        p = jnp.exp2(s - m)
        d = p.sum(axis=-1, keepdims=True)
        o = jnp.dot(p.astype(BF16), v, preferred_element_type=F32)
        den = d if den is None else den + d
        acc = o if acc is None else acc + o
    if sink_col is not None:
        den = den + jnp.exp2(sink_col - m)
    acc = acc / den
    out = None
    for g, msk in enumerate(masks):
        part = jnp.where(msk, acc[g * tq:(g + 1) * tq], 0.0)
        out = part if out is None else out + part
    return out


def _attend_blocks(q_list, local_list, ctx_kv, sink_col, heads, hd):
    tq = q_list[0].shape[0]
    masks = _lane_masks(heads, hd)
    rows = len(masks) * tq
    nt = (((1,), (1,)), ((), ()))
    qx_list = []
    for q in q_list:
        zero = jnp.zeros_like(q)
        qx_list.append(jnp.concatenate([jnp.where(m, q, zero) for m in masks], axis=0))
    kc, vc = ctx_kv
    s_c = lax.dot_general(jnp.concatenate(qx_list, axis=0), kc, nt, preferred_element_type=F32)
    p_c_list, den_list, o_list = [], [], []
    for j, (qx, (k, v, bias)) in enumerate(zip(qx_list, local_list)):
        s_cj = s_c[j * rows:(j + 1) * rows]
        s = lax.dot_general(qx, k, nt, preferred_element_type=F32) + bias
        m = jnp.maximum(s.max(axis=-1, keepdims=True), s_cj.max(axis=-1, keepdims=True))
        if sink_col is not None:
            m = jnp.maximum(m, sink_col)
        p = jnp.exp2(s - m)
        p_c = jnp.exp2(s_cj - m)
        den = p.sum(axis=-1, keepdims=True) + p_c.sum(axis=-1, keepdims=True)
        if sink_col is not None:
            den = den + jnp.exp2(sink_col - m)
        p_c_list.append(p_c.astype(BF16))
        den_list.append(den)
        o_list.append(jnp.dot(p.astype(BF16), v, preferred_element_type=F32))
    o_c = jnp.dot(jnp.concatenate(p_c_list, axis=0), vc, preferred_element_type=F32)
    outs = []
    for j, (o, den) in enumerate(zip(o_list, den_list)):
        acc = (o + o_c[j * rows:(j + 1) * rows]) / den
        out = None
        for g, msk in enumerate(masks):
            part = jnp.where(msk, acc[g * tq:(g + 1) * tq], 0.0)
            out = part if out is None else out + part
        outs.append(out)
    return outs


def _branch_b_kernel(sink_ref, q_ref, kv_ref, ckv_ref, bz_ref, o_ref, *, seq, qt):
    nl = qt + 2 * B_WIN
    half = BRANCH_W // 2
    n_blk = q_ref.shape[0] // qt
    starts, biases = [], []
    for blk in range(n_blk):
        i = pl.program_id(1) * n_blk + blk
        start = pl.multiple_of(jnp.clip(i * qt - B_WIN, 0, seq - nl), B_BLK)
        qpos = i * qt + lax.broadcasted_iota(jnp.int32, (qt, nl), 0)
        kpos = start + lax.broadcasted_iota(jnp.int32, (qt, nl), 1)
        bias1 = jnp.where(jnp.abs(kpos - qpos) <= B_WIN, 0.0, NEG).astype(F32)
        starts.append(start)
        biases.append(jnp.concatenate([bias1, bias1], axis=0))
    for grp in range(BRANCH_W // LANES):
        kvh = grp // 2
        kc = slice(kvh * LANES, (kvh + 1) * LANES)
        vc = slice(half + kvh * LANES, half + (kvh + 1) * LANES)
        cols = slice(grp * LANES, (grp + 1) * LANES)
        sink_col = jnp.concatenate(
            [jnp.full((qt, 1), sink_ref[2 * grp + g] * LOG2E, F32) for g in range(2)], axis=0)
        q_list = [q_ref[blk * qt:(blk + 1) * qt, cols] for blk in range(n_blk)]
        local = [(kv_ref[pl.ds(s, nl), kc], kv_ref[pl.ds(s, nl), vc], b) for s, b in zip(starts, biases)]
        outs = _attend_blocks(q_list, local, (ckv_ref[:, kc], ckv_ref[:, vc]), sink_col, range(2), B_HD)
        for blk, o in enumerate(outs):
            rows = slice(blk * qt, (blk + 1) * qt)
            o_ref[rows, cols] = (o * _silu(bz_ref[rows, cols].astype(F32))).astype(BF16)


def _branch_b(z, zc, sink, tq, qt):
    B, S, _ = z.shape
    C = zc.shape[1]
    W = BRANCH_W
    assert tq % qt == 0 and qt % B_BLK == 0 and S >= qt + 2 * B_WIN
    return pl.pallas_call(
        functools.partial(_branch_b_kernel, seq=S, qt=qt),
        grid=(B, S // tq),
        in_specs=[pl.BlockSpec(memory_space=pltpu.SMEM),
                  pl.BlockSpec((None, tq, W), lambda b, i: (b, i, COL_BQ)),
                  pl.BlockSpec((None, S, W), lambda b, i: (b, 0, COL_BKV)),
                  pl.BlockSpec((None, C, W), lambda b, i: (b, 0, COL_BKV)),
                  pl.BlockSpec((None, tq, W), lambda b, i: (b, i, COL_BZ))],
        out_specs=pl.BlockSpec((None, tq, W), lambda b, i: (b, i, 0)),
        out_shape=jax.ShapeDtypeStruct((B, S, W), BF16),
        compiler_params=_cparams(2),
        name="branch_b",
    )(sink, z, z, zc, z)


def _ctx_b_kernel(sink_ref, q_ref, kv_ref, bz_ref, o_ref):
    tq = q_ref.shape[0]
    half = BRANCH_W // 2
    for grp in range(BRANCH_W // LANES):
        kvh = grp // 2
        kc = slice(kvh * LANES, (kvh + 1) * LANES)
        vc = slice(half + kvh * LANES, half + (kvh + 1) * LANES)
        cols = slice(grp * LANES, (grp + 1) * LANES)
        sink_col = jnp.concatenate(
            [jnp.full((tq, 1), sink_ref[2 * grp + g] * LOG2E, F32) for g in range(2)], axis=0)
        o = _attend(q_ref[:, cols], [(kv_ref[:, kc], kv_ref[:, vc], None)], sink_col, range(2), B_HD)
        o_ref[:, cols] = (o * _silu(bz_ref[:, cols].astype(F32))).astype(BF16)


def _ctx_b(zc, sink):
    B, C, _ = zc.shape
    W = BRANCH_W
    col = lambda cb: pl.BlockSpec((None, C, W), lambda b: (b, 0, cb))
    return pl.pallas_call(
        _ctx_b_kernel,
        grid=(B,),
        in_specs=[pl.BlockSpec(memory_space=pltpu.SMEM), col(COL_BQ), col(COL_BKV), col(COL_BZ)],
        out_specs=pl.BlockSpec((None, C, W), lambda b: (b, 0, 0)),
        out_shape=jax.ShapeDtypeStruct((B, C, W), BF16),
        compiler_params=_cparams(1),
        name="ctx_b",
    )(sink, zc, zc, zc)


def _na_row_start(r, rows):
    return jnp.clip(r - NA_ROWS // 2, 0, rows - NA_ROWS)


def _branch_d_kernel(q_ref, k_ref, v_ref, ck_ref, cv_ref, dz_ref, *rest, rows):
    bias_refs, o_ref = rest[:-1], rest[-1]
    nk = NA_ROWS * GRID_W
    hpg = LANES // D_HD
    n_rows = len(bias_refs)
    starts = [pl.multiple_of(_na_row_start(pl.program_id(1) * n_rows + rr, rows) * GRID_W, GRID_W)
              for rr in range(n_rows)]
    for grp in range(BRANCH_W // LANES):
        cols = slice(grp * LANES, (grp + 1) * LANES)
        q_list = [q_ref[rr * GRID_W:(rr + 1) * GRID_W, cols] for rr in range(n_rows)]
        local = [(k_ref[pl.ds(s, nk), cols], v_ref[pl.ds(s, nk), cols],
                  b[grp * hpg:(grp + 1) * hpg].reshape(hpg * GRID_W, nk)) for s, b in zip(starts, bias_refs)]
        outs = _attend_blocks(q_list, local, (ck_ref[:, cols], cv_ref[:, cols]), None, range(hpg), D_HD)
        for rr, o in enumerate(outs):
            qrows = slice(rr * GRID_W, (rr + 1) * GRID_W)
            o_ref[qrows, cols] = (o * _silu(dz_ref[qrows, cols].astype(F32))).astype(BF16)


def _branch_d(z, zc, bias_tabs, rps, l):
    B, S, _ = z.shape
    C = zc.shape[1]
    W = BRANCH_W
    rows = S // GRID_W
    nk = NA_ROWS * GRID_W
    tile = lambda cb: pl.BlockSpec((None, rps * GRID_W, W), lambda b, i: (b, i, cb))
    full = lambda cb: pl.BlockSpec((None, S, W), lambda b, i: (b, 0, cb))
    ctx = lambda cb: pl.BlockSpec((None, C, W), lambda b, i: (b, 0, cb))

    def bias_spec(rr):
        def index(b, i):
            r = i * rps + rr
            return (l, r - _na_row_start(r, rows), 0, 0, 0)
        return pl.BlockSpec((None, None, D_H, GRID_W, nk), index)

    return pl.pallas_call(
        functools.partial(_branch_d_kernel, rows=rows),
        grid=(B, rows // rps),
        in_specs=[tile(COL_DQ), full(COL_DK), full(COL_DV), ctx(COL_DK), ctx(COL_DV), tile(COL_DZ)]
                 + [bias_spec(rr) for rr in range(rps)],
        out_specs=pl.BlockSpec((None, rps * GRID_W, W), lambda b, i: (b, i, 0)),
        out_shape=jax.ShapeDtypeStruct((B, S, W), BF16),
        compiler_params=_cparams(2),
        name="branch_d",
    )(z, z, z, zc, zc, z, *([bias_tabs] * rps))


def _ctx_d_kernel(q_ref, k_ref, v_ref, dz_ref, o_ref):
    hpg = LANES // D_HD
    for grp in range(BRANCH_W // LANES):
        cols = slice(grp * LANES, (grp + 1) * LANES)
        o = _attend(q_ref[:, cols], [(k_ref[:, cols], v_ref[:, cols], None)], None, range(hpg), D_HD)
        o_ref[:, cols] = (o * _silu(dz_ref[:, cols].astype(F32))).astype(BF16)


def _ctx_d(zc):
    B, C, _ = zc.shape
    W = BRANCH_W
    col = lambda cb: pl.BlockSpec((None, C, W), lambda b: (b, 0, cb))
    return pl.pallas_call(
        _ctx_d_kernel,
        grid=(B,),
        in_specs=[col(COL_DQ), col(COL_DK), col(COL_DV), col(COL_DZ)],
        out_specs=pl.BlockSpec((None, C, W), lambda b: (b, 0, 0)),
        out_shape=jax.ShapeDtypeStruct((B, C, W), BF16),
        compiler_params=_cparams(1),
        name="ctx_d",
    )(zc, zc, zc, zc)


def _gate_mix_kernel(n_ref, ua_ref, ub_ref, uc_ref, ud_ref, wg_ref, wb_ref, o_ref):
    n = n_ref[...]
    mix = None
    for b, u_ref in enumerate((ua_ref, ub_ref, uc_ref, ud_ref)):
        gate = _sigmoid(jnp.dot(n, wg_ref[b], preferred_element_type=F32))
        t = gate * jnp.dot(u_ref[...], wb_ref[b], preferred_element_type=F32)
        mix = t if mix is None else mix + t
    o_ref[...] = mix.astype(BF16)


def _gate_mix(n, us, wg, wb, tm, cn, l):
    B, S, D = n.shape
    W = BRANCH_W
    row = lambda width: pl.BlockSpec((None, tm, width), lambda j, b, i: (b, i, 0))
    uc_spec = row(W)
    if us[2].ndim == 4:
        nt = S // (2 * tm)
        uc_spec = pl.BlockSpec((None, None, tm, W),
                               lambda j, b, i: (b, i // nt, jnp.where(i < nt, i, 2 * nt - 1 - i), 0))
    return pl.pallas_call(
        _gate_mix_kernel,
        grid=(D // cn, B, S // tm),
        in_specs=[row(D), row(W), row(W), uc_spec, row(W),
                  pl.BlockSpec((None, N_BRANCH, D, cn), lambda j, b, i: (l, 0, 0, j)),
                  pl.BlockSpec((None, N_BRANCH, W, cn), lambda j, b, i: (l, 0, 0, j))],
        out_specs=pl.BlockSpec((None, tm, cn), lambda j, b, i: (b, i, j)),
        out_shape=jax.ShapeDtypeStruct((B, S, D), BF16),
        compiler_params=_cparams(3),
        name="gate_mix",
    )(n, *us, wg, wb)


def _out_proj_kernel(h_ref, mod_ref, mix_ref, wo_ref, o_ref):
    D = h_ref.shape[-1]
    y = jnp.dot(mix_ref[...], wo_ref[...], preferred_element_type=F32)
    o_ref[...] = h_ref[...] + mod_ref[:, 2 * D:3 * D] * y


def _out_proj(h, mod, mod_row, mix, wo, tm, l):
    B, S, D = h.shape
    row = lambda: pl.BlockSpec((None, tm, D), lambda b, i: (b, i, 0))
    return pl.pallas_call(
        _out_proj_kernel,
        grid=(B, S // tm),
        in_specs=[row(),
                  pl.BlockSpec((None, None, 1, 3 * D), lambda b, i: (l, mod_row(b), 0, 0)),
                  row(),
                  pl.BlockSpec((None, D, D), lambda b, i: (l, 0, 0), pipeline_mode=pl.Buffered(1))],
        out_specs=row(),
        out_shape=jax.ShapeDtypeStruct((B, S, D), F32),
        compiler_params=_cparams(2),
        name="out_proj",
    )(h, mod, mix, wo)


def _merge(h, mod, mod_row, n, us, wg, wb, wo, tm, cn, l):
    return _out_proj(h, mod, mod_row, _gate_mix(n, us, wg, wb, tm, cn, l), wo, tm, l)


def _rope_tables(S):
    t = np.arange(S)
    pos = np.stack([t // GRID_W, t % GRID_W], axis=1).astype(np.float32)
    lane = np.arange(B_HD)
    which = lane // (B_HD // 2)
    fi = lane % (B_HD // 4)
    inv = (ROPE_BASE ** (-(fi.astype(np.float32)) / (B_HD // 4))).astype(np.float32)
    ang = pos[:, which] * inv[None, :]
    cos, sin = np.cos(ang), np.sin(ang)
    lower = (lane % (B_HD // 2)) < (B_HD // 4)
    sina = np.where(lower[None, :], -sin, 0.0)
    sinb = np.where(lower[None, :], 0.0, sin)
    rep = lambda a: jnp.asarray(np.tile(a.astype(np.float32), (1, LANES // B_HD)))
    return rep(cos), rep(sina), rep(sinb)


def _block_diag_mean(width, hd):
    i = np.arange(width)
    return jnp.asarray(((i[:, None] // hd) == (i[None, :] // hd)).astype(np.float32) / hd, dtype=BF16)


def _dft_cos_sin(n):
    k = jnp.arange(n, dtype=jnp.int32)
    m = (k[:, None] * k[None, :]) % n
    ang = m.astype(F32) * (2.0 * math.pi / n)
    return jnp.cos(ang), jnp.sin(ang)


def _dft_half_matrix(n):
    k = jnp.arange(n // 2 + 8, dtype=jnp.int32)
    m = jnp.arange(n // 2, dtype=jnp.int32)
    ang = ((k[:, None] * m[None, :]) % n).astype(F32) * (2.0 * math.pi / n)
    return jnp.concatenate([jnp.cos(ang), -jnp.sin(ang)], axis=1).astype(BF16)


def _rev_shift_matrix(tm):
    r = np.arange(tm)
    return jnp.asarray(((r[:, None] >= 1) & (r[None, :] == tm - r[:, None])).astype(np.float32), dtype=BF16)


def _bias_expand_kernel(rpb_ref, e_ref, o_ref):
    o_ref[...] = LOG2E * jnp.dot(rpb_ref[...], e_ref[...], preferred_element_type=F32,
                                 precision=lax.Precision.HIGHEST)


def _na_bias_tables(rpb):
    L, H, n_dr, n_dc = rpb.shape
    c = np.arange(GRID_W)
    c0 = np.clip(c - NA_COLS // 2, 0, GRID_W - NA_COLS)
    kc = np.arange(GRID_W)
    inwin = (kc[None, :] >= c0[:, None]) & (kc[None, :] < c0[:, None] + NA_COLS)
    dc = kc[None, :] - c[:, None] + (NA_COLS - 1)
    n_dc_pad = 32
    onehot = (np.arange(n_dc_pad)[:, None, None] == dc[None]) & inwin[None]
    e2d = jnp.asarray(onehot.reshape(n_dc_pad, GRID_W * GRID_W).astype(np.float32))
    rpb2d = jnp.pad(rpb.reshape(L, H * n_dr, n_dc), ((0, 0), (0, 0), (0, n_dc_pad - n_dc)))
    t = pl.pallas_call(
        _bias_expand_kernel,
        grid=(L,),
        in_specs=[pl.BlockSpec((None, H * n_dr, n_dc_pad), lambda l: (l, 0, 0)),
                  pl.BlockSpec((n_dc_pad, GRID_W * GRID_W), lambda l: (0, 0))],
        out_specs=pl.BlockSpec((None, H * n_dr, GRID_W * GRID_W), lambda l: (l, 0, 0)),
        out_shape=jax.ShapeDtypeStruct((L, H * n_dr, GRID_W * GRID_W), F32),
        compiler_params=_cparams(1),
        name="bias_expand",
    )(rpb2d, e2d)
    t = t.reshape(L, H, n_dr, GRID_W, GRID_W)
    t = jnp.where(jnp.asarray(inwin)[None, None, None], t, NEG)
    t = t.transpose(0, 1, 3, 2, 4).reshape(L, H, GRID_W, n_dr * GRID_W)
    nk = NA_ROWS * GRID_W
    cases = [t[..., (NA_ROWS - 1 - d) * GRID_W:(NA_ROWS - 1 - d) * GRID_W + nk] for d in range(NA_ROWS)]
    return jnp.stack(cases, axis=1)


def kernel(x, c, ctx, c_ctx, norm_g, w_ada, b_ada, w_in, a_norm_g, a_w_s, a_b_s, b_q_g, b_k_g, b_sink,
           c_w_f, d_q_g, d_k_g, d_rpb, w_gate, w_branch, w_out):
    B, S, D = x.shape
    C = ctx.shape[1]
    L = norm_g.shape[0]
    assert S % 512 == 0 and S // GRID_W >= NA_ROWS and S >= 3 * B_BLK and C % CHUNK == 0
    tm, tmc = 512, C
    tm_ctx = math.gcd(B * C, 512)
    cn = 512
    tm_in = 512
    tq_b = 8 * B_BLK
    qt_b = B_BLK
    rps_d = 4
    W = BRANCH_W

    n_rows = -(-(B + 1) // 8) * 8
    cc = jnp.concatenate([c, c_ctx[None], jnp.zeros((n_rows - B - 1, D), F32)], axis=0)
    mod = _modulation(cc, w_ada, b_ada).reshape(L, n_rows, 1, 3 * D)
    lat_row = lambda b: b
    ctx_row = lambda b: B

    w_in_b = w_in.astype(BF16)
    w_gate_b, w_branch_b, w_out_b = w_gate.astype(BF16), w_branch.astype(BF16), w_out.astype(BF16)
    a_ws_b = a_w_s.astype(BF16)
    gains = jnp.stack([jnp.tile(b_q_g, (1, W // B_HD)),
                       jnp.tile(b_k_g, (1, W // B_HD)),
                       jnp.tile(d_q_g, (1, W // D_HD)),
                       jnp.tile(d_k_g, (1, W // D_HD))] + [jnp.zeros((L, W), F32)] * 4, axis=1)
    bd64, bd32 = _block_diag_mean(W // 2, B_HD), _block_diag_mean(W // 2, D_HD)
    ropes = _rope_tables(S)
    c128, s128 = _dft_cos_sin(C_GW)
    tm_f, tm_fc = min(512, S // 2), min(512, C // 2)
    dft_lat, dft_ctx = _dft_half_matrix(S), _dft_half_matrix(C)
    jm_lat, jm_ctx = _rev_shift_matrix(tm_f), _rev_shift_matrix(tm_fc)
    assert tm_f == tm and tm_fc == C // 2
    bias_tabs = _na_bias_tables(d_rpb)

    h, hc = x, ctx.reshape(1, B * C, D)
    for l in range(L):
        last = l == L - 1
        n_lat, z = _inproj(h, mod, lat_row, norm_g, w_in_b, gains, bd64, bd32, ropes, tm_in, True, l)
        n_ctx, zc = _inproj(hc, mod, ctx_row, norm_g, w_in_b, gains, bd64, bd32, ropes, tm_ctx, False, l)
        zc = zc.reshape(B, C, Z_WIDTH)
        cw, sw = _dft_weights(c128, s128, c_w_f[l])
        g_v = a_norm_g[l][None]
        b_sT = a_b_s[l].T

        u_lat = [
            _branch_a(z, g_v, a_ws_b[l], b_sT, tm),
            _branch_b(z, zc, b_sink[l], tq_b, qt_b),
            _fourier_branch(z, cw, sw, dft_lat, jm_lat, tm_f),
            _branch_d(z, zc, bias_tabs, rps_d, l),
        ]
        h = _merge(h, mod, lat_row, n_lat, u_lat, w_gate_b, w_branch_b, w_out_b, tm, cn, l)
        if not last:
            u_ctx = [
                _branch_a(zc, g_v, a_ws_b[l], b_sT, tmc),
                _ctx_b(zc, b_sink[l]),
                _fourier_branch(zc, cw, sw, dft_ctx, jm_ctx, tm_fc).reshape(B, C, W),
                _ctx_d(zc),
            ]
            u_ctx = [u.reshape(1, B * C, W) for u in u_ctx]
            hc = _merge(hc, mod, ctx_row, n_ctx, u_ctx, w_gate_b, w_branch_b, w_out_b, tm_ctx, cn, l)
    return h
```

```python
import functools
import math

import jax
import jax.numpy as jnp
import numpy as np
from jax import lax
from jax.experimental import pallas as pl
from jax.experimental.pallas import tpu as pltpu

F32 = jnp.float32
BF16 = jnp.bfloat16

LANES = 128
V7X_VMEM_LIMIT_BYTES = 56 * 1024 * 1024

GRID_W = 64
N_BRANCH = 4
BRANCH_W = 512
CHUNK = 128
A_GROUPS = 4
B_HD = 64
B_QH = 8
B_KVH = 2
B_WIN = 128
B_BLK = 128
C_GROUPS = 4
C_GW = 128
D_HD = 32
D_H = 16
NA_ROWS = 8
NA_COLS = 16
ROPE_BASE = 10000.0
EPS = 1e-6
NEG = -1e30
LOG2E = math.log2(math.e)

(COL_BQ, COL_BKV, COL_BZ, COL_FIN, COL_FZ, COL_DQ, COL_DK, COL_DV, COL_DZ) = range(9)
Z_WIDTH = 9 * BRANCH_W
W_IN_SRC = {COL_BQ: 1536, COL_BZ: 2304, COL_FIN: 2816, COL_FZ: 3328,
            COL_DQ: 3840, COL_DK: 4352, COL_DV: 4864, COL_DZ: 5376}
W_IN_BK = 2048
W_IN_AU, W_IN_AV, W_IN_AZ = 0, 512, 1024


def _cparams(n_grid, flags=None):
    return pltpu.CompilerParams(
        dimension_semantics=("arbitrary",) * n_grid,
        vmem_limit_bytes=V7X_VMEM_LIMIT_BYTES,
        flags=flags)


def _sigmoid(x):
    return 1.0 / (1.0 + jnp.exp(-x))


def _silu(x):
    return x * _sigmoid(x)


def _gelu(x):
    return 0.5 * x * (1.0 + jnp.tanh(math.sqrt(2.0 / math.pi) * (x + 0.044715 * (x * x * x))))


def _mod_kernel(c_ref, w_ref, b_ref, o_ref):
    c = c_ref[...]
    a = _silu(c).astype(BF16)
    o_ref[...] = jnp.dot(a, w_ref[...].astype(BF16), preferred_element_type=F32) + b_ref[...]


def _modulation(cc, w_ada, b_ada):
    L, D, N = w_ada.shape
    R = cc.shape[0]
    tn = 512
    return pl.pallas_call(
        _mod_kernel,
        grid=(L, N // tn),
        in_specs=[pl.BlockSpec((R, D), lambda l, j: (0, 0)),
                  pl.BlockSpec((None, D, tn), lambda l, j: (l, 0, j)),
                  pl.BlockSpec((None, 1, tn), lambda l, j: (l, 0, j))],
        out_specs=pl.BlockSpec((None, R, tn), lambda l, j: (l, 0, j)),
        out_shape=jax.ShapeDtypeStruct((L, R, N), F32),
        compiler_params=_cparams(2),
        name="modulation",
    )(cc, w_ada, b_ada.reshape(L, 1, N))


def _head_rms(x, bd_ref, width):
    sq = (x * x).astype(BF16)
    c = min(bd_ref.shape[0], width)
    bd = bd_ref[0:c, 0:c]
    parts = [jnp.dot(sq[:, o:o + c], bd, preferred_element_type=F32) for o in range(0, width, c)]
    return parts[0] if len(parts) == 1 else jnp.concatenate(parts, axis=1)


def _rope(x, cos, sina, sinb):
    w = x.shape[-1]
    reps = w // LANES
    cos = jnp.concatenate([cos] * reps, axis=1)
    sina = jnp.concatenate([sina] * reps, axis=1)
    sinb = jnp.concatenate([sinb] * reps, axis=1)
    up = pltpu.roll(x, w - 16, 1)
    dn = pltpu.roll(x, 16, 1)
    return x * cos + up * sina + dn * sinb


def _ada_norm(x, mod_ref, ng_ref):
    D = x.shape[-1]
    ms = jnp.mean(x * x, axis=-1, keepdims=True)
    y = x * lax.rsqrt(ms + EPS) * ng_ref[...]
    return (y * (1.0 + mod_ref[:, D:2 * D]) + mod_ref[:, 0:D]).astype(BF16)


def _dup_heads(t):
    swapped = pltpu.roll(t, B_HD, 1)
    low = lax.broadcasted_iota(jnp.int32, (1, LANES), 1) < B_HD
    return jnp.concatenate([jnp.where(low, t, swapped), jnp.where(low, swapped, t)], axis=1)


def _chunk_mlp(nb, w_ref, gv_ref, ws_ref, bs_ref, ua_ref):
    W = BRANCH_W
    proj = lambda src: jnp.dot(nb, w_ref[:, src:src + W], preferred_element_type=F32)
    v = _gelu(proj(W_IN_AV))
    ms = jnp.mean(v * v, axis=-1, keepdims=True)
    vb = (v * lax.rsqrt(ms + EPS) * gv_ref[...]).astype(BF16)
    gate = _gelu(proj(W_IN_AU)) * _silu(proj(W_IN_AZ))
    for c in range(nb.shape[0] // CHUNK):
        rows = slice(c * CHUNK, (c + 1) * CHUNK)
        for g in range(A_GROUPS):
            cols = slice(g * LANES, (g + 1) * LANES)
            mixed = jnp.dot(ws_ref[g], vb[rows, cols], preferred_element_type=F32) + bs_ref[:, g:g + 1]
            ua_ref[rows, cols] = (gate[rows, cols] * mixed).astype(BF16)


def _inproj_kernel(h_ref, mod_ref, ng_ref, w_ref, gains_ref, bd64_ref, bd32_ref,
                   cos_ref, sina_ref, sinb_ref, gv_ref, ws_ref, bs_ref, n_ref, z_ref, ua_ref, *, rope):
    W = BRANCH_W
    nb = _ada_norm(h_ref[...], mod_ref, ng_ref)
    n_ref[...] = nb
    _chunk_mlp(nb, w_ref, gv_ref, ws_ref, bs_ref, ua_ref)

    def maybe_rope(t):
        if rope:
            return _rope(t, cos_ref[...], sina_ref[...], sinb_ref[...])
        return t

    def head_norm(t, bd_ref, gain):
        return t * lax.rsqrt(_head_rms(t, bd_ref, t.shape[-1]) + EPS) * gain

    for cb in range(Z_WIDTH // W):
        if cb == COL_BKV:
            kvw = B_KVH * B_HD
            kv = jnp.dot(nb, w_ref[:, W_IN_BK:W_IN_BK + 2 * kvw], preferred_element_type=F32)
            k = maybe_rope(head_norm(kv[:, 0:kvw], bd64_ref, gains_ref[1:2, 0:kvw]))
            acc = jnp.concatenate([_dup_heads(k), _dup_heads(kv[:, kvw:])], axis=1)
        else:
            src = W_IN_SRC[cb]
            acc = jnp.dot(nb, w_ref[:, src:src + W], preferred_element_type=F32)
            if cb == COL_BQ:
                acc = maybe_rope(head_norm(acc, bd64_ref, gains_ref[0:1, :])) * (B_HD ** -0.5 * LOG2E)
            elif cb == COL_DQ:
                acc = head_norm(acc, bd32_ref, gains_ref[2:3, :]) * (D_HD ** -0.5 * LOG2E)
            elif cb == COL_DK:
                acc = head_norm(acc, bd32_ref, gains_ref[3:4, :])
        z_ref[:, cb * W:(cb + 1) * W] = acc.astype(BF16)


def _inproj(h, mod, mod_row, norm_g, w_in, gains, bd64, bd32, ropes, a_params, tm, rope, l):
    g_v, w_s, b_sT = a_params
    B, S, D = h.shape
    cos, sina, sinb = ropes
    layer = lambda shape: pl.BlockSpec((None,) + shape, lambda b, i: (l,) + (0,) * len(shape),
                                       pipeline_mode=pl.Buffered(1))
    const = lambda shape: pl.BlockSpec(shape, lambda b, i: (0,) * len(shape), pipeline_mode=pl.Buffered(1))
    rope_spec = pl.BlockSpec((tm, LANES), lambda b, i: (i if rope else 0, 0))
    return pl.pallas_call(
        functools.partial(_inproj_kernel, rope=rope),
        grid=(B, S // tm),
        in_specs=[pl.BlockSpec((None, tm, D), lambda b, i: (b, i, 0)),
                  pl.BlockSpec((None, None, 1, 3 * D), lambda b, i: (l, mod_row(b), 0, 0)),
                  layer((1, D)),
                  layer(w_in.shape[1:]),
                  layer((8, BRANCH_W)),
                  const(bd64.shape),
                  const(bd32.shape),
                  rope_spec, rope_spec, rope_spec,
                  layer((1, BRANCH_W)),
                  layer((A_GROUPS, CHUNK, CHUNK)),
                  layer((CHUNK, A_GROUPS))],
        out_specs=[pl.BlockSpec((None, tm, D), lambda b, i: (b, i, 0)),
                   pl.BlockSpec((None, tm, Z_WIDTH), lambda b, i: (b, i, 0)),
                   pl.BlockSpec((None, tm, BRANCH_W), lambda b, i: (b, i, 0))],
        out_shape=[jax.ShapeDtypeStruct((B, S, D), BF16),
                   jax.ShapeDtypeStruct((B, S, Z_WIDTH), BF16),
                   jax.ShapeDtypeStruct((B, S, BRANCH_W), BF16)],
        compiler_params=_cparams(2),
        name="inproj",
    )(h, mod, norm_g.reshape(norm_g.shape[0], 1, D), w_in, gains, bd64, bd32, cos, sina, sinb,
      g_v.reshape(g_v.shape[0], 1, BRANCH_W), w_s, b_sT)


def _dftw_kernel(c_ref, s_ref, w_ref, cw_ref, sw_ref):
    for g in range(C_GROUPS):
        w = w_ref[g]
        cw_ref[g] = jnp.dot(c_ref[...], w, preferred_element_type=F32,
                            precision=lax.Precision.HIGHEST).astype(BF16)
        sw_ref[g] = jnp.dot(s_ref[...], w, preferred_element_type=F32,
                            precision=lax.Precision.HIGHEST).astype(BF16)


def _dft_weights(c128, s128, w_f):
    shp = jax.ShapeDtypeStruct((C_GROUPS, C_GW, C_GW), BF16)
    return pl.pallas_call(_dftw_kernel, out_shape=[shp, shp], name="dft_weights")(c128, s128, w_f)


def _fourier_pq_kernel(xa_ref, xm_ref, xn_ref, xh_ref, jm_ref, cw_ref, sw_ref, pq_ref, aux_ref):
    i = pl.program_id(1)
    tm = xa_ref.shape[0]
    row = lax.broadcasted_iota(jnp.int32, (tm, 1), 0)
    first = jnp.where(i > 0, xn_ref[0:1, :].astype(F32), 0.0)
    xb = jnp.where(row == 0, first, jnp.dot(jm_ref[...], xm_ref[...], preferred_element_type=F32))
    xa = xa_ref[...].astype(F32)
    xe = (xa + xb).astype(BF16)
    xo = (xa - xb).astype(BF16)
    for g in range(C_GROUPS):
        cols = slice(g * LANES, (g + 1) * LANES)
        pq_ref[0, :, cols] = jnp.dot(xe[:, cols], cw_ref[g], preferred_element_type=F32).astype(BF16)
        pq_ref[1, :, cols] = jnp.dot(xo[:, cols], sw_ref[g], preferred_element_type=F32).astype(BF16)

    @pl.when(i == 0)
    def _():
        for g in range(C_GROUPS):
            cols = slice(g * LANES, (g + 1) * LANES)
            aux_ref[:, cols] = jnp.dot(xh_ref[:, cols], cw_ref[g], preferred_element_type=F32)


def _fourier_pq(z, jm, cw, sw, tm):
    B, L, _ = z.shape
    H = L // 2
    W = BRANCH_W
    nt = H // tm
    wspec = pl.BlockSpec((C_GROUPS, C_GW, C_GW), lambda b, i: (0, 0, 0))
    return pl.pallas_call(
        _fourier_pq_kernel,
        grid=(B, nt),
        in_specs=[pl.BlockSpec((None, tm, W), lambda b, i: (b, i, COL_FIN)),
                  pl.BlockSpec((None, tm, W), lambda b, i: (b, 2 * nt - 1 - i, COL_FIN)),
                  pl.BlockSpec((None, 8, W),
                               lambda b, i: (b, jnp.minimum((L - i * tm) // 8, L // 8 - 1), COL_FIN)),
                  pl.BlockSpec((None, 8, W), lambda b, i: (b, H // 8, COL_FIN)),
                  pl.BlockSpec((tm, tm), lambda b, i: (0, 0)),
                  wspec, wspec],
        out_specs=[pl.BlockSpec((None, 2, tm, W), lambda b, i: (b, 0, i, 0)),
                   pl.BlockSpec((None, 8, W), lambda b, i: (b, 0, 0))],
        out_shape=[jax.ShapeDtypeStruct((B, 2, H, W), BF16),
                   jax.ShapeDtypeStruct((B, 8, W), F32)],
        compiler_params=_cparams(2),
        name="fourier_pq",
    )(z, z, z, z, jm, cw, sw)


def _fourier_mix_kernel(dft_ref, dftx_ref, pq_ref, aux_ref, jm_ref, fza_ref, fzb_ref, u_ref, *, scale, half):
    tm = dft_ref.shape[0]
    pe = pq_ref[0:half, :]
    qo = pq_ref[half:, :]
    ev = jnp.dot(dft_ref[:, 0:half], pe, preferred_element_type=F32)
    od = jnp.dot(dft_ref[:, half:], qo, preferred_element_type=F32)
    row = lax.broadcasted_iota(jnp.int32, (tm, 1), 0)
    nyq = (1 - 2 * (row & 1)).astype(F32) * aux_ref[0:1, :]
    y1 = ev + od + nyq
    y2 = ev - od + nyq
    evx = jnp.dot(dftx_ref[:, 0:half], pe, preferred_element_type=F32)
    odx = jnp.dot(dftx_ref[:, half:], qo, preferred_element_type=F32)
    y2x = evx[0:1, :] - odx[0:1, :] + aux_ref[0:1, :]
    hi = y2.astype(BF16)
    lo = (y2 - hi.astype(F32)).astype(BF16)
    y2r = (jnp.dot(jm_ref[...], hi, preferred_element_type=F32)
           + jnp.dot(jm_ref[...], lo, preferred_element_type=F32))
    y2r = jnp.where(row == 0, y2x, y2r)
    u_ref[0] = (y1 * scale * _silu(fza_ref[...].astype(F32))).astype(BF16)
    u_ref[1] = (y2r * scale * _silu(fzb_ref[...].astype(F32))).astype(BF16)


def _fourier_mix(dft, pq, aux, jm, z, tm):
    B, L, _ = z.shape
    H = L // 2
    W = BRANCH_W
    nt = H // tm
    scale = 1.0 / math.sqrt(L * C_GW)
    return pl.pallas_call(
        functools.partial(_fourier_mix_kernel, scale=scale, half=H),
        grid=(B, nt),
        in_specs=[pl.BlockSpec((tm, L), lambda b, i: (i, 0)),
                  pl.BlockSpec((8, L), lambda b, i: ((i + 1) * (tm // 8), 0)),
                  pl.BlockSpec((None, L, W), lambda b, i: (b, 0, 0)),
                  pl.BlockSpec((None, 8, W), lambda b, i: (b, 0, 0)),
                  pl.BlockSpec((tm, tm), lambda b, i: (0, 0)),
                  pl.BlockSpec((None, tm, W), lambda b, i: (b, i, COL_FZ)),
                  pl.BlockSpec((None, tm, W), lambda b, i: (b, 2 * nt - 1 - i, COL_FZ))],
        out_specs=pl.BlockSpec((None, 2, tm, W), lambda b, i: (b, 0, i, 0)),
        out_shape=jax.ShapeDtypeStruct((B, 2, H, W), BF16),
        compiler_params=_cparams(2),
        name="fourier_mix",
    )(dft, dft, pq, aux, jm, z, z)


def _fourier_branch(z, cw, sw, dft, jm, tm):
    B, L, _ = z.shape
    pq, aux = _fourier_pq(z, jm, cw, sw, tm)
    return _fourier_mix(dft, pq.reshape(B, L, BRANCH_W), aux, jm, z, tm)


def _lane_masks(heads, hd):
    lane = lax.broadcasted_iota(jnp.int32, (1, LANES), 1)
    return [((lane >= g * hd) & (lane < (g + 1) * hd)) for g in heads]


def _attend(q, segs, sink_col, heads, hd):
    tq = q.shape[0]
    masks = _lane_masks(heads, hd)
    zero = jnp.zeros_like(q)
    qx = jnp.concatenate([jnp.where(m, q, zero) for m in masks], axis=0)
    scores = []
    for k, _, bias in segs:
        s = lax.dot_general(qx, k, (((1,), (1,)), ((), ())), preferred_element_type=F32)
        if bias is not None:
            s = s + bias
        scores.append(s)
    m = scores[0].max(axis=-1, keepdims=True)
    for s in scores[1:]:
        m = jnp.maximum(m, s.max(axis=-1, keepdims=True))
    if sink_col is not None:
        m = jnp.maximum(m, sink_col)
    den = None
    acc = None
    for s, (_, v, _) in zip(scores, segs):
        p = jnp.exp2(s - m)
        d = p.sum(axis=-1, keepdims=True)
        o = jnp.dot(p.astype(BF16), v, preferred_element_type=F32)
        den = d if den is None else den + d
        acc = o if acc is None else acc + o
    if sink_col is not None:
        den = den + jnp.exp2(sink_col - m)
    acc = acc / den
    out = None
    for g, msk in enumerate(masks):
        part = jnp.where(msk, acc[g * tq:(g + 1) * tq], 0.0)
        out = part if out is None else out + part
    return out


def _attend_blocks(q_list, local_list, ctx_kv, sink_col, heads, hd):
    tq = q_list[0].shape[0]
    masks = _lane_masks(heads, hd)
    rows = len(masks) * tq
    nt = (((1,), (1,)), ((), ()))
    qx_list = []
    for q in q_list:
        zero = jnp.zeros_like(q)
        qx_list.append(jnp.concatenate([jnp.where(m, q, zero) for m in masks], axis=0))
    kc, vc = ctx_kv
    s_c = lax.dot_general(jnp.concatenate(qx_list, axis=0), kc, nt, preferred_element_type=F32)
    p_c_list, den_list, o_list = [], [], []
    for j, (qx, (k, v, bias)) in enumerate(zip(qx_list, local_list)):
        s_cj = s_c[j * rows:(j + 1) * rows]
        s = lax.dot_general(qx, k, nt, preferred_element_type=F32) + bias
        m = jnp.maximum(s.max(axis=-1, keepdims=True), s_cj.max(axis=-1, keepdims=True))
        if sink_col is not None:
            m = jnp.maximum(m, sink_col)
        p = jnp.exp2(s - m)
        p_c = jnp.exp2(s_cj - m)
        den = p.sum(axis=-1, keepdims=True) + p_c.sum(axis=-1, keepdims=True)
        if sink_col is not None:
            den = den + jnp.exp2(sink_col - m)
        p_c_list.append(p_c.astype(BF16))
        den_list.append(den)
        o_list.append(jnp.dot(p.astype(BF16), v, preferred_element_type=F32))
    o_c = jnp.dot(jnp.concatenate(p_c_list, axis=0), vc, preferred_element_type=F32)
    outs = []
    for j, (o, den) in enumerate(zip(o_list, den_list)):
        acc = (o + o_c[j * rows:(j + 1) * rows]) / den
        out = None
        for g, msk in enumerate(masks):
            part = jnp.where(msk, acc[g * tq:(g + 1) * tq], 0.0)
            out = part if out is None else out + part
        outs.append(out)
    return outs


def _branch_b_kernel(sink_ref, q_ref, kv_ref, ckv_ref, bz_ref, o_ref, *, seq, qt):
    nl = qt + 2 * B_WIN
    half = BRANCH_W // 2
    n_blk = q_ref.shape[0] // qt
    starts, biases = [], []
    for blk in range(n_blk):
        i = pl.program_id(1) * n_blk + blk
        start = pl.multiple_of(jnp.clip(i * qt - B_WIN, 0, seq - nl), B_BLK)
        qpos = i * qt + lax.broadcasted_iota(jnp.int32, (qt, nl), 0)
        kpos = start + lax.broadcasted_iota(jnp.int32, (qt, nl), 1)
        bias1 = jnp.where(jnp.abs(kpos - qpos) <= B_WIN, 0.0, NEG).astype(F32)
        starts.append(start)
        biases.append(jnp.concatenate([bias1, bias1], axis=0))
    for grp in range(BRANCH_W // LANES):
        kvh = grp // 2
        kc = slice(kvh * LANES, (kvh + 1) * LANES)
        vc = slice(half + kvh * LANES, half + (kvh + 1) * LANES)
        cols = slice(grp * LANES, (grp + 1) * LANES)
        sink_col = jnp.concatenate(
            [jnp.full((qt, 1), sink_ref[2 * grp + g] * LOG2E, F32) for g in range(2)], axis=0)
        q_list = [q_ref[blk * qt:(blk + 1) * qt, cols] for blk in range(n_blk)]
        local = [(kv_ref[pl.ds(s, nl), kc], kv_ref[pl.ds(s, nl), vc], b) for s, b in zip(starts, biases)]
        outs = _attend_blocks(q_list, local, (ckv_ref[:, kc], ckv_ref[:, vc]), sink_col, range(2), B_HD)
        for blk, o in enumerate(outs):
            rows = slice(blk * qt, (blk + 1) * qt)
            o_ref[rows, cols] = (o * _silu(bz_ref[rows, cols].astype(F32))).astype(BF16)


def _branch_b(z, zc, sink, tq, qt):
    B, S, _ = z.shape
    C = zc.shape[1]
    W = BRANCH_W
    assert tq % qt == 0 and qt % B_BLK == 0 and S >= qt + 2 * B_WIN
    return pl.pallas_call(
        functools.partial(_branch_b_kernel, seq=S, qt=qt),
        grid=(B, S // tq),
        in_specs=[pl.BlockSpec(memory_space=pltpu.SMEM),
                  pl.BlockSpec((None, tq, W), lambda b, i: (b, i, COL_BQ)),
                  pl.BlockSpec((None, S, W), lambda b, i: (b, 0, COL_BKV)),
                  pl.BlockSpec((None, C, W), lambda b, i: (b, 0, COL_BKV)),
                  pl.BlockSpec((None, tq, W), lambda b, i: (b, i, COL_BZ))],
        out_specs=pl.BlockSpec((None, tq, W), lambda b, i: (b, i, 0)),
        out_shape=jax.ShapeDtypeStruct((B, S, W), BF16),
        compiler_params=_cparams(2),
        name="branch_b",
    )(sink, z, z, zc, z)


def _ctx_b_kernel(sink_ref, q_ref, kv_ref, bz_ref, o_ref):
    tq = q_ref.shape[0]
    half = BRANCH_W // 2
    for grp in range(BRANCH_W // LANES):
        kvh = grp // 2
        kc = slice(kvh * LANES, (kvh + 1) * LANES)
        vc = slice(half + kvh * LANES, half + (kvh + 1) * LANES)
        cols = slice(grp * LANES, (grp + 1) * LANES)
        sink_col = jnp.concatenate(
            [jnp.full((tq, 1), sink_ref[2 * grp + g] * LOG2E, F32) for g in range(2)], axis=0)
        o = _attend(q_ref[:, cols], [(kv_ref[:, kc], kv_ref[:, vc], None)], sink_col, range(2), B_HD)
        o_ref[:, cols] = (o * _silu(bz_ref[:, cols].astype(F32))).astype(BF16)


def _ctx_b(zc, sink):
    B, C, _ = zc.shape
    W = BRANCH_W
    col = lambda cb: pl.BlockSpec((None, C, W), lambda b: (b, 0, cb))
    return pl.pallas_call(
        _ctx_b_kernel,
        grid=(B,),
        in_specs=[pl.BlockSpec(memory_space=pltpu.SMEM), col(COL_BQ), col(COL_BKV), col(COL_BZ)],
        out_specs=pl.BlockSpec((None, C, W), lambda b: (b, 0, 0)),
        out_shape=jax.ShapeDtypeStruct((B, C, W), BF16),
        compiler_params=_cparams(1),
        name="ctx_b",
    )(sink, zc, zc, zc)


def _na_row_start(r, rows):
    return jnp.clip(r - NA_ROWS // 2, 0, rows - NA_ROWS)


def _branch_d_kernel(q_ref, k_ref, v_ref, ck_ref, cv_ref, dz_ref, *rest, rows):
    bias_refs, o_ref = rest[:-1], rest[-1]
    nk = NA_ROWS * GRID_W
    hpg = LANES // D_HD
    n_rows = len(bias_refs)
    starts = [pl.multiple_of(_na_row_start(pl.program_id(1) * n_rows + rr, rows) * GRID_W, GRID_W)
              for rr in range(n_rows)]
    for grp in range(BRANCH_W // LANES):
        cols = slice(grp * LANES, (grp + 1) * LANES)
        q_list = [q_ref[rr * GRID_W:(rr + 1) * GRID_W, cols] for rr in range(n_rows)]
        local = [(k_ref[pl.ds(s, nk), cols], v_ref[pl.ds(s, nk), cols],
                  b[grp * hpg:(grp + 1) * hpg].reshape(hpg * GRID_W, nk)) for s, b in zip(starts, bias_refs)]
        outs = _attend_blocks(q_list, local, (ck_ref[:, cols], cv_ref[:, cols]), None, range(hpg), D_HD)
        for rr, o in enumerate(outs):
            qrows = slice(rr * GRID_W, (rr + 1) * GRID_W)
            o_ref[qrows, cols] = (o * _silu(dz_ref[qrows, cols].astype(F32))).astype(BF16)


def _branch_d(z, zc, bias_tabs, rps, l):
    B, S, _ = z.shape
    C = zc.shape[1]
    W = BRANCH_W
    rows = S // GRID_W
    nk = NA_ROWS * GRID_W
    tile = lambda cb: pl.BlockSpec((None, rps * GRID_W, W), lambda b, i: (b, i, cb))
    full = lambda cb: pl.BlockSpec((None, S, W), lambda b, i: (b, 0, cb))
    ctx = lambda cb: pl.BlockSpec((None, C, W), lambda b, i: (b, 0, cb))

    def bias_spec(rr):
        def index(b, i):
            r = i * rps + rr
            return (l, r - _na_row_start(r, rows), 0, 0, 0)
        return pl.BlockSpec((None, None, D_H, GRID_W, nk), index)

    return pl.pallas_call(
        functools.partial(_branch_d_kernel, rows=rows),
        grid=(B, rows // rps),
        in_specs=[tile(COL_DQ), full(COL_DK), full(COL_DV), ctx(COL_DK), ctx(COL_DV), tile(COL_DZ)]
                 + [bias_spec(rr) for rr in range(rps)],
        out_specs=pl.BlockSpec((None, rps * GRID_W, W), lambda b, i: (b, i, 0)),
        out_shape=jax.ShapeDtypeStruct((B, S, W), BF16),
        compiler_params=_cparams(2),
        name="branch_d",
    )(z, z, z, zc, zc, z, *([bias_tabs] * rps))


def _ctx_d_kernel(q_ref, k_ref, v_ref, dz_ref, o_ref):
    hpg = LANES // D_HD
    for grp in range(BRANCH_W // LANES):
        cols = slice(grp * LANES, (grp + 1) * LANES)
        o = _attend(q_ref[:, cols], [(k_ref[:, cols], v_ref[:, cols], None)], None, range(hpg), D_HD)
        o_ref[:, cols] = (o * _silu(dz_ref[:, cols].astype(F32))).astype(BF16)


def _ctx_d(zc):
    B, C, _ = zc.shape
    W = BRANCH_W
    col = lambda cb: pl.BlockSpec((None, C, W), lambda b: (b, 0, cb))
    return pl.pallas_call(
        _ctx_d_kernel,
        grid=(B,),
        in_specs=[col(COL_DQ), col(COL_DK), col(COL_DV), col(COL_DZ)],
        out_specs=pl.BlockSpec((None, C, W), lambda b: (b, 0, 0)),
        out_shape=jax.ShapeDtypeStruct((B, C, W), BF16),
        compiler_params=_cparams(1),
        name="ctx_d",
    )(zc, zc, zc, zc)


def _gate_mix_kernel(n_ref, ua_ref, ub_ref, uc_ref, ud_ref, wg_ref, wb_ref, o_ref):
    n = n_ref[...]
    mix = None
    for b, u_ref in enumerate((ua_ref, ub_ref, uc_ref, ud_ref)):
        gate = _sigmoid(jnp.dot(n, wg_ref[b], preferred_element_type=F32))
        t = gate * jnp.dot(u_ref[...], wb_ref[b], preferred_element_type=F32)
        mix = t if mix is None else mix + t
    o_ref[...] = mix.astype(BF16)


def _gate_mix(n, us, wg, wb, tm, cn, l):
    B, S, D = n.shape
    W = BRANCH_W
    row = lambda width: pl.BlockSpec((None, tm, width), lambda j, b, i: (b, i, 0))
    uc_spec = row(W)
    if us[2].ndim == 4:
        nt = S // (2 * tm)
        uc_spec = pl.BlockSpec((None, None, tm, W),
                               lambda j, b, i: (b, i // nt, jnp.where(i < nt, i, 2 * nt - 1 - i), 0))
    return pl.pallas_call(
        _gate_mix_kernel,
        grid=(D // cn, B, S // tm),
        in_specs=[row(D), row(W), row(W), uc_spec, row(W),
                  pl.BlockSpec((None, N_BRANCH, D, cn), lambda j, b, i: (l, 0, 0, j)),
                  pl.BlockSpec((None, N_BRANCH, W, cn), lambda j, b, i: (l, 0, 0, j))],
        out_specs=pl.BlockSpec((None, tm, cn), lambda j, b, i: (b, i, j)),
        out_shape=jax.ShapeDtypeStruct((B, S, D), BF16),
        compiler_params=_cparams(3),
        name="gate_mix",
    )(n, *us, wg, wb)


def _out_proj_kernel(h_ref, mod_ref, mix_ref, wo_ref, o_ref):
    D = h_ref.shape[-1]
    y = jnp.dot(mix_ref[...], wo_ref[...], preferred_element_type=F32)
    o_ref[...] = h_ref[...] + mod_ref[:, 2 * D:3 * D] * y


def _out_proj(h, mod, mod_row, mix, wo, tm, l):
    B, S, D = h.shape
    row = lambda: pl.BlockSpec((None, tm, D), lambda b, i: (b, i, 0))
    return pl.pallas_call(
        _out_proj_kernel,
        grid=(B, S // tm),
        in_specs=[row(),
                  pl.BlockSpec((None, None, 1, 3 * D), lambda b, i: (l, mod_row(b), 0, 0)),
                  row(),
                  pl.BlockSpec((None, D, D), lambda b, i: (l, 0, 0), pipeline_mode=pl.Buffered(1))],
        out_specs=row(),
        out_shape=jax.ShapeDtypeStruct((B, S, D), F32),
        compiler_params=_cparams(2),
        name="out_proj",
    )(h, mod, mix, wo)


def _merge(h, mod, mod_row, n, us, wg, wb, wo, tm, cn, l):
    return _out_proj(h, mod, mod_row, _gate_mix(n, us, wg, wb, tm, cn, l), wo, tm, l)


def _rope_tables(S):
    t = np.arange(S)
    pos = np.stack([t // GRID_W, t % GRID_W], axis=1).astype(np.float32)
    lane = np.arange(B_HD)
    which = lane // (B_HD // 2)
    fi = lane % (B_HD // 4)
    inv = (ROPE_BASE ** (-(fi.astype(np.float32)) / (B_HD // 4))).astype(np.float32)
    ang = pos[:, which] * inv[None, :]
    cos, sin = np.cos(ang), np.sin(ang)
    lower = (lane % (B_HD // 2)) < (B_HD // 4)
    sina = np.where(lower[None, :], -sin, 0.0)
    sinb = np.where(lower[None, :], 0.0, sin)
    rep = lambda a: jnp.asarray(np.tile(a.astype(np.float32), (1, LANES // B_HD)))
    return rep(cos), rep(sina), rep(sinb)


def _block_diag_mean(width, hd):
    i = np.arange(width)
    return jnp.asarray(((i[:, None] // hd) == (i[None, :] // hd)).astype(np.float32) / hd, dtype=BF16)


def _dft_cos_sin(n):
    k = jnp.arange(n, dtype=jnp.int32)
    m = (k[:, None] * k[None, :]) % n
    ang = m.astype(F32) * (2.0 * math.pi / n)
    return jnp.cos(ang), jnp.sin(ang)


def _dft_half_matrix(n):
    k = jnp.arange(n // 2 + 8, dtype=jnp.int32)
    m = jnp.arange(n // 2, dtype=jnp.int32)
    ang = ((k[:, None] * m[None, :]) % n).astype(F32) * (2.0 * math.pi / n)
    return jnp.concatenate([jnp.cos(ang), -jnp.sin(ang)], axis=1).astype(BF16)


def _rev_shift_matrix(tm):
    r = np.arange(tm)
    return jnp.asarray(((r[:, None] >= 1) & (r[None, :] == tm - r[:, None])).astype(np.float32), dtype=BF16)


def _bias_expand_kernel(rpb_ref, e_ref, o_ref):
    o_ref[...] = LOG2E * jnp.dot(rpb_ref[...], e_ref[...], preferred_element_type=F32,
                                 precision=lax.Precision.HIGHEST)


def _na_bias_tables(rpb):
    L, H, n_dr, n_dc = rpb.shape
    c = np.arange(GRID_W)
    c0 = np.clip(c - NA_COLS // 2, 0, GRID_W - NA_COLS)
    kc = np.arange(GRID_W)
    inwin = (kc[None, :] >= c0[:, None]) & (kc[None, :] < c0[:, None] + NA_COLS)
    dc = kc[None, :] - c[:, None] + (NA_COLS - 1)
    n_dc_pad = 32
    onehot = (np.arange(n_dc_pad)[:, None, None] == dc[None]) & inwin[None]
    e2d = jnp.asarray(onehot.reshape(n_dc_pad, GRID_W * GRID_W).astype(np.float32))
    rpb2d = jnp.pad(rpb.reshape(L, H * n_dr, n_dc), ((0, 0), (0, 0), (0, n_dc_pad - n_dc)))
    t = pl.pallas_call(
        _bias_expand_kernel,
        grid=(L,),
        in_specs=[pl.BlockSpec((None, H * n_dr, n_dc_pad), lambda l: (l, 0, 0)),
                  pl.BlockSpec((n_dc_pad, GRID_W * GRID_W), lambda l: (0, 0))],
        out_specs=pl.BlockSpec((None, H * n_dr, GRID_W * GRID_W), lambda l: (l, 0, 0)),
        out_shape=jax.ShapeDtypeStruct((L, H * n_dr, GRID_W * GRID_W), F32),
        compiler_params=_cparams(1),
        name="bias_expand",
    )(rpb2d, e2d)
    t = t.reshape(L, H, n_dr, GRID_W, GRID_W)
    t = jnp.where(jnp.asarray(inwin)[None, None, None], t, NEG)
    t = t.transpose(0, 1, 3, 2, 4).reshape(L, H, GRID_W, n_dr * GRID_W)
    nk = NA_ROWS * GRID_W
    cases = [t[..., (NA_ROWS - 1 - d) * GRID_W:(NA_ROWS - 1 - d) * GRID_W + nk] for d in range(NA_ROWS)]
    return jnp.stack(cases, axis=1)


def kernel(x, c, ctx, c_ctx, norm_g, w_ada, b_ada, w_in, a_norm_g, a_w_s, a_b_s, b_q_g, b_k_g, b_sink,
           c_w_f, d_q_g, d_k_g, d_rpb, w_gate, w_branch, w_out):
    B, S, D = x.shape
    C = ctx.shape[1]
    L = norm_g.shape[0]
    assert S % 512 == 0 and S // GRID_W >= NA_ROWS and S >= 3 * B_BLK and C % CHUNK == 0
    tm = 512
    tm_ctx = math.gcd(B * C, 512)
    cn = 512
    tm_in = 512
    tq_b = 8 * B_BLK
    qt_b = B_BLK
    rps_d = 4
    W = BRANCH_W

    n_rows = -(-(B + 1) // 8) * 8
    cc = jnp.concatenate([c, c_ctx[None], jnp.zeros((n_rows - B - 1, D), F32)], axis=0)
    mod = _modulation(cc, w_ada, b_ada).reshape(L, n_rows, 1, 3 * D)
    lat_row = lambda b: b
    ctx_row = lambda b: B

    w_in_b = w_in.astype(BF16)
    w_gate_b, w_branch_b, w_out_b = w_gate.astype(BF16), w_branch.astype(BF16), w_out.astype(BF16)
    a_params = (a_norm_g, a_w_s.astype(BF16), jnp.swapaxes(a_b_s, 1, 2))
    gains = jnp.stack([jnp.tile(b_q_g, (1, W // B_HD)),
                       jnp.tile(b_k_g, (1, W // B_HD)),
                       jnp.tile(d_q_g, (1, W // D_HD)),
                       jnp.tile(d_k_g, (1, W // D_HD))] + [jnp.zeros((L, W), F32)] * 4, axis=1)
    bd64, bd32 = _block_diag_mean(W // 2, B_HD), _block_diag_mean(W // 2, D_HD)
    ropes = _rope_tables(S)
    c128, s128 = _dft_cos_sin(C_GW)
    tm_f, tm_fc = min(512, S // 2), min(512, C // 2)
    dft_lat, dft_ctx = _dft_half_matrix(S), _dft_half_matrix(C)
    jm_lat, jm_ctx = _rev_shift_matrix(tm_f), _rev_shift_matrix(tm_fc)
    assert tm_f == tm and tm_fc == C // 2
    bias_tabs = _na_bias_tables(d_rpb)

    h, hc = x, ctx.reshape(1, B * C, D)
    for l in range(L):
        last = l == L - 1
        n_lat, z, ua = _inproj(h, mod, lat_row, norm_g, w_in_b, gains, bd64, bd32, ropes, a_params, tm_in, True, l)
        n_ctx, zc, ua_ctx = _inproj(hc, mod, ctx_row, norm_g, w_in_b, gains, bd64, bd32, ropes, a_params,
                                    tm_ctx, False, l)
        zc = zc.reshape(B, C, Z_WIDTH)
        cw, sw = _dft_weights(c128, s128, c_w_f[l])

        u_lat = [
            ua,
            _branch_b(z, zc, b_sink[l], tq_b, qt_b),
            _fourier_branch(z, cw, sw, dft_lat, jm_lat, tm_f),
            _branch_d(z, zc, bias_tabs, rps_d, l),
        ]
        h = _merge(h, mod, lat_row, n_lat, u_lat, w_gate_b, w_branch_b, w_out_b, tm, cn, l)
        if not last:
            u_ctx = [
                ua_ctx,
                _ctx_b(zc, b_sink[l]),
                _fourier_branch(zc, cw, sw, dft_ctx, jm_ctx, tm_fc).reshape(B, C, W),
                _ctx_d(zc),
            ]
            u_ctx = [u.reshape(1, B * C, W) for u in u_ctx]
            hc = _merge(hc, mod, ctx_row, n_ctx, u_ctx, w_gate_b, w_branch_b, w_out_b, tm_ctx, cn, l)
    return h
```

```python
import functools
import math

import jax
import jax.numpy as jnp
import numpy as np
from jax import lax
from jax.experimental import pallas as pl
from jax.experimental.pallas import tpu as pltpu

F32 = jnp.float32
BF16 = jnp.bfloat16

LANES = 128
V7X_VMEM_LIMIT_BYTES = 56 * 1024 * 1024

GRID_W = 64
N_BRANCH = 4
BRANCH_W = 512
CHUNK = 128
A_GROUPS = 4
B_HD = 64
B_QH = 8
B_KVH = 2
B_WIN = 128
B_BLK = 128
C_GROUPS = 4
C_GW = 128
D_HD = 32
D_H = 16
NA_ROWS = 8
NA_COLS = 16
ROPE_BASE = 10000.0
EPS = 1e-6
NEG = -1e30
LOG2E = math.log2(math.e)

(COL_BQ, COL_BKV, COL_BZ, COL_FIN, COL_FZ, COL_DQ, COL_DK, COL_DV, COL_DZ) = range(9)
Z_WIDTH = 9 * BRANCH_W
W_IN_SRC = {COL_BQ: 1536, COL_BZ: 2304, COL_FIN: 2816, COL_FZ: 3328,
            COL_DQ: 3840, COL_DK: 4352, COL_DV: 4864, COL_DZ: 5376}
W_IN_BK = 2048
W_IN_AU, W_IN_AV, W_IN_AZ = 0, 512, 1024


def _cparams(n_grid, flags=None):
    return pltpu.CompilerParams(
        dimension_semantics=("arbitrary",) * n_grid,
        vmem_limit_bytes=V7X_VMEM_LIMIT_BYTES,
        flags=flags)


def _sigmoid(x):
    return 1.0 / (1.0 + jnp.exp(-x))


def _silu(x):
    return x * _sigmoid(x)


def _gelu(x):
    return 0.5 * x * (1.0 + jnp.tanh(math.sqrt(2.0 / math.pi) * (x + 0.044715 * (x * x * x))))


def _mod_kernel(c_ref, w_ref, b_ref, o_ref):
    c = c_ref[...]
    a = _silu(c).astype(BF16)
    o_ref[...] = jnp.dot(a, w_ref[...].astype(BF16), preferred_element_type=F32) + b_ref[...]


def _modulation(cc, w_ada, b_ada):
    L, D, N = w_ada.shape
    R = cc.shape[0]
    tn = 512
    return pl.pallas_call(
        _mod_kernel,
        grid=(L, N // tn),
        in_specs=[pl.BlockSpec((R, D), lambda l, j: (0, 0)),
                  pl.BlockSpec((None, D, tn), lambda l, j: (l, 0, j)),
                  pl.BlockSpec((None, 1, tn), lambda l, j: (l, 0, j))],
        out_specs=pl.BlockSpec((None, R, tn), lambda l, j: (l, 0, j)),
        out_shape=jax.ShapeDtypeStruct((L, R, N), F32),
        compiler_params=_cparams(2),
        name="modulation",
    )(cc, w_ada, b_ada.reshape(L, 1, N))


def _head_rms(x, bd_ref, width):
    sq = (x * x).astype(BF16)
    c = min(bd_ref.shape[0], width)
    bd = bd_ref[0:c, 0:c]
    parts = [jnp.dot(sq[:, o:o + c], bd, preferred_element_type=F32) for o in range(0, width, c)]
    return parts[0] if len(parts) == 1 else jnp.concatenate(parts, axis=1)


def _rope(x, cos, sina, sinb):
    w = x.shape[-1]
    reps = w // LANES
    cos = jnp.concatenate([cos] * reps, axis=1)
    sina = jnp.concatenate([sina] * reps, axis=1)
    sinb = jnp.concatenate([sinb] * reps, axis=1)
    up = pltpu.roll(x, w - 16, 1)
    dn = pltpu.roll(x, 16, 1)
    return x * cos + up * sina + dn * sinb


def _ada_norm(x, mod_ref, ng_ref):
    D = x.shape[-1]
    ms = jnp.mean(x * x, axis=-1, keepdims=True)
    y = x * lax.rsqrt(ms + EPS) * ng_ref[...]
    return (y * (1.0 + mod_ref[:, D:2 * D]) + mod_ref[:, 0:D]).astype(BF16)


def _dup_heads(t):
    swapped = pltpu.roll(t, B_HD, 1)
    low = lax.broadcasted_iota(jnp.int32, (1, LANES), 1) < B_HD
    return jnp.concatenate([jnp.where(low, t, swapped), jnp.where(low, swapped, t)], axis=1)


def _chunk_mlp(nb, w_ref, gv_ref, ws_ref, bs_ref, ua_ref):
    W = BRANCH_W
    proj = lambda src: jnp.dot(nb, w_ref[:, src:src + W], preferred_element_type=F32)
    v = _gelu(proj(W_IN_AV))
    ms = jnp.mean(v * v, axis=-1, keepdims=True)
    vb = (v * lax.rsqrt(ms + EPS) * gv_ref[...]).astype(BF16)
    gate = _gelu(proj(W_IN_AU)) * _silu(proj(W_IN_AZ))
    for c in range(nb.shape[0] // CHUNK):
        rows = slice(c * CHUNK, (c + 1) * CHUNK)
        for g in range(A_GROUPS):
            cols = slice(g * LANES, (g + 1) * LANES)
            mixed = jnp.dot(ws_ref[g], vb[rows, cols], preferred_element_type=F32) + bs_ref[:, g:g + 1]
            ua_ref[rows, cols] = (gate[rows, cols] * mixed).astype(BF16)


def _inproj_kernel(h_ref, mod_ref, ng_ref, w_ref, gains_ref, bd64_ref, bd32_ref,
                   cos_ref, sina_ref, sinb_ref, gv_ref, ws_ref, bs_ref, n_ref, z_ref, ua_ref, *, rope):
    W = BRANCH_W
    nb = _ada_norm(h_ref[...], mod_ref, ng_ref)
    n_ref[...] = nb
    _chunk_mlp(nb, w_ref, gv_ref, ws_ref, bs_ref, ua_ref)

    def maybe_rope(t):
        if rope:
            return _rope(t, cos_ref[...], sina_ref[...], sinb_ref[...])
        return t

    def head_norm(t, bd_ref, gain):
        return t * lax.rsqrt(_head_rms(t, bd_ref, t.shape[-1]) + EPS) * gain

    for cb in range(Z_WIDTH // W):
        if cb == COL_BKV:
            kvw = B_KVH * B_HD
            kv = jnp.dot(nb, w_ref[:, W_IN_BK:W_IN_BK + 2 * kvw], preferred_element_type=F32)
            k = maybe_rope(head_norm(kv[:, 0:kvw], bd64_ref, gains_ref[1:2, 0:kvw]))
            acc = jnp.concatenate([_dup_heads(k), _dup_heads(kv[:, kvw:])], axis=1)
        else:
            src = W_IN_SRC[cb]
            acc = jnp.dot(nb, w_ref[:, src:src + W], preferred_element_type=F32)
            if cb == COL_BQ:
                acc = maybe_rope(head_norm(acc, bd64_ref, gains_ref[0:1, :])) * (B_HD ** -0.5 * LOG2E)
            elif cb == COL_DQ:
                acc = head_norm(acc, bd32_ref, gains_ref[2:3, :]) * (D_HD ** -0.5 * LOG2E)
            elif cb == COL_DK:
                acc = head_norm(acc, bd32_ref, gains_ref[3:4, :])
        z_ref[:, cb * W:(cb + 1) * W] = acc.astype(BF16)


def _inproj(h, mod, mod_row, norm_g, w_in, gains, bd64, bd32, ropes, a_params, tm, rope, l):
    g_v, w_s, b_sT = a_params
    B, S, D = h.shape
    cos, sina, sinb = ropes
    layer = lambda shape: pl.BlockSpec((None,) + shape, lambda b, i: (l,) + (0,) * len(shape),
                                       pipeline_mode=pl.Buffered(1))
    const = lambda shape: pl.BlockSpec(shape, lambda b, i: (0,) * len(shape), pipeline_mode=pl.Buffered(1))
    rope_spec = pl.BlockSpec((tm, LANES), lambda b, i: (i if rope else 0, 0))
    return pl.pallas_call(
        functools.partial(_inproj_kernel, rope=rope),
        grid=(B, S // tm),
        in_specs=[pl.BlockSpec((None, tm, D), lambda b, i: (b, i, 0)),
                  pl.BlockSpec((None, None, 1, 3 * D), lambda b, i: (l, mod_row(b), 0, 0)),
                  layer((1, D)),
                  layer(w_in.shape[1:]),
                  layer((8, BRANCH_W)),
                  const(bd64.shape),
                  const(bd32.shape),
                  rope_spec, rope_spec, rope_spec,
                  layer((1, BRANCH_W)),
                  layer((A_GROUPS, CHUNK, CHUNK)),
                  layer((CHUNK, A_GROUPS))],
        out_specs=[pl.BlockSpec((None, tm, D), lambda b, i: (b, i, 0)),
                   pl.BlockSpec((None, tm, Z_WIDTH), lambda b, i: (b, i, 0)),
                   pl.BlockSpec((None, tm, BRANCH_W), lambda b, i: (b, i, 0))],
        out_shape=[jax.ShapeDtypeStruct((B, S, D), BF16),
                   jax.ShapeDtypeStruct((B, S, Z_WIDTH), BF16),
                   jax.ShapeDtypeStruct((B, S, BRANCH_W), BF16)],
        compiler_params=_cparams(2),
        name="inproj",
    )(h, mod, norm_g.reshape(norm_g.shape[0], 1, D), w_in, gains, bd64, bd32, cos, sina, sinb,
      g_v.reshape(g_v.shape[0], 1, BRANCH_W), w_s, b_sT)


def _dftw_kernel(c_ref, s_ref, w_ref, cw_ref, sw_ref):
    for g in range(C_GROUPS):
        w = w_ref[g]
        cw_ref[g] = jnp.dot(c_ref[...], w, preferred_element_type=F32,
                            precision=lax.Precision.HIGHEST).astype(BF16)
        sw_ref[g] = jnp.dot(s_ref[...], w, preferred_element_type=F32,
                            precision=lax.Precision.HIGHEST).astype(BF16)


def _dft_weights(c128, s128, w_f):
    shp = jax.ShapeDtypeStruct((C_GROUPS, C_GW, C_GW), BF16)
    return pl.pallas_call(_dftw_kernel, out_shape=[shp, shp], name="dft_weights")(c128, s128, w_f)


def _fourier_pq_kernel(xa_ref, xm_ref, xn_ref, xh_ref, jm_ref, cw_ref, sw_ref, pq_ref, aux_ref):
    i = pl.program_id(1)
    tm = xa_ref.shape[0]
    row = lax.broadcasted_iota(jnp.int32, (tm, 1), 0)
    first = jnp.where(i > 0, xn_ref[0:1, :].astype(F32), 0.0)
    xb = jnp.where(row == 0, first, jnp.dot(jm_ref[...], xm_ref[...], preferred_element_type=F32))
    xa = xa_ref[...].astype(F32)
    xe = (xa + xb).astype(BF16)
    xo = (xa - xb).astype(BF16)
    for g in range(C_GROUPS):
        cols = slice(g * LANES, (g + 1) * LANES)
        pq_ref[0, :, cols] = jnp.dot(xe[:, cols], cw_ref[g], preferred_element_type=F32).astype(BF16)
        pq_ref[1, :, cols] = jnp.dot(xo[:, cols], sw_ref[g], preferred_element_type=F32).astype(BF16)

    @pl.when(i == 0)
    def _():
        for g in range(C_GROUPS):
            cols = slice(g * LANES, (g + 1) * LANES)
            aux_ref[:, cols] = jnp.dot(xh_ref[:, cols], cw_ref[g], preferred_element_type=F32)


def _fourier_pq(z, jm, cw, sw, tm):
    B, L, _ = z.shape
    H = L // 2
    W = BRANCH_W
    nt = H // tm
    wspec = pl.BlockSpec((C_GROUPS, C_GW, C_GW), lambda b, i: (0, 0, 0))
    return pl.pallas_call(
        _fourier_pq_kernel,
        grid=(B, nt),
        in_specs=[pl.BlockSpec((None, tm, W), lambda b, i: (b, i, COL_FIN)),
                  pl.BlockSpec((None, tm, W), lambda b, i: (b, 2 * nt - 1 - i, COL_FIN)),
                  pl.BlockSpec((None, 8, W),
                               lambda b, i: (b, jnp.minimum((L - i * tm) // 8, L // 8 - 1), COL_FIN)),
                  pl.BlockSpec((None, 8, W), lambda b, i: (b, H // 8, COL_FIN)),
                  pl.BlockSpec((tm, tm), lambda b, i: (0, 0)),
                  wspec, wspec],
        out_specs=[pl.BlockSpec((None, 2, tm, W), lambda b, i: (b, 0, i, 0)),
                   pl.BlockSpec((None, 8, W), lambda b, i: (b, 0, 0))],
        out_shape=[jax.ShapeDtypeStruct((B, 2, H, W), BF16),
                   jax.ShapeDtypeStruct((B, 8, W), F32)],
        compiler_params=_cparams(2),
        name="fourier_pq",
    )(z, z, z, z, jm, cw, sw)


def _fourier_mix_kernel(dft_ref, dftx_ref, pq_ref, aux_ref, jm_ref, fza_ref, fzb_ref, u_ref, *, scale, half):
    tm = dft_ref.shape[0]
    pe = pq_ref[0:half, :]
    qo = pq_ref[half:, :]
    ev = jnp.dot(dft_ref[:, 0:half], pe, preferred_element_type=F32)
    od = jnp.dot(dft_ref[:, half:], qo, preferred_element_type=F32)
    row = lax.broadcasted_iota(jnp.int32, (tm, 1), 0)
    nyq = (1 - 2 * (row & 1)).astype(F32) * aux_ref[0:1, :]
    y1 = ev + od + nyq
    y2 = ev - od + nyq
    evx = jnp.dot(dftx_ref[:, 0:half], pe, preferred_element_type=F32)
    odx = jnp.dot(dftx_ref[:, half:], qo, preferred_element_type=F32)
    y2x = evx[0:1, :] - odx[0:1, :] + aux_ref[0:1, :]
    hi = y2.astype(BF16)
    lo = (y2 - hi.astype(F32)).astype(BF16)
    y2r = (jnp.dot(jm_ref[...], hi, preferred_element_type=F32)
           + jnp.dot(jm_ref[...], lo, preferred_element_type=F32))
    y2r = jnp.where(row == 0, y2x, y2r)
    u_ref[0] = (y1 * scale * _silu(fza_ref[...].astype(F32))).astype(BF16)
    u_ref[1] = (y2r * scale * _silu(fzb_ref[...].astype(F32))).astype(BF16)


def _fourier_mix(dft, pq, aux, jm, z, tm):
    B, L, _ = z.shape
    H = L // 2
    W = BRANCH_W
    nt = H // tm
    scale = 1.0 / math.sqrt(L * C_GW)
    return pl.pallas_call(
        functools.partial(_fourier_mix_kernel, scale=scale, half=H),
        grid=(B, nt),
        in_specs=[pl.BlockSpec((tm, L), lambda b, i: (i, 0)),
                  pl.BlockSpec((8, L), lambda b, i: ((i + 1) * (tm // 8), 0)),
                  pl.BlockSpec((None, L, W), lambda b, i: (b, 0, 0)),
                  pl.BlockSpec((None, 8, W), lambda b, i: (b, 0, 0)),
                  pl.BlockSpec((tm, tm), lambda b, i: (0, 0)),
                  pl.BlockSpec((None, tm, W), lambda b, i: (b, i, COL_FZ)),
                  pl.BlockSpec((None, tm, W), lambda b, i: (b, 2 * nt - 1 - i, COL_FZ))],
        out_specs=pl.BlockSpec((None, 2, tm, W), lambda b, i: (b, 0, i, 0)),
        out_shape=jax.ShapeDtypeStruct((B, 2, H, W), BF16),
        compiler_params=_cparams(2),
        name="fourier_mix",
    )(dft, dft, pq, aux, jm, z, z)


def _fourier_branch(z, cw, sw, dft, jm, tm):
    B, L, _ = z.shape
    pq, aux = _fourier_pq(z, jm, cw, sw, tm)
    return _fourier_mix(dft, pq.reshape(B, L, BRANCH_W), aux, jm, z, tm)


def _lane_masks(heads, hd):
    lane = lax.broadcasted_iota(jnp.int32, (1, LANES), 1)
    return [((lane >= g * hd) & (lane < (g + 1) * hd)) for g in heads]


def _attend(q, segs, sink_col, heads, hd):
    tq = q.shape[0]
    masks = _lane_masks(heads, hd)
    zero = jnp.zeros_like(q)
    qx = jnp.concatenate([jnp.where(m, q, zero) for m in masks], axis=0)
    scores = []
    for k, _, bias in segs:
        s = lax.dot_general(qx, k, (((1,), (1,)), ((), ())), preferred_element_type=F32)
        if bias is not None:
            s = s + bias
        scores.append(s)
    m = scores[0].max(axis=-1, keepdims=True)
    for s in scores[1:]:
        m = jnp.maximum(m, s.max(axis=-1, keepdims=True))
    if sink_col is not None:
        m = jnp.maximum(m, sink_col)
    den = None
    acc = None
    for s, (_, v, _) in zip(scores, segs):
        p = jnp.exp2(s - m)
        d = p.sum(axis=-1, keepdims=True)
        o = jnp.dot(p.astype(BF16), v, preferred_element_type=F32)
        den = d if den is None else den + d
        acc = o if acc is None else acc + o
    if sink_col is not None:
        den = den + jnp.exp2(sink_col - m)
    acc = acc / den
    out = None
    for g, msk in enumerate(masks):
        part = jnp.where(msk, acc[g * tq:(g + 1) * tq], 0.0)
        out = part if out is None else out + part
    return out


def _attend_blocks(q_lists, local_list, ctx_kv, sink_cols, heads, hd):
    tq = q_lists[0][0].shape[0]
    nsub = len(q_lists[0])
    masks = _lane_masks(heads, hd)
    rows = len(masks) * tq
    nt = (((1,), (1,)), ((), ()))

    def expand(q):
        zero = jnp.zeros_like(q)
        return jnp.concatenate([jnp.where(m, q, zero) for m in masks], axis=0)

    def cat(parts):
        return parts[0] if len(parts) == 1 else jnp.concatenate(parts, axis=0)

    qx = [cat([expand(q) for q in ql]) for ql in q_lists]
    kc, vc = ctx_kv
    s_c = lax.dot_general(cat(qx), kc, nt, preferred_element_type=F32)
    p_c_list, den_list, o_list = [], [], []
    for j, (qx_j, (k, v, bias)) in enumerate(zip(qx, local_list)):
        s_all = lax.dot_general(qx_j, k, nt, preferred_element_type=F32)
        p_tiles = []
        for t in range(nsub):
            idx = j * nsub + t
            s = s_all[t * rows:(t + 1) * rows] + bias
            s_cj = s_c[idx * rows:(idx + 1) * rows]
            m = jnp.maximum(s.max(axis=-1, keepdims=True), s_cj.max(axis=-1, keepdims=True))
            if sink_cols is not None:
                m = jnp.maximum(m, sink_cols[t])
            p = jnp.exp2(s - m)
            p_c = jnp.exp2(s_cj - m)
            den = p.sum(axis=-1, keepdims=True) + p_c.sum(axis=-1, keepdims=True)
            if sink_cols is not None:
                den = den + jnp.exp2(sink_cols[t] - m)
            p_tiles.append(p.astype(BF16))
            p_c_list.append(p_c.astype(BF16))
            den_list.append(den)
        o_list.append(jnp.dot(cat(p_tiles), v, preferred_element_type=F32))
    o_c = jnp.dot(cat(p_c_list), vc, preferred_element_type=F32)
    outs = []
    for j in range(len(q_lists)):
        outs_j = []
        for t in range(nsub):
            idx = j * nsub + t
            acc = (o_list[j][t * rows:(t + 1) * rows] + o_c[idx * rows:(idx + 1) * rows]) / den_list[idx]
            out = None
            for g, msk in enumerate(masks):
                part = jnp.where(msk, acc[g * tq:(g + 1) * tq], 0.0)
                out = part if out is None else out + part
            outs_j.append(out)
        outs.append(outs_j)
    return outs


def _branch_b_kernel(sink_ref, q_ref, kv_ref, ckv_ref, bz_ref, o_ref, *, seq, qt):
    nl = qt + 2 * B_WIN
    half = BRANCH_W // 2
    n_blk = q_ref.shape[0] // qt
    starts, biases = [], []
    for blk in range(n_blk):
        i = pl.program_id(1) * n_blk + blk
        start = pl.multiple_of(jnp.clip(i * qt - B_WIN, 0, seq - nl), B_BLK)
        qpos = i * qt + lax.broadcasted_iota(jnp.int32, (qt, nl), 0)
        kpos = start + lax.broadcasted_iota(jnp.int32, (qt, nl), 1)
        bias1 = jnp.where(jnp.abs(kpos - qpos) <= B_WIN, 0.0, NEG).astype(F32)
        starts.append(start)
        biases.append(jnp.concatenate([bias1, bias1], axis=0))
    for kvh in range(B_KVH):
        kc = slice(kvh * LANES, (kvh + 1) * LANES)
        vc = slice(half + kvh * LANES, half + (kvh + 1) * LANES)
        grps = (2 * kvh, 2 * kvh + 1)
        sink_cols = [jnp.concatenate([jnp.full((qt, 1), sink_ref[2 * grp + g] * LOG2E, F32) for g in range(2)],
                                     axis=0) for grp in grps]
        q_lists = [[q_ref[blk * qt:(blk + 1) * qt, grp * LANES:(grp + 1) * LANES] for grp in grps]
                   for blk in range(n_blk)]
        local = [(kv_ref[pl.ds(s, nl), kc], kv_ref[pl.ds(s, nl), vc], b) for s, b in zip(starts, biases)]
        outs = _attend_blocks(q_lists, local, (ckv_ref[:, kc], ckv_ref[:, vc]), sink_cols, range(2), B_HD)
        for blk, outs_blk in enumerate(outs):
            rows = slice(blk * qt, (blk + 1) * qt)
            for grp, o in zip(grps, outs_blk):
                cols = slice(grp * LANES, (grp + 1) * LANES)
                o_ref[rows, cols] = (o * _silu(bz_ref[rows, cols].astype(F32))).astype(BF16)


def _branch_b(z, zc, sink, tq, qt):
    B, S, _ = z.shape
    C = zc.shape[1]
    W = BRANCH_W
    assert tq % qt == 0 and qt % B_BLK == 0 and S >= qt + 2 * B_WIN
    return pl.pallas_call(
        functools.partial(_branch_b_kernel, seq=S, qt=qt),
        grid=(B, S // tq),
        in_specs=[pl.BlockSpec(memory_space=pltpu.SMEM),
                  pl.BlockSpec((None, tq, W), lambda b, i: (b, i, COL_BQ)),
                  pl.BlockSpec((None, S, W), lambda b, i: (b, 0, COL_BKV)),
                  pl.BlockSpec((None, C, W), lambda b, i: (b, 0, COL_BKV)),
                  pl.BlockSpec((None, tq, W), lambda b, i: (b, i, COL_BZ))],
        out_specs=pl.BlockSpec((None, tq, W), lambda b, i: (b, i, 0)),
        out_shape=jax.ShapeDtypeStruct((B, S, W), BF16),
        compiler_params=_cparams(2),
        name="branch_b",
    )(sink, z, z, zc, z)


def _ctx_b_kernel(sink_ref, q_ref, kv_ref, bz_ref, o_ref):
    tq = q_ref.shape[0]
    half = BRANCH_W // 2
    for grp in range(BRANCH_W // LANES):
        kvh = grp // 2
        kc = slice(kvh * LANES, (kvh + 1) * LANES)
        vc = slice(half + kvh * LANES, half + (kvh + 1) * LANES)
        cols = slice(grp * LANES, (grp + 1) * LANES)
        sink_col = jnp.concatenate(
            [jnp.full((tq, 1), sink_ref[2 * grp + g] * LOG2E, F32) for g in range(2)], axis=0)
        o = _attend(q_ref[:, cols], [(kv_ref[:, kc], kv_ref[:, vc], None)], sink_col, range(2), B_HD)
        o_ref[:, cols] = (o * _silu(bz_ref[:, cols].astype(F32))).astype(BF16)


def _ctx_b(zc, sink):
    B, C, _ = zc.shape
    W = BRANCH_W
    col = lambda cb: pl.BlockSpec((None, C, W), lambda b: (b, 0, cb))
    return pl.pallas_call(
        _ctx_b_kernel,
        grid=(B,),
        in_specs=[pl.BlockSpec(memory_space=pltpu.SMEM), col(COL_BQ), col(COL_BKV), col(COL_BZ)],
        out_specs=pl.BlockSpec((None, C, W), lambda b: (b, 0, 0)),
        out_shape=jax.ShapeDtypeStruct((B, C, W), BF16),
        compiler_params=_cparams(1),
        name="ctx_b",
    )(sink, zc, zc, zc)


def _na_row_start(r, rows):
    return jnp.clip(r - NA_ROWS // 2, 0, rows - NA_ROWS)


def _branch_d_kernel(q_ref, k_ref, v_ref, ck_ref, cv_ref, dz_ref, *rest, rows):
    bias_refs, o_ref = rest[:-1], rest[-1]
    nk = NA_ROWS * GRID_W
    hpg = LANES // D_HD
    n_rows = len(bias_refs)
    starts = [pl.multiple_of(_na_row_start(pl.program_id(1) * n_rows + rr, rows) * GRID_W, GRID_W)
              for rr in range(n_rows)]
    for grp in range(BRANCH_W // LANES):
        cols = slice(grp * LANES, (grp + 1) * LANES)
        q_lists = [[q_ref[rr * GRID_W:(rr + 1) * GRID_W, cols]] for rr in range(n_rows)]
        local = [(k_ref[pl.ds(s, nk), cols], v_ref[pl.ds(s, nk), cols],
                  b[grp * hpg:(grp + 1) * hpg].reshape(hpg * GRID_W, nk)) for s, b in zip(starts, bias_refs)]
        outs = _attend_blocks(q_lists, local, (ck_ref[:, cols], cv_ref[:, cols]), None, range(hpg), D_HD)
        for rr, (o,) in enumerate(outs):
            qrows = slice(rr * GRID_W, (rr + 1) * GRID_W)
            o_ref[qrows, cols] = (o * _silu(dz_ref[qrows, cols].astype(F32))).astype(BF16)


def _branch_d(z, zc, bias_tabs, rps, l):
    B, S, _ = z.shape
    C = zc.shape[1]
    W = BRANCH_W
    rows = S // GRID_W
    nk = NA_ROWS * GRID_W
    tile = lambda cb: pl.BlockSpec((None, rps * GRID_W, W), lambda b, i: (b, i, cb))
    full = lambda cb: pl.BlockSpec((None, S, W), lambda b, i: (b, 0, cb))
    ctx = lambda cb: pl.BlockSpec((None, C, W), lambda b, i: (b, 0, cb))

    def bias_spec(rr):
        def index(b, i):
            r = i * rps + rr
            return (l, r - _na_row_start(r, rows), 0, 0, 0)
        return pl.BlockSpec((None, None, D_H, GRID_W, nk), index)

    return pl.pallas_call(
        functools.partial(_branch_d_kernel, rows=rows),
        grid=(B, rows // rps),
        in_specs=[tile(COL_DQ), full(COL_DK), full(COL_DV), ctx(COL_DK), ctx(COL_DV), tile(COL_DZ)]
                 + [bias_spec(rr) for rr in range(rps)],
        out_specs=pl.BlockSpec((None, rps * GRID_W, W), lambda b, i: (b, i, 0)),
        out_shape=jax.ShapeDtypeStruct((B, S, W), BF16),
        compiler_params=_cparams(2),
        name="branch_d",
    )(z, z, z, zc, zc, z, *([bias_tabs] * rps))


def _ctx_d_kernel(q_ref, k_ref, v_ref, dz_ref, o_ref):
    hpg = LANES // D_HD
    for grp in range(BRANCH_W // LANES):
        cols = slice(grp * LANES, (grp + 1) * LANES)
        o = _attend(q_ref[:, cols], [(k_ref[:, cols], v_ref[:, cols], None)], None, range(hpg), D_HD)
        o_ref[:, cols] = (o * _silu(dz_ref[:, cols].astype(F32))).astype(BF16)


def _ctx_d(zc):
    B, C, _ = zc.shape
    W = BRANCH_W
    col = lambda cb: pl.BlockSpec((None, C, W), lambda b: (b, 0, cb))
    return pl.pallas_call(
        _ctx_d_kernel,
        grid=(B,),
        in_specs=[col(COL_DQ), col(COL_DK), col(COL_DV), col(COL_DZ)],
        out_specs=pl.BlockSpec((None, C, W), lambda b: (b, 0, 0)),
        out_shape=jax.ShapeDtypeStruct((B, C, W), BF16),
        compiler_params=_cparams(1),
        name="ctx_d",
    )(zc, zc, zc, zc)


def _gate_mix_kernel(n_ref, ua_ref, ub_ref, uc_ref, ud_ref, wg_ref, wb_ref, o_ref):
    n = n_ref[...]
    mix = None
    for b, u_ref in enumerate((ua_ref, ub_ref, uc_ref, ud_ref)):
        gate = _sigmoid(jnp.dot(n, wg_ref[b], preferred_element_type=F32))
        t = gate * jnp.dot(u_ref[...], wb_ref[b], preferred_element_type=F32)
        mix = t if mix is None else mix + t
    o_ref[...] = mix.astype(BF16)


def _gate_mix(n, us, wg, wb, tm, cn, l):
    B, S, D = n.shape
    W = BRANCH_W
    row = lambda width: pl.BlockSpec((None, tm, width), lambda j, b, i: (b, i, 0))
    uc_spec = row(W)
    if us[2].ndim == 4:
        nt = S // (2 * tm)
        uc_spec = pl.BlockSpec((None, None, tm, W),
                               lambda j, b, i: (b, i // nt, jnp.where(i < nt, i, 2 * nt - 1 - i), 0))
    return pl.pallas_call(
        _gate_mix_kernel,
        grid=(D // cn, B, S // tm),
        in_specs=[row(D), row(W), row(W), uc_spec, row(W),
                  pl.BlockSpec((None, N_BRANCH, D, cn), lambda j, b, i: (l, 0, 0, j)),
                  pl.BlockSpec((None, N_BRANCH, W, cn), lambda j, b, i: (l, 0, 0, j))],
        out_specs=pl.BlockSpec((None, tm, cn), lambda j, b, i: (b, i, j)),
        out_shape=jax.ShapeDtypeStruct((B, S, D), BF16),
        compiler_params=_cparams(3),
        name="gate_mix",
    )(n, *us, wg, wb)


def _out_proj_kernel(h_ref, mod_ref, mix_ref, wo_ref, o_ref):
    D = h_ref.shape[-1]
    y = jnp.dot(mix_ref[...], wo_ref[...], preferred_element_type=F32)
    o_ref[...] = h_ref[...] + mod_ref[:, 2 * D:3 * D] * y


def _out_proj(h, mod, mod_row, mix, wo, tm, l):
    B, S, D = h.shape
    row = lambda: pl.BlockSpec((None, tm, D), lambda b, i: (b, i, 0))
    return pl.pallas_call(
        _out_proj_kernel,
        grid=(B, S // tm),
        in_specs=[row(),
                  pl.BlockSpec((None, None, 1, 3 * D), lambda b, i: (l, mod_row(b), 0, 0)),
                  row(),
                  pl.BlockSpec((None, D, D), lambda b, i: (l, 0, 0), pipeline_mode=pl.Buffered(1))],
        out_specs=row(),
        out_shape=jax.ShapeDtypeStruct((B, S, D), F32),
        compiler_params=_cparams(2),
        name="out_proj",
    )(h, mod, mix, wo)


def _merge(h, mod, mod_row, n, us, wg, wb, wo, tm, cn, l):
    return _out_proj(h, mod, mod_row, _gate_mix(n, us, wg, wb, tm, cn, l), wo, tm, l)


def _rope_tables(S):
    t = np.arange(S)
    pos = np.stack([t // GRID_W, t % GRID_W], axis=1).astype(np.float32)
    lane = np.arange(B_HD)
    which = lane // (B_HD // 2)
    fi = lane % (B_HD // 4)
    inv = (ROPE_BASE ** (-(fi.astype(np.float32)) / (B_HD // 4))).astype(np.float32)
    ang = pos[:, which] * inv[None, :]
    cos, sin = np.cos(ang), np.sin(ang)
    lower = (lane % (B_HD // 2)) < (B_HD // 4)
    sina = np.where(lower[None, :], -sin, 0.0)
    sinb = np.where(lower[None, :], 0.0, sin)
    rep = lambda a: jnp.asarray(np.tile(a.astype(np.float32), (1, LANES // B_HD)))
    return rep(cos), rep(sina), rep(sinb)


def _block_diag_mean(width, hd):
    i = np.arange(width)
    return jnp.asarray(((i[:, None] // hd) == (i[None, :] // hd)).astype(np.float32) / hd, dtype=BF16)


def _dft_cos_sin(n):
    k = jnp.arange(n, dtype=jnp.int32)
    m = (k[:, None] * k[None, :]) % n
    ang = m.astype(F32) * (2.0 * math.pi / n)
    return jnp.cos(ang), jnp.sin(ang)


def _dft_half_matrix(n):
    k = jnp.arange(n // 2 + 8, dtype=jnp.int32)
    m = jnp.arange(n // 2, dtype=jnp.int32)
    ang = ((k[:, None] * m[None, :]) % n).astype(F32) * (2.0 * math.pi / n)
    return jnp.concatenate([jnp.cos(ang), -jnp.sin(ang)], axis=1).astype(BF16)


def _rev_shift_matrix(tm):
    r = np.arange(tm)
    return jnp.asarray(((r[:, None] >= 1) & (r[None, :] == tm - r[:, None])).astype(np.float32), dtype=BF16)


def _bias_expand_kernel(rpb_ref, e_ref, o_ref):
    o_ref[...] = LOG2E * jnp.dot(rpb_ref[...], e_ref[...], preferred_element_type=F32,
                                 precision=lax.Precision.HIGHEST)


def _na_bias_tables(rpb):
    L, H, n_dr, n_dc = rpb.shape
    c = np.arange(GRID_W)
    c0 = np.clip(c - NA_COLS // 2, 0, GRID_W - NA_COLS)
    kc = np.arange(GRID_W)
    inwin = (kc[None, :] >= c0[:, None]) & (kc[None, :] < c0[:, None] + NA_COLS)
    dc = kc[None, :] - c[:, None] + (NA_COLS - 1)
    n_dc_pad = 32
    onehot = (np.arange(n_dc_pad)[:, None, None] == dc[None]) & inwin[None]
    e2d = jnp.asarray(onehot.reshape(n_dc_pad, GRID_W * GRID_W).astype(np.float32))
    rpb2d = jnp.pad(rpb.reshape(L, H * n_dr, n_dc), ((0, 0), (0, 0), (0, n_dc_pad - n_dc)))
    t = pl.pallas_call(
        _bias_expand_kernel,
        grid=(L,),
        in_specs=[pl.BlockSpec((None, H * n_dr, n_dc_pad), lambda l: (l, 0, 0)),
                  pl.BlockSpec((n_dc_pad, GRID_W * GRID_W), lambda l: (0, 0))],
        out_specs=pl.BlockSpec((None, H * n_dr, GRID_W * GRID_W), lambda l: (l, 0, 0)),
        out_shape=jax.ShapeDtypeStruct((L, H * n_dr, GRID_W * GRID_W), F32),
        compiler_params=_cparams(1),
        name="bias_expand",
    )(rpb2d, e2d)
    t = t.reshape(L, H, n_dr, GRID_W, GRID_W)
    t = jnp.where(jnp.asarray(inwin)[None, None, None], t, NEG)
    t = t.transpose(0, 1, 3, 2, 4).reshape(L, H, GRID_W, n_dr * GRID_W)
    nk = NA_ROWS * GRID_W
    cases = [t[..., (NA_ROWS - 1 - d) * GRID_W:(NA_ROWS - 1 - d) * GRID_W + nk] for d in range(NA_ROWS)]
    return jnp.stack(cases, axis=1)


def kernel(x, c, ctx, c_ctx, norm_g, w_ada, b_ada, w_in, a_norm_g, a_w_s, a_b_s, b_q_g, b_k_g, b_sink,
           c_w_f, d_q_g, d_k_g, d_rpb, w_gate, w_branch, w_out):
    B, S, D = x.shape
    C = ctx.shape[1]
    L = norm_g.shape[0]
    assert S % 512 == 0 and S // GRID_W >= NA_ROWS and S >= 3 * B_BLK and C % CHUNK == 0
    tm = 512
    tm_ctx = math.gcd(B * C, 512)
    cn = 512
    tm_in = 512
    tq_b = 8 * B_BLK
    qt_b = B_BLK
    rps_d = 4
    W = BRANCH_W

    n_rows = -(-(B + 1) // 8) * 8
    cc = jnp.concatenate([c, c_ctx[None], jnp.zeros((n_rows - B - 1, D), F32)], axis=0)
    mod = _modulation(cc, w_ada, b_ada).reshape(L, n_rows, 1, 3 * D)
    lat_row = lambda b: b
    ctx_row = lambda b: B

    w_in_b = w_in.astype(BF16)
    w_gate_b, w_branch_b, w_out_b = w_gate.astype(BF16), w_branch.astype(BF16), w_out.astype(BF16)
    a_params = (a_norm_g, a_w_s.astype(BF16), jnp.swapaxes(a_b_s, 1, 2))
    gains = jnp.stack([jnp.tile(b_q_g, (1, W // B_HD)),
                       jnp.tile(b_k_g, (1, W // B_HD)),
                       jnp.tile(d_q_g, (1, W // D_HD)),
                       jnp.tile(d_k_g, (1, W // D_HD))] + [jnp.zeros((L, W), F32)] * 4, axis=1)
    bd64, bd32 = _block_diag_mean(W // 2, B_HD), _block_diag_mean(W // 2, D_HD)
    ropes = _rope_tables(S)
    c128, s128 = _dft_cos_sin(C_GW)
    tm_f, tm_fc = min(512, S // 2), min(512, C // 2)
    dft_lat, dft_ctx = _dft_half_matrix(S), _dft_half_matrix(C)
    jm_lat, jm_ctx = _rev_shift_matrix(tm_f), _rev_shift_matrix(tm_fc)
    assert tm_f == tm and tm_fc == C // 2
    bias_tabs = _na_bias_tables(d_rpb)

    h, hc = x, ctx.reshape(1, B * C, D)
    for l in range(L):
        last = l == L - 1
        n_lat, z, ua = _inproj(h, mod, lat_row, norm_g, w_in_b, gains, bd64, bd32, ropes, a_params, tm_in, True, l)
        n_ctx, zc, ua_ctx = _inproj(hc, mod, ctx_row, norm_g, w_in_b, gains, bd64, bd32, ropes, a_params,
                                    tm_ctx, False, l)
        zc = zc.reshape(B, C, Z_WIDTH)
        cw, sw = _dft_weights(c128, s128, c_w_f[l])

        u_lat = [
            ua,
            _branch_b(z, zc, b_sink[l], tq_b, qt_b),
            _fourier_branch(z, cw, sw, dft_lat, jm_lat, tm_f),
            _branch_d(z, zc, bias_tabs, rps_d, l),
        ]
        h = _merge(h, mod, lat_row, n_lat, u_lat, w_gate_b, w_branch_b, w_out_b, tm, cn, l)
        if not last:
            u_ctx = [
                ua_ctx,
                _ctx_b(zc, b_sink[l]),
                _fourier_branch(zc, cw, sw, dft_ctx, jm_ctx, tm_fc).reshape(B, C, W),
                _ctx_d(zc),
            ]
            u_ctx = [u.reshape(1, B * C, W) for u in u_ctx]
            hc = _merge(hc, mod, ctx_row, n_ctx, u_ctx, w_gate_b, w_branch_b, w_out_b, tm_ctx, cn, l)
    return h
```

```python
import functools
import math

import jax
import jax.numpy as jnp
import numpy as np
from jax import lax
from jax.experimental import pallas as pl
from jax.experimental.pallas import tpu as pltpu

F32 = jnp.float32
BF16 = jnp.bfloat16

LANES = 128
V7X_VMEM_LIMIT_BYTES = 56 * 1024 * 1024

GRID_W = 64
N_BRANCH = 4
BRANCH_W = 512
CHUNK = 128
A_GROUPS = 4
B_HD = 64
B_QH = 8
B_KVH = 2
B_WIN = 128
B_BLK = 128
C_GROUPS = 4
C_GW = 128
D_HD = 32
D_H = 16
NA_ROWS = 8
NA_COLS = 16
ROPE_BASE = 10000.0
EPS = 1e-6
NEG = -1e30
LOG2E = math.log2(math.e)

(COL_BQ, COL_BKV, COL_BZ, COL_FIN, COL_FZ, COL_DQ, COL_DK, COL_DV, COL_DZ) = range(9)
Z_WIDTH = 9 * BRANCH_W
W_IN_SRC = {COL_BQ: 1536, COL_BZ: 2304, COL_FIN: 2816, COL_FZ: 3328,
            COL_DQ: 3840, COL_DK: 4352, COL_DV: 4864, COL_DZ: 5376}
W_IN_BK = 2048
W_IN_AU, W_IN_AV, W_IN_AZ = 0, 512, 1024


def _cparams(n_grid, flags=None):
    return pltpu.CompilerParams(
        dimension_semantics=("arbitrary",) * n_grid,
        vmem_limit_bytes=V7X_VMEM_LIMIT_BYTES,
        flags=flags)


def _sigmoid(x):
    return 1.0 / (1.0 + jnp.exp(-x))


def _silu(x):
    return x * _sigmoid(x)


def _gelu(x):
    return 0.5 * x * (1.0 + jnp.tanh(math.sqrt(2.0 / math.pi) * (x + 0.044715 * (x * x * x))))


def _mod_kernel(c_ref, w_ref, b_ref, o_ref):
    c = c_ref[...]
    a = _silu(c).astype(BF16)
    o_ref[...] = jnp.dot(a, w_ref[...].astype(BF16), preferred_element_type=F32) + b_ref[...]


def _modulation(cc, w_ada, b_ada):
    L, D, N = w_ada.shape
    R = cc.shape[0]
    tn = 512
    return pl.pallas_call(
        _mod_kernel,
        grid=(L, N // tn),
        in_specs=[pl.BlockSpec((R, D), lambda l, j: (0, 0)),
                  pl.BlockSpec((None, D, tn), lambda l, j: (l, 0, j)),
                  pl.BlockSpec((None, 1, tn), lambda l, j: (l, 0, j))],
        out_specs=pl.BlockSpec((None, R, tn), lambda l, j: (l, 0, j)),
        out_shape=jax.ShapeDtypeStruct((L, R, N), F32),
        compiler_params=_cparams(2),
        name="modulation",
    )(cc, w_ada, b_ada.reshape(L, 1, N))


def _head_rms(x, bd_ref, width):
    sq = (x * x).astype(BF16)
    c = min(bd_ref.shape[0], width)
    bd = bd_ref[0:c, 0:c]
    parts = [jnp.dot(sq[:, o:o + c], bd, preferred_element_type=F32) for o in range(0, width, c)]
    return parts[0] if len(parts) == 1 else jnp.concatenate(parts, axis=1)


def _rope(x, cos, sina, sinb):
    w = x.shape[-1]
    reps = w // LANES
    cos = jnp.concatenate([cos] * reps, axis=1)
    sina = jnp.concatenate([sina] * reps, axis=1)
    sinb = jnp.concatenate([sinb] * reps, axis=1)
    up = pltpu.roll(x, w - 16, 1)
    dn = pltpu.roll(x, 16, 1)
    return x * cos + up * sina + dn * sinb


def _ada_norm(x, mod_ref, ng_ref):
    D = x.shape[-1]
    ms = jnp.mean(x * x, axis=-1, keepdims=True)
    y = x * lax.rsqrt(ms + EPS) * ng_ref[...]
    return (y * (1.0 + mod_ref[:, D:2 * D]) + mod_ref[:, 0:D]).astype(BF16)


def _dup_heads(t):
    swapped = pltpu.roll(t, B_HD, 1)
    low = lax.broadcasted_iota(jnp.int32, (1, LANES), 1) < B_HD
    return jnp.concatenate([jnp.where(low, t, swapped), jnp.where(low, swapped, t)], axis=1)


def _chunk_mlp(nb, w_ref, gv_ref, ws_ref, bs_ref, ua_ref):
    W = BRANCH_W
    proj = lambda src: jnp.dot(nb, w_ref[:, src:src + W], preferred_element_type=F32)
    v = _gelu(proj(W_IN_AV))
    ms = jnp.mean(v * v, axis=-1, keepdims=True)
    vb = (v * lax.rsqrt(ms + EPS) * gv_ref[...]).astype(BF16)
    gate = _gelu(proj(W_IN_AU)) * _silu(proj(W_IN_AZ))
    for c in range(nb.shape[0] // CHUNK):
        rows = slice(c * CHUNK, (c + 1) * CHUNK)
        for g in range(A_GROUPS):
            cols = slice(g * LANES, (g + 1) * LANES)
            mixed = jnp.dot(ws_ref[g], vb[rows, cols], preferred_element_type=F32) + bs_ref[:, g:g + 1]
            ua_ref[rows, cols] = (gate[rows, cols] * mixed).astype(BF16)


def _inproj_kernel(h_ref, mod_ref, ng_ref, w_ref, gains_ref, bd64_ref, bd32_ref,
                   cos_ref, sina_ref, sinb_ref, gv_ref, ws_ref, bs_ref, n_ref, z_ref, ua_ref, *, rope):
    W = BRANCH_W
    nb = _ada_norm(h_ref[...], mod_ref, ng_ref)
    n_ref[...] = nb
    _chunk_mlp(nb, w_ref, gv_ref, ws_ref, bs_ref, ua_ref)

    def maybe_rope(t):
        if rope:
            return _rope(t, cos_ref[...], sina_ref[...], sinb_ref[...])
        return t

    def head_norm(t, bd_ref, gain):
        return t * lax.rsqrt(_head_rms(t, bd_ref, t.shape[-1]) + EPS) * gain

    for cb in range(Z_WIDTH // W):
        if cb == COL_BKV:
            kvw = B_KVH * B_HD
            kv = jnp.dot(nb, w_ref[:, W_IN_BK:W_IN_BK + 2 * kvw], preferred_element_type=F32)
            k = maybe_rope(head_norm(kv[:, 0:kvw], bd64_ref, gains_ref[1:2, 0:kvw]))
            acc = jnp.concatenate([_dup_heads(k), _dup_heads(kv[:, kvw:])], axis=1)
        else:
            src = W_IN_SRC[cb]
            acc = jnp.dot(nb, w_ref[:, src:src + W], preferred_element_type=F32)
            if cb == COL_BQ:
                acc = maybe_rope(head_norm(acc, bd64_ref, gains_ref[0:1, :])) * (B_HD ** -0.5 * LOG2E)
            elif cb == COL_DQ:
                acc = head_norm(acc, bd32_ref, gains_ref[2:3, :]) * (D_HD ** -0.5 * LOG2E)
            elif cb == COL_DK:
                acc = head_norm(acc, bd32_ref, gains_ref[3:4, :])
        z_ref[:, cb * W:(cb + 1) * W] = acc.astype(BF16)


def _inproj(h, mod, mod_row, norm_g, w_in, gains, bd64, bd32, ropes, a_params, tm, rope, l):
    g_v, w_s, b_sT = a_params
    B, S, D = h.shape
    cos, sina, sinb = ropes
    layer = lambda shape: pl.BlockSpec((None,) + shape, lambda b, i: (l,) + (0,) * len(shape),
                                       pipeline_mode=pl.Buffered(1))
    const = lambda shape: pl.BlockSpec(shape, lambda b, i: (0,) * len(shape), pipeline_mode=pl.Buffered(1))
    rope_spec = pl.BlockSpec((tm, LANES), lambda b, i: (i if rope else 0, 0))
    return pl.pallas_call(
        functools.partial(_inproj_kernel, rope=rope),
        grid=(B, S // tm),
        in_specs=[pl.BlockSpec((None, tm, D), lambda b, i: (b, i, 0)),
                  pl.BlockSpec((None, None, 1, 3 * D), lambda b, i: (l, mod_row(b), 0, 0)),
                  layer((1, D)),
                  layer(w_in.shape[1:]),
                  layer((8, BRANCH_W)),
                  const(bd64.shape),
                  const(bd32.shape),
                  rope_spec, rope_spec, rope_spec,
                  layer((1, BRANCH_W)),
                  layer((A_GROUPS, CHUNK, CHUNK)),
                  layer((CHUNK, A_GROUPS))],
        out_specs=[pl.BlockSpec((None, tm, D), lambda b, i: (b, i, 0)),
                   pl.BlockSpec((None, tm, Z_WIDTH), lambda b, i: (b, i, 0)),
                   pl.BlockSpec((None, tm, BRANCH_W), lambda b, i: (b, i, 0))],
        out_shape=[jax.ShapeDtypeStruct((B, S, D), BF16),
                   jax.ShapeDtypeStruct((B, S, Z_WIDTH), BF16),
                   jax.ShapeDtypeStruct((B, S, BRANCH_W), BF16)],
        compiler_params=_cparams(2),
        name="inproj",
    )(h, mod, norm_g.reshape(norm_g.shape[0], 1, D), w_in, gains, bd64, bd32, cos, sina, sinb,
      g_v.reshape(g_v.shape[0], 1, BRANCH_W), w_s, b_sT)


def _dftw_kernel(c_ref, s_ref, w_ref, cw_ref, sw_ref):
    for g in range(C_GROUPS):
        w = w_ref[g]
        cw_ref[g] = jnp.dot(c_ref[...], w, preferred_element_type=F32,
                            precision=lax.Precision.HIGHEST).astype(BF16)
        sw_ref[g] = jnp.dot(s_ref[...], w, preferred_element_type=F32,
                            precision=lax.Precision.HIGHEST).astype(BF16)


def _dft_weights(c128, s128, w_f):
    shp = jax.ShapeDtypeStruct((C_GROUPS, C_GW, C_GW), BF16)
    return pl.pallas_call(_dftw_kernel, out_shape=[shp, shp], name="dft_weights")(c128, s128, w_f)


def _fourier_pq_kernel(xa_ref, xm_ref, xn_ref, xh_ref, jm_ref, cw_ref, sw_ref, pq_ref, aux_ref):
    i = pl.program_id(1)
    tm = xa_ref.shape[0]
    row = lax.broadcasted_iota(jnp.int32, (tm, 1), 0)
    first = jnp.where(i > 0, xn_ref[0:1, :].astype(F32), 0.0)
    xb = jnp.where(row == 0, first, jnp.dot(jm_ref[...], xm_ref[...], preferred_element_type=F32))
    xa = xa_ref[...].astype(F32)
    xe = (xa + xb).astype(BF16)
    xo = (xa - xb).astype(BF16)
    for g in range(C_GROUPS):
        cols = slice(g * LANES, (g + 1) * LANES)
        pq_ref[0, :, cols] = jnp.dot(xe[:, cols], cw_ref[g], preferred_element_type=F32).astype(BF16)
        pq_ref[1, :, cols] = jnp.dot(xo[:, cols], sw_ref[g], preferred_element_type=F32).astype(BF16)

    @pl.when(i == 0)
    def _():
        for g in range(C_GROUPS):
            cols = slice(g * LANES, (g + 1) * LANES)
            aux_ref[:, cols] = jnp.dot(xh_ref[:, cols], cw_ref[g], preferred_element_type=F32)


def _fourier_pq(z, jm, cw, sw, tm):
    B, L, _ = z.shape
    H = L // 2
    W = BRANCH_W
    nt = H // tm
    wspec = pl.BlockSpec((C_GROUPS, C_GW, C_GW), lambda b, i: (0, 0, 0))
    return pl.pallas_call(
        _fourier_pq_kernel,
        grid=(B, nt),
        in_specs=[pl.BlockSpec((None, tm, W), lambda b, i: (b, i, COL_FIN)),
                  pl.BlockSpec((None, tm, W), lambda b, i: (b, 2 * nt - 1 - i, COL_FIN)),
                  pl.BlockSpec((None, 8, W),
                               lambda b, i: (b, jnp.minimum((L - i * tm) // 8, L // 8 - 1), COL_FIN)),
                  pl.BlockSpec((None, 8, W), lambda b, i: (b, H // 8, COL_FIN)),
                  pl.BlockSpec((tm, tm), lambda b, i: (0, 0)),
                  wspec, wspec],
        out_specs=[pl.BlockSpec((None, 2, tm, W), lambda b, i: (b, 0, i, 0)),
                   pl.BlockSpec((None, 8, W), lambda b, i: (b, 0, 0))],
        out_shape=[jax.ShapeDtypeStruct((B, 2, H, W), BF16),
                   jax.ShapeDtypeStruct((B, 8, W), F32)],
        compiler_params=_cparams(2),
        name="fourier_pq",
    )(z, z, z, z, jm, cw, sw)


def _fourier_mix_kernel(dft_ref, dftx_ref, pq_ref, aux_ref, jm_ref, fza_ref, fzb_ref, u_ref, *, scale, half):
    tm = dft_ref.shape[0]
    pe = pq_ref[0:half, :]
    qo = pq_ref[half:, :]
    ev = jnp.dot(dft_ref[:, 0:half], pe, preferred_element_type=F32)
    od = jnp.dot(dft_ref[:, half:], qo, preferred_element_type=F32)
    row = lax.broadcasted_iota(jnp.int32, (tm, 1), 0)
    nyq = (1 - 2 * (row & 1)).astype(F32) * aux_ref[0:1, :]
    y1 = ev + od + nyq
    y2 = ev - od + nyq
    evx = jnp.dot(dftx_ref[:, 0:half], pe, preferred_element_type=F32)
    odx = jnp.dot(dftx_ref[:, half:], qo, preferred_element_type=F32)
    y2x = evx[0:1, :] - odx[0:1, :] + aux_ref[0:1, :]
    hi = y2.astype(BF16)
    lo = (y2 - hi.astype(F32)).astype(BF16)
    y2r = (jnp.dot(jm_ref[...], hi, preferred_element_type=F32)
           + jnp.dot(jm_ref[...], lo, preferred_element_type=F32))
    y2r = jnp.where(row == 0, y2x, y2r)
    u_ref[0] = (y1 * scale * _silu(fza_ref[...].astype(F32))).astype(BF16)
    u_ref[1] = (y2r * scale * _silu(fzb_ref[...].astype(F32))).astype(BF16)


def _fourier_mix(dft, pq, aux, jm, z, tm):
    B, L, _ = z.shape
    H = L // 2
    W = BRANCH_W
    nt = H // tm
    scale = 1.0 / math.sqrt(L * C_GW)
    return pl.pallas_call(
        functools.partial(_fourier_mix_kernel, scale=scale, half=H),
        grid=(B, nt),
        in_specs=[pl.BlockSpec((tm, L), lambda b, i: (i, 0)),
                  pl.BlockSpec((8, L), lambda b, i: ((i + 1) * (tm // 8), 0)),
                  pl.BlockSpec((None, L, W), lambda b, i: (b, 0, 0)),
                  pl.BlockSpec((None, 8, W), lambda b, i: (b, 0, 0)),
                  pl.BlockSpec((tm, tm), lambda b, i: (0, 0)),
                  pl.BlockSpec((None, tm, W), lambda b, i: (b, i, COL_FZ)),
                  pl.BlockSpec((None, tm, W), lambda b, i: (b, 2 * nt - 1 - i, COL_FZ))],
        out_specs=pl.BlockSpec((None, 2, tm, W), lambda b, i: (b, 0, i, 0)),
        out_shape=jax.ShapeDtypeStruct((B, 2, H, W), BF16),
        compiler_params=_cparams(2),
        name="fourier_mix",
    )(dft, dft, pq, aux, jm, z, z)


def _fourier_branch(z, cw, sw, dft, jm, tm):
    B, L, _ = z.shape
    pq, aux = _fourier_pq(z, jm, cw, sw, tm)
    return _fourier_mix(dft, pq.reshape(B, L, BRANCH_W), aux, jm, z, tm)


def _lane_masks(heads, hd):
    lane = lax.broadcasted_iota(jnp.int32, (1, LANES), 1)
    return [((lane >= g * hd) & (lane < (g + 1) * hd)) for g in heads]


def _attend(q, segs, sink_col, heads, hd):
    tq = q.shape[0]
    masks = _lane_masks(heads, hd)
    zero = jnp.zeros_like(q)
    qx = jnp.concatenate([jnp.where(m, q, zero) for m in masks], axis=0)
    scores = []
    for k, _, bias in segs:
        s = lax.dot_general(qx, k, (((1,), (1,)), ((), ())), preferred_element_type=F32)
        if bias is not None:
            s = s + bias
        scores.append(s)
    m = scores[0].max(axis=-1, keepdims=True)
    for s in scores[1:]:
        m = jnp.maximum(m, s.max(axis=-1, keepdims=True))
    if sink_col is not None:
        m = jnp.maximum(m, sink_col)
    den = None
    acc = None
    for s, (_, v, _) in zip(scores, segs):
        p = jnp.exp2(s - m)
        d = p.sum(axis=-1, keepdims=True)
        o = jnp.dot(p.astype(BF16), v, preferred_element_type=F32)
        den = d if den is None else den + d
        acc = o if acc is None else acc + o
    if sink_col is not None:
        den = den + jnp.exp2(sink_col - m)
    acc = acc / den
    out = None
    for g, msk in enumerate(masks):
        part = jnp.where(msk, acc[g * tq:(g + 1) * tq], 0.0)
        out = part if out is None else out + part
    return out


def _attend_blocks(q_lists, local_list, ctx_kv, sink_cols, heads, hd):
    tq = q_lists[0][0].shape[0]
    nsub = len(q_lists[0])
    masks = _lane_masks(heads, hd)
    rows = len(masks) * tq
    nt = (((1,), (1,)), ((), ()))

    def expand(q):
        zero = jnp.zeros_like(q)
        return jnp.concatenate([jnp.where(m, q, zero) for m in masks], axis=0)

    def cat(parts):
        return parts[0] if len(parts) == 1 else jnp.concatenate(parts, axis=0)

    def fold(x, op):
        return functools.reduce(op, [x[:, o:o + LANES] for o in range(0, x.shape[1], LANES)])

    qx = [cat([expand(q) for q in ql]) for ql in q_lists]
    kc, vc = ctx_kv
    s_c = lax.dot_general(cat(qx), kc, nt, preferred_element_type=F32)
    p_c_list, den_list, o_list = [], [], []
    for j, (qx_j, (k, v, bias)) in enumerate(zip(qx, local_list)):
        s_all = lax.dot_general(qx_j, k, nt, preferred_element_type=F32)
        p_tiles = []
        for t in range(nsub):
            idx = j * nsub + t
            s = s_all[t * rows:(t + 1) * rows] + bias
            s_cj = s_c[idx * rows:(idx + 1) * rows]
            m = jnp.maximum(fold(s, jnp.maximum), fold(s_cj, jnp.maximum)).max(axis=-1, keepdims=True)
            if sink_cols is not None:
                m = jnp.maximum(m, sink_cols[t])
            p = jnp.exp2(s - m)
            p_c = jnp.exp2(s_cj - m)
            den = (fold(p, jnp.add) + fold(p_c, jnp.add)).sum(axis=-1, keepdims=True)
            if sink_cols is not None:
                den = den + jnp.exp2(sink_cols[t] - m)
            p_tiles.append(p.astype(BF16))
            p_c_list.append(p_c.astype(BF16))
            den_list.append(den)
        o_list.append(jnp.dot(cat(p_tiles), v, preferred_element_type=F32))
    o_c = jnp.dot(cat(p_c_list), vc, preferred_element_type=F32)
    outs = []
    for j in range(len(q_lists)):
        outs_j = []
        for t in range(nsub):
            idx = j * nsub + t
            acc = (o_list[j][t * rows:(t + 1) * rows] + o_c[idx * rows:(idx + 1) * rows]) / den_list[idx]
            out = None
            for g, msk in enumerate(masks):
                part = jnp.where(msk, acc[g * tq:(g + 1) * tq], 0.0)
                out = part if out is None else out + part
            outs_j.append(out)
        outs.append(outs_j)
    return outs


def _branch_b_kernel(sink_ref, q_ref, kv_ref, ckv_ref, bz_ref, o_ref, *, seq, qt):
    nl = qt + 2 * B_WIN
    half = BRANCH_W // 2
    n_blk = q_ref.shape[0] // qt
    starts, biases = [], []
    for blk in range(n_blk):
        i = pl.program_id(1) * n_blk + blk
        start = pl.multiple_of(jnp.clip(i * qt - B_WIN, 0, seq - nl), B_BLK)
        qpos = i * qt + lax.broadcasted_iota(jnp.int32, (qt, nl), 0)
        kpos = start + lax.broadcasted_iota(jnp.int32, (qt, nl), 1)
        bias1 = jnp.where(jnp.abs(kpos - qpos) <= B_WIN, 0.0, NEG).astype(F32)
        starts.append(start)
        biases.append(jnp.concatenate([bias1, bias1], axis=0))
    for kvh in range(B_KVH):
        kc = slice(kvh * LANES, (kvh + 1) * LANES)
        vc = slice(half + kvh * LANES, half + (kvh + 1) * LANES)
        grps = (2 * kvh, 2 * kvh + 1)
        sink_cols = [jnp.concatenate([jnp.full((qt, 1), sink_ref[2 * grp + g] * LOG2E, F32) for g in range(2)],
                                     axis=0) for grp in grps]
        q_lists = [[q_ref[blk * qt:(blk + 1) * qt, grp * LANES:(grp + 1) * LANES] for grp in grps]
                   for blk in range(n_blk)]
        local = [(kv_ref[pl.ds(s, nl), kc], kv_ref[pl.ds(s, nl), vc], b) for s, b in zip(starts, biases)]
        outs = _attend_blocks(q_lists, local, (ckv_ref[:, kc], ckv_ref[:, vc]), sink_cols, range(2), B_HD)
        for blk, outs_blk in enumerate(outs):
            rows = slice(blk * qt, (blk + 1) * qt)
            for grp, o in zip(grps, outs_blk):
                cols = slice(grp * LANES, (grp + 1) * LANES)
                o_ref[rows, cols] = (o * _silu(bz_ref[rows, cols].astype(F32))).astype(BF16)


def _branch_b(z, zc, sink, tq, qt):
    B, S, _ = z.shape
    C = zc.shape[1]
    W = BRANCH_W
    assert tq % qt == 0 and qt % B_BLK == 0 and S >= qt + 2 * B_WIN
    return pl.pallas_call(
        functools.partial(_branch_b_kernel, seq=S, qt=qt),
        grid=(B, S // tq),
        in_specs=[pl.BlockSpec(memory_space=pltpu.SMEM),
                  pl.BlockSpec((None, tq, W), lambda b, i: (b, i, COL_BQ)),
                  pl.BlockSpec((None, S, W), lambda b, i: (b, 0, COL_BKV)),
                  pl.BlockSpec((None, C, W), lambda b, i: (b, 0, COL_BKV)),
                  pl.BlockSpec((None, tq, W), lambda b, i: (b, i, COL_BZ))],
        out_specs=pl.BlockSpec((None, tq, W), lambda b, i: (b, i, 0)),
        out_shape=jax.ShapeDtypeStruct((B, S, W), BF16),
        compiler_params=_cparams(2),
        name="branch_b",
    )(sink, z, z, zc, z)


def _ctx_b_kernel(sink_ref, q_ref, kv_ref, bz_ref, o_ref):
    tq = q_ref.shape[0]
    half = BRANCH_W // 2
    for grp in range(BRANCH_W // LANES):
        kvh = grp // 2
        kc = slice(kvh * LANES, (kvh + 1) * LANES)
        vc = slice(half + kvh * LANES, half + (kvh + 1) * LANES)
        cols = slice(grp * LANES, (grp + 1) * LANES)
        sink_col = jnp.concatenate(
            [jnp.full((tq, 1), sink_ref[2 * grp + g] * LOG2E, F32) for g in range(2)], axis=0)
        o = _attend(q_ref[:, cols], [(kv_ref[:, kc], kv_ref[:, vc], None)], sink_col, range(2), B_HD)
        o_ref[:, cols] = (o * _silu(bz_ref[:, cols].astype(F32))).astype(BF16)


def _ctx_b(zc, sink):
    B, C, _ = zc.shape
    W = BRANCH_W
    col = lambda cb: pl.BlockSpec((None, C, W), lambda b: (b, 0, cb))
    return pl.pallas_call(
        _ctx_b_kernel,
        grid=(B,),
        in_specs=[pl.BlockSpec(memory_space=pltpu.SMEM), col(COL_BQ), col(COL_BKV), col(COL_BZ)],
        out_specs=pl.BlockSpec((None, C, W), lambda b: (b, 0, 0)),
        out_shape=jax.ShapeDtypeStruct((B, C, W), BF16),
        compiler_params=_cparams(1),
        name="ctx_b",
    )(sink, zc, zc, zc)


def _na_row_start(r, rows):
    return jnp.clip(r - NA_ROWS // 2, 0, rows - NA_ROWS)


def _branch_d_kernel(q_ref, k_ref, v_ref, ck_ref, cv_ref, dz_ref, *rest, rows):
    bias_refs, o_ref = rest[:-1], rest[-1]
    nk = NA_ROWS * GRID_W
    hpg = LANES // D_HD
    n_rows = len(bias_refs)
    starts = [pl.multiple_of(_na_row_start(pl.program_id(1) * n_rows + rr, rows) * GRID_W, GRID_W)
              for rr in range(n_rows)]
    for grp in range(BRANCH_W // LANES):
        cols = slice(grp * LANES, (grp + 1) * LANES)
        q_lists = [[q_ref[rr * GRID_W:(rr + 1) * GRID_W, cols]] for rr in range(n_rows)]
        local = [(k_ref[pl.ds(s, nk), cols], v_ref[pl.ds(s, nk), cols],
                  b[grp * hpg:(grp + 1) * hpg].reshape(hpg * GRID_W, nk)) for s, b in zip(starts, bias_refs)]
        outs = _attend_blocks(q_lists, local, (ck_ref[:, cols], cv_ref[:, cols]), None, range(hpg), D_HD)
        for rr, (o,) in enumerate(outs):
            qrows = slice(rr * GRID_W, (rr + 1) * GRID_W)
            o_ref[qrows, cols] = (o * _silu(dz_ref[qrows, cols].astype(F32))).astype(BF16)


def _branch_d(z, zc, bias_tabs, rps, l):
    B, S, _ = z.shape
    C = zc.shape[1]
    W = BRANCH_W
    rows = S // GRID_W
    nk = NA_ROWS * GRID_W
    tile = lambda cb: pl.BlockSpec((None, rps * GRID_W, W), lambda b, i: (b, i, cb))
    full = lambda cb: pl.BlockSpec((None, S, W), lambda b, i: (b, 0, cb))
    ctx = lambda cb: pl.BlockSpec((None, C, W), lambda b, i: (b, 0, cb))

    def bias_spec(rr):
        def index(b, i):
            r = i * rps + rr
            return (l, r - _na_row_start(r, rows), 0, 0, 0)
        return pl.BlockSpec((None, None, D_H, GRID_W, nk), index)

    return pl.pallas_call(
        functools.partial(_branch_d_kernel, rows=rows),
        grid=(B, rows // rps),
        in_specs=[tile(COL_DQ), full(COL_DK), full(COL_DV), ctx(COL_DK), ctx(COL_DV), tile(COL_DZ)]
                 + [bias_spec(rr) for rr in range(rps)],
        out_specs=pl.BlockSpec((None, rps * GRID_W, W), lambda b, i: (b, i, 0)),
        out_shape=jax.ShapeDtypeStruct((B, S, W), BF16),
        compiler_params=_cparams(2),
        name="branch_d",
    )(z, z, z, zc, zc, z, *([bias_tabs] * rps))


def _ctx_d_kernel(q_ref, k_ref, v_ref, dz_ref, o_ref):
    hpg = LANES // D_HD
    for grp in range(BRANCH_W // LANES):
        cols = slice(grp * LANES, (grp + 1) * LANES)
        o = _attend(q_ref[:, cols], [(k_ref[:, cols], v_ref[:, cols], None)], None, range(hpg), D_HD)
        o_ref[:, cols] = (o * _silu(dz_ref[:, cols].astype(F32))).astype(BF16)


def _ctx_d(zc):
    B, C, _ = zc.shape
    W = BRANCH_W
    col = lambda cb: pl.BlockSpec((None, C, W), lambda b: (b, 0, cb))
    return pl.pallas_call(
        _ctx_d_kernel,
        grid=(B,),
        in_specs=[col(COL_DQ), col(COL_DK), col(COL_DV), col(COL_DZ)],
        out_specs=pl.BlockSpec((None, C, W), lambda b: (b, 0, 0)),
        out_shape=jax.ShapeDtypeStruct((B, C, W), BF16),
        compiler_params=_cparams(1),
        name="ctx_d",
    )(zc, zc, zc, zc)


def _gate_mix_kernel(n_ref, ua_ref, ub_ref, uc_ref, ud_ref, wg_ref, wb_ref, o_ref):
    n = n_ref[...]
    mix = None
    for b, u_ref in enumerate((ua_ref, ub_ref, uc_ref, ud_ref)):
        gate = _sigmoid(jnp.dot(n, wg_ref[b], preferred_element_type=F32))
        t = gate * jnp.dot(u_ref[...], wb_ref[b], preferred_element_type=F32)
        mix = t if mix is None else mix + t
    o_ref[...] = mix.astype(BF16)


def _gate_mix(n, us, wg, wb, tm, cn, l):
    B, S, D = n.shape
    W = BRANCH_W
    row = lambda width: pl.BlockSpec((None, tm, width), lambda j, b, i: (b, i, 0))
    uc_spec = row(W)
    if us[2].ndim == 4:
        nt = S // (2 * tm)
        uc_spec = pl.BlockSpec((None, None, tm, W),
                               lambda j, b, i: (b, i // nt, jnp.where(i < nt, i, 2 * nt - 1 - i), 0))
    return pl.pallas_call(
        _gate_mix_kernel,
        grid=(D // cn, B, S // tm),
        in_specs=[row(D), row(W), row(W), uc_spec, row(W),
                  pl.BlockSpec((None, N_BRANCH, D, cn), lambda j, b, i: (l, 0, 0, j)),
                  pl.BlockSpec((None, N_BRANCH, W, cn), lambda j, b, i: (l, 0, 0, j))],
        out_specs=pl.BlockSpec((None, tm, cn), lambda j, b, i: (b, i, j)),
        out_shape=jax.ShapeDtypeStruct((B, S, D), BF16),
        compiler_params=_cparams(3),
        name="gate_mix",
    )(n, *us, wg, wb)


def _out_proj_kernel(h_ref, mod_ref, mix_ref, wo_ref, o_ref):
    D = h_ref.shape[-1]
    y = jnp.dot(mix_ref[...], wo_ref[...], preferred_element_type=F32)
    o_ref[...] = h_ref[...] + mod_ref[:, 2 * D:3 * D] * y


def _out_proj(h, mod, mod_row, mix, wo, tm, l):
    B, S, D = h.shape
    row = lambda: pl.BlockSpec((None, tm, D), lambda b, i: (b, i, 0))
    return pl.pallas_call(
        _out_proj_kernel,
        grid=(B, S // tm),
        in_specs=[row(),
                  pl.BlockSpec((None, None, 1, 3 * D), lambda b, i: (l, mod_row(b), 0, 0)),
                  row(),
                  pl.BlockSpec((None, D, D), lambda b, i: (l, 0, 0), pipeline_mode=pl.Buffered(1))],
        out_specs=row(),
        out_shape=jax.ShapeDtypeStruct((B, S, D), F32),
        compiler_params=_cparams(2),
        name="out_proj",
    )(h, mod, mix, wo)


def _merge(h, mod, mod_row, n, us, wg, wb, wo, tm, tm_out, cn, l):
    return _out_proj(h, mod, mod_row, _gate_mix(n, us, wg, wb, tm, cn, l), wo, tm_out, l)


def _rope_tables(S):
    t = np.arange(S)
    pos = np.stack([t // GRID_W, t % GRID_W], axis=1).astype(np.float32)
    lane = np.arange(B_HD)
    which = lane // (B_HD // 2)
    fi = lane % (B_HD // 4)
    inv = (ROPE_BASE ** (-(fi.astype(np.float32)) / (B_HD // 4))).astype(np.float32)
    ang = pos[:, which] * inv[None, :]
    cos, sin = np.cos(ang), np.sin(ang)
    lower = (lane % (B_HD // 2)) < (B_HD // 4)
    sina = np.where(lower[None, :], -sin, 0.0)
    sinb = np.where(lower[None, :], 0.0, sin)
    rep = lambda a: jnp.asarray(np.tile(a.astype(np.float32), (1, LANES // B_HD)))
    return rep(cos), rep(sina), rep(sinb)


def _block_diag_mean(width, hd):
    i = np.arange(width)
    return jnp.asarray(((i[:, None] // hd) == (i[None, :] // hd)).astype(np.float32) / hd, dtype=BF16)


def _dft_cos_sin(n):
    k = jnp.arange(n, dtype=jnp.int32)
    m = (k[:, None] * k[None, :]) % n
    ang = m.astype(F32) * (2.0 * math.pi / n)
    return jnp.cos(ang), jnp.sin(ang)


def _dft_half_matrix(n):
    k = jnp.arange(n // 2 + 8, dtype=jnp.int32)
    m = jnp.arange(n // 2, dtype=jnp.int32)
    ang = ((k[:, None] * m[None, :]) % n).astype(F32) * (2.0 * math.pi / n)
    return jnp.concatenate([jnp.cos(ang), -jnp.sin(ang)], axis=1).astype(BF16)


def _rev_shift_matrix(tm):
    r = np.arange(tm)
    return jnp.asarray(((r[:, None] >= 1) & (r[None, :] == tm - r[:, None])).astype(np.float32), dtype=BF16)


def _bias_expand_kernel(rpb_ref, e_ref, o_ref):
    o_ref[...] = LOG2E * jnp.dot(rpb_ref[...], e_ref[...], preferred_element_type=F32,
                                 precision=lax.Precision.HIGHEST)


def _na_bias_tables(rpb):
    L, H, n_dr, n_dc = rpb.shape
    c = np.arange(GRID_W)
    c0 = np.clip(c - NA_COLS // 2, 0, GRID_W - NA_COLS)
    kc = np.arange(GRID_W)
    inwin = (kc[None, :] >= c0[:, None]) & (kc[None, :] < c0[:, None] + NA_COLS)
    dc = kc[None, :] - c[:, None] + (NA_COLS - 1)
    n_dc_pad = 32
    onehot = (np.arange(n_dc_pad)[:, None, None] == dc[None]) & inwin[None]
    e2d = jnp.asarray(onehot.reshape(n_dc_pad, GRID_W * GRID_W).astype(np.float32))
    rpb2d = jnp.pad(rpb.reshape(L, H * n_dr, n_dc), ((0, 0), (0, 0), (0, n_dc_pad - n_dc)))
    t = pl.pallas_call(
        _bias_expand_kernel,
        grid=(L,),
        in_specs=[pl.BlockSpec((None, H * n_dr, n_dc_pad), lambda l: (l, 0, 0)),
                  pl.BlockSpec((n_dc_pad, GRID_W * GRID_W), lambda l: (0, 0))],
        out_specs=pl.BlockSpec((None, H * n_dr, GRID_W * GRID_W), lambda l: (l, 0, 0)),
        out_shape=jax.ShapeDtypeStruct((L, H * n_dr, GRID_W * GRID_W), F32),
        compiler_params=_cparams(1),
        name="bias_expand",
    )(rpb2d, e2d)
    t = t.reshape(L, H, n_dr, GRID_W, GRID_W)
    t = jnp.where(jnp.asarray(inwin)[None, None, None], t, NEG)
    t = t.transpose(0, 1, 3, 2, 4).reshape(L, H, GRID_W, n_dr * GRID_W)
    nk = NA_ROWS * GRID_W
    cases = [t[..., (NA_ROWS - 1 - d) * GRID_W:(NA_ROWS - 1 - d) * GRID_W + nk] for d in range(NA_ROWS)]
    return jnp.stack(cases, axis=1)


def kernel(x, c, ctx, c_ctx, norm_g, w_ada, b_ada, w_in, a_norm_g, a_w_s, a_b_s, b_q_g, b_k_g, b_sink,
           c_w_f, d_q_g, d_k_g, d_rpb, w_gate, w_branch, w_out):
    B, S, D = x.shape
    C = ctx.shape[1]
    L = norm_g.shape[0]
    assert S % 512 == 0 and S // GRID_W >= NA_ROWS and S >= 3 * B_BLK and C % CHUNK == 0
    tm = 512
    tm_ctx = math.gcd(B * C, 512)
    cn = 512
    tm_out = 2 * tm if S % (2 * tm) == 0 else tm
    tm_in = 512
    tq_b = 8 * B_BLK
    qt_b = B_BLK
    rps_d = 4
    W = BRANCH_W

    n_rows = -(-(B + 1) // 8) * 8
    cc = jnp.concatenate([c, c_ctx[None], jnp.zeros((n_rows - B - 1, D), F32)], axis=0)
    mod = _modulation(cc, w_ada, b_ada).reshape(L, n_rows, 1, 3 * D)
    lat_row = lambda b: b
    ctx_row = lambda b: B

    w_in_b = w_in.astype(BF16)
    w_gate_b, w_branch_b, w_out_b = w_gate.astype(BF16), w_branch.astype(BF16), w_out.astype(BF16)
    a_params = (a_norm_g, a_w_s.astype(BF16), jnp.swapaxes(a_b_s, 1, 2))
    gains = jnp.stack([jnp.tile(b_q_g, (1, W // B_HD)),
                       jnp.tile(b_k_g, (1, W // B_HD)),
                       jnp.tile(d_q_g, (1, W // D_HD)),
                       jnp.tile(d_k_g, (1, W // D_HD))] + [jnp.zeros((L, W), F32)] * 4, axis=1)
    bd64, bd32 = _block_diag_mean(W // 2, B_HD), _block_diag_mean(W // 2, D_HD)
    ropes = _rope_tables(S)
    c128, s128 = _dft_cos_sin(C_GW)
    tm_f, tm_fc = min(512, S // 2), min(512, C // 2)
    dft_lat, dft_ctx = _dft_half_matrix(S), _dft_half_matrix(C)
    jm_lat, jm_ctx = _rev_shift_matrix(tm_f), _rev_shift_matrix(tm_fc)
    assert tm_f == tm and tm_fc == C // 2
    bias_tabs = _na_bias_tables(d_rpb)

    h, hc = x, ctx.reshape(1, B * C, D)
    for l in range(L):
        last = l == L - 1
        n_lat, z, ua = _inproj(h, mod, lat_row, norm_g, w_in_b, gains, bd64, bd32, ropes, a_params, tm_in, True, l)
        n_ctx, zc, ua_ctx = _inproj(hc, mod, ctx_row, norm_g, w_in_b, gains, bd64, bd32, ropes, a_params,
                                    tm_ctx, False, l)
        zc = zc.reshape(B, C, Z_WIDTH)
        cw, sw = _dft_weights(c128, s128, c_w_f[l])

        u_lat = [
            ua,
            _branch_b(z, zc, b_sink[l], tq_b, qt_b),
            _fourier_branch(z, cw, sw, dft_lat, jm_lat, tm_f),
            _branch_d(z, zc, bias_tabs, rps_d, l),
        ]
        h = _merge(h, mod, lat_row, n_lat, u_lat, w_gate_b, w_branch_b, w_out_b, tm, tm_out, cn, l)
        if not last:
            u_ctx = [
                ua_ctx,
                _ctx_b(zc, b_sink[l]),
                _fourier_branch(zc, cw, sw, dft_ctx, jm_ctx, tm_fc).reshape(B, C, W),
                _ctx_d(zc),
            ]
            u_ctx = [u.reshape(1, B * C, W) for u in u_ctx]
            hc = _merge(hc, mod, ctx_row, n_ctx, u_ctx, w_gate_b, w_branch_b, w_out_b, tm_ctx, tm_ctx, cn, l)
    return h
```

```python
import functools
import math

import jax
import jax.numpy as jnp
import numpy as np
from jax import lax
from jax.experimental import pallas as pl
from jax.experimental.pallas import tpu as pltpu

F32 = jnp.float32
BF16 = jnp.bfloat16

LANES = 128
V7X_VMEM_LIMIT_BYTES = 56 * 1024 * 1024

GRID_W = 64
N_BRANCH = 4
BRANCH_W = 512
CHUNK = 128
A_GROUPS = 4
B_HD = 64
B_QH = 8
B_KVH = 2
B_WIN = 128
B_BLK = 128
C_GROUPS = 4
C_GW = 128
D_HD = 32
D_H = 16
NA_ROWS = 8
NA_COLS = 16
ROPE_BASE = 10000.0
EPS = 1e-6
NEG = -1e30
LOG2E = math.log2(math.e)

(COL_BQ, COL_BKV, COL_BZ, COL_FIN, COL_FZ, COL_DQ, COL_DK, COL_DV, COL_DZ) = range(9)
Z_WIDTH = 9 * BRANCH_W
W_IN_SRC = {COL_BQ: 1536, COL_BZ: 2304, COL_FIN: 2816, COL_FZ: 3328,
            COL_DQ: 3840, COL_DK: 4352, COL_DV: 4864, COL_DZ: 5376}
W_IN_BK = 2048
W_IN_AU, W_IN_AV, W_IN_AZ = 0, 512, 1024


def _cparams(n_grid, flags=None):
    return pltpu.CompilerParams(
        dimension_semantics=("arbitrary",) * n_grid,
        vmem_limit_bytes=V7X_VMEM_LIMIT_BYTES,
        flags=flags)


def _sigmoid(x):
    return 1.0 / (1.0 + jnp.exp(-x))


def _silu(x):
    return x * _sigmoid(x)


def _gelu(x):
    return 0.5 * x * (1.0 + jnp.tanh(math.sqrt(2.0 / math.pi) * (x + 0.044715 * (x * x * x))))


def _mod_kernel(c_ref, w_ref, b_ref, o_ref):
    c = c_ref[...]
    a = _silu(c).astype(BF16)
    o_ref[...] = jnp.dot(a, w_ref[...].astype(BF16), preferred_element_type=F32) + b_ref[...]


def _modulation(cc, w_ada, b_ada):
    L, D, N = w_ada.shape
    R = cc.shape[0]
    tn = 512
    return pl.pallas_call(
        _mod_kernel,
        grid=(L, N // tn),
        in_specs=[pl.BlockSpec((R, D), lambda l, j: (0, 0)),
                  pl.BlockSpec((None, D, tn), lambda l, j: (l, 0, j)),
                  pl.BlockSpec((None, 1, tn), lambda l, j: (l, 0, j))],
        out_specs=pl.BlockSpec((None, R, tn), lambda l, j: (l, 0, j)),
        out_shape=jax.ShapeDtypeStruct((L, R, N), F32),
        compiler_params=_cparams(2),
        name="modulation",
    )(cc, w_ada, b_ada.reshape(L, 1, N))


def _head_rms(x, bd_ref, width):
    sq = (x * x).astype(BF16)
    c = min(bd_ref.shape[0], width)
    bd = bd_ref[0:c, 0:c]
    parts = [jnp.dot(sq[:, o:o + c], bd, preferred_element_type=F32) for o in range(0, width, c)]
    return parts[0] if len(parts) == 1 else jnp.concatenate(parts, axis=1)


def _rope(x, cos, sina, sinb):
    w = x.shape[-1]
    reps = w // LANES
    cos = jnp.concatenate([cos] * reps, axis=1)
    sina = jnp.concatenate([sina] * reps, axis=1)
    sinb = jnp.concatenate([sinb] * reps, axis=1)
    up = pltpu.roll(x, w - 16, 1)
    dn = pltpu.roll(x, 16, 1)
    return x * cos + up * sina + dn * sinb


def _ada_norm(x, mod_ref, ng_ref):
    D = x.shape[-1]
    ms = jnp.mean(x * x, axis=-1, keepdims=True)
    y = x * lax.rsqrt(ms + EPS) * ng_ref[...]
    return (y * (1.0 + mod_ref[:, D:2 * D]) + mod_ref[:, 0:D]).astype(BF16)


def _dup_heads(t):
    swapped = pltpu.roll(t, B_HD, 1)
    low = lax.broadcasted_iota(jnp.int32, (1, LANES), 1) < B_HD
    return jnp.concatenate([jnp.where(low, t, swapped), jnp.where(low, swapped, t)], axis=1)


def _chunk_mlp(nb, w_ref, gv_ref, ws_ref, bs_ref, ua_ref):
    W = BRANCH_W
    proj = lambda src: jnp.dot(nb, w_ref[:, src:src + W], preferred_element_type=F32)
    v = _gelu(proj(W_IN_AV))
    ms = jnp.mean(v * v, axis=-1, keepdims=True)
    vb = (v * lax.rsqrt(ms + EPS) * gv_ref[...]).astype(BF16)
    gate = _gelu(proj(W_IN_AU)) * _silu(proj(W_IN_AZ))
    for c in range(nb.shape[0] // CHUNK):
        rows = slice(c * CHUNK, (c + 1) * CHUNK)
        for g in range(A_GROUPS):
            cols = slice(g * LANES, (g + 1) * LANES)
            mixed = jnp.dot(ws_ref[g], vb[rows, cols], preferred_element_type=F32) + bs_ref[:, g:g + 1]
            ua_ref[rows, cols] = (gate[rows, cols] * mixed).astype(BF16)


def _inproj_kernel(h_ref, mod_ref, ng_ref, w_ref, gains_ref, bd64_ref, bd32_ref,
                   cos_ref, sina_ref, sinb_ref, gv_ref, ws_ref, bs_ref, n_ref, z_ref, ua_ref, *, rope):
    W = BRANCH_W
    nb = _ada_norm(h_ref[...], mod_ref, ng_ref)
    n_ref[...] = nb
    _chunk_mlp(nb, w_ref, gv_ref, ws_ref, bs_ref, ua_ref)

    def maybe_rope(t):
        if rope:
            return _rope(t, cos_ref[...], sina_ref[...], sinb_ref[...])
        return t

    def head_norm(t, bd_ref, gain):
        return t * lax.rsqrt(_head_rms(t, bd_ref, t.shape[-1]) + EPS) * gain

    for cb in range(Z_WIDTH // W):
        if cb == COL_BKV:
            kvw = B_KVH * B_HD
            kv = jnp.dot(nb, w_ref[:, W_IN_BK:W_IN_BK + 2 * kvw], preferred_element_type=F32)
            k = maybe_rope(head_norm(kv[:, 0:kvw], bd64_ref, gains_ref[1:2, 0:kvw]))
            acc = jnp.concatenate([_dup_heads(k), _dup_heads(kv[:, kvw:])], axis=1)
        else:
            src = W_IN_SRC[cb]
            acc = jnp.dot(nb, w_ref[:, src:src + W], preferred_element_type=F32)
            if cb == COL_BQ:
                acc = maybe_rope(head_norm(acc, bd64_ref, gains_ref[0:1, :])) * (B_HD ** -0.5 * LOG2E)
            elif cb == COL_DQ:
                acc = head_norm(acc, bd32_ref, gains_ref[2:3, :]) * (D_HD ** -0.5 * LOG2E)
            elif cb == COL_DK:
                acc = head_norm(acc, bd32_ref, gains_ref[3:4, :])
        z_ref[:, cb * W:(cb + 1) * W] = acc.astype(BF16)


def _inproj(h, mod, mod_row, norm_g, w_in, gains, bd64, bd32, ropes, a_params, tm, rope, l):
    g_v, w_s, b_sT = a_params
    B, S, D = h.shape
    cos, sina, sinb = ropes
    layer = lambda shape: pl.BlockSpec((None,) + shape, lambda b, i: (l,) + (0,) * len(shape),
                                       pipeline_mode=pl.Buffered(1))
    const = lambda shape: pl.BlockSpec(shape, lambda b, i: (0,) * len(shape), pipeline_mode=pl.Buffered(1))
    rope_spec = pl.BlockSpec((tm, LANES), lambda b, i: (i if rope else 0, 0))
    return pl.pallas_call(
        functools.partial(_inproj_kernel, rope=rope),
        grid=(B, S // tm),
        in_specs=[pl.BlockSpec((None, tm, D), lambda b, i: (b, i, 0)),
                  pl.BlockSpec((None, None, 1, 3 * D), lambda b, i: (l, mod_row(b), 0, 0)),
                  layer((1, D)),
                  layer(w_in.shape[1:]),
                  layer((8, BRANCH_W)),
                  const(bd64.shape),
                  const(bd32.shape),
                  rope_spec, rope_spec, rope_spec,
                  layer((1, BRANCH_W)),
                  layer((A_GROUPS, CHUNK, CHUNK)),
                  layer((CHUNK, A_GROUPS))],
        out_specs=[pl.BlockSpec((None, tm, D), lambda b, i: (b, i, 0)),
                   pl.BlockSpec((None, tm, Z_WIDTH), lambda b, i: (b, i, 0)),
                   pl.BlockSpec((None, tm, BRANCH_W), lambda b, i: (b, i, 0))],
        out_shape=[jax.ShapeDtypeStruct((B, S, D), BF16),
                   jax.ShapeDtypeStruct((B, S, Z_WIDTH), BF16),
                   jax.ShapeDtypeStruct((B, S, BRANCH_W), BF16)],
        compiler_params=_cparams(2),
        name="inproj",
    )(h, mod, norm_g.reshape(norm_g.shape[0], 1, D), w_in, gains, bd64, bd32, cos, sina, sinb,
      g_v.reshape(g_v.shape[0], 1, BRANCH_W), w_s, b_sT)


def _dftw_kernel(c_ref, s_ref, w_ref, cw_ref, sw_ref):
    for g in range(C_GROUPS):
        w = w_ref[g]
        cw_ref[g] = jnp.dot(c_ref[...], w, preferred_element_type=F32,
                            precision=lax.Precision.HIGHEST).astype(BF16)
        sw_ref[g] = jnp.dot(s_ref[...], w, preferred_element_type=F32,
                            precision=lax.Precision.HIGHEST).astype(BF16)


def _dft_weights(c128, s128, w_f):
    shp = jax.ShapeDtypeStruct((C_GROUPS, C_GW, C_GW), BF16)
    return pl.pallas_call(_dftw_kernel, out_shape=[shp, shp], name="dft_weights")(c128, s128, w_f)


def _fourier_pq_kernel(xa_ref, xm_ref, xn_ref, xh_ref, jm_ref, cw_ref, sw_ref, pq_ref, aux_ref):
    i = pl.program_id(1)
    tm = xa_ref.shape[0]
    row = lax.broadcasted_iota(jnp.int32, (tm, 1), 0)
    first = jnp.where(i > 0, xn_ref[0:1, :].astype(F32), 0.0)
    xb = jnp.where(row == 0, first, jnp.dot(jm_ref[...], xm_ref[...], preferred_element_type=F32))
    xa = xa_ref[...].astype(F32)
    xe = (xa + xb).astype(BF16)
    xo = (xa - xb).astype(BF16)
    for g in range(C_GROUPS):
        cols = slice(g * LANES, (g + 1) * LANES)
        pq_ref[0, :, cols] = jnp.dot(xe[:, cols], cw_ref[g], preferred_element_type=F32).astype(BF16)
        pq_ref[1, :, cols] = jnp.dot(xo[:, cols], sw_ref[g], preferred_element_type=F32).astype(BF16)

    @pl.when(i == 0)
    def _():
        for g in range(C_GROUPS):
            cols = slice(g * LANES, (g + 1) * LANES)
            aux_ref[:, cols] = jnp.dot(xh_ref[:, cols], cw_ref[g], preferred_element_type=F32)


def _fourier_pq(z, jm, cw, sw, tm):
    B, L, _ = z.shape
    H = L // 2
    W = BRANCH_W
    nt = H // tm
    wspec = pl.BlockSpec((C_GROUPS, C_GW, C_GW), lambda b, i: (0, 0, 0))
    return pl.pallas_call(
        _fourier_pq_kernel,
        grid=(B, nt),
        in_specs=[pl.BlockSpec((None, tm, W), lambda b, i: (b, i, COL_FIN)),
                  pl.BlockSpec((None, tm, W), lambda b, i: (b, 2 * nt - 1 - i, COL_FIN)),
                  pl.BlockSpec((None, 8, W),
                               lambda b, i: (b, jnp.minimum((L - i * tm) // 8, L // 8 - 1), COL_FIN)),
                  pl.BlockSpec((None, 8, W), lambda b, i: (b, H // 8, COL_FIN)),
                  pl.BlockSpec((tm, tm), lambda b, i: (0, 0)),
                  wspec, wspec],
        out_specs=[pl.BlockSpec((None, 2, tm, W), lambda b, i: (b, 0, i, 0)),
                   pl.BlockSpec((None, 8, W), lambda b, i: (b, 0, 0))],
        out_shape=[jax.ShapeDtypeStruct((B, 2, H, W), BF16),
                   jax.ShapeDtypeStruct((B, 8, W), F32)],
        compiler_params=_cparams(2),
        name="fourier_pq",
    )(z, z, z, z, jm, cw, sw)


def _fourier_mix_kernel(dft_ref, dftx_ref, pq_ref, aux_ref, jm_ref, fza_ref, fzb_ref, u_ref, *, scale, half):
    tm = dft_ref.shape[0]
    pe = pq_ref[0:half, :]
    qo = pq_ref[half:, :]
    ev = jnp.dot(dft_ref[:, 0:half], pe, preferred_element_type=F32)
    od = jnp.dot(dft_ref[:, half:], qo, preferred_element_type=F32)
    row = lax.broadcasted_iota(jnp.int32, (tm, 1), 0)
    nyq = (1 - 2 * (row & 1)).astype(F32) * aux_ref[0:1, :]
    y1 = ev + od + nyq
    y2 = ev - od + nyq
    evx = jnp.dot(dftx_ref[:, 0:half], pe, preferred_element_type=F32)
    odx = jnp.dot(dftx_ref[:, half:], qo, preferred_element_type=F32)
    y2x = evx[0:1, :] - odx[0:1, :] + aux_ref[0:1, :]
    hi = y2.astype(BF16)
    lo = (y2 - hi.astype(F32)).astype(BF16)
    y2r = (jnp.dot(jm_ref[...], hi, preferred_element_type=F32)
           + jnp.dot(jm_ref[...], lo, preferred_element_type=F32))
    y2r = jnp.where(row == 0, y2x, y2r)
    u_ref[0] = (y1 * scale * _silu(fza_ref[...].astype(F32))).astype(BF16)
    u_ref[1] = (y2r * scale * _silu(fzb_ref[...].astype(F32))).astype(BF16)


def _fourier_mix(dft, pq, aux, jm, z, tm):
    B, L, _ = z.shape
    H = L // 2
    W = BRANCH_W
    nt = H // tm
    scale = 1.0 / math.sqrt(L * C_GW)
    return pl.pallas_call(
        functools.partial(_fourier_mix_kernel, scale=scale, half=H),
        grid=(B, nt),
        in_specs=[pl.BlockSpec((tm, L), lambda b, i: (i, 0)),
                  pl.BlockSpec((8, L), lambda b, i: ((i + 1) * (tm // 8), 0)),
                  pl.BlockSpec((None, L, W), lambda b, i: (b, 0, 0)),
                  pl.BlockSpec((None, 8, W), lambda b, i: (b, 0, 0)),
                  pl.BlockSpec((tm, tm), lambda b, i: (0, 0)),
                  pl.BlockSpec((None, tm, W), lambda b, i: (b, i, COL_FZ)),
                  pl.BlockSpec((None, tm, W), lambda b, i: (b, 2 * nt - 1 - i, COL_FZ))],
        out_specs=pl.BlockSpec((None, 2, tm, W), lambda b, i: (b, 0, i, 0)),
        out_shape=jax.ShapeDtypeStruct((B, 2, H, W), BF16),
        compiler_params=_cparams(2),
        name="fourier_mix",
    )(dft, dft, pq, aux, jm, z, z)


def _fourier_branch(z, cw, sw, dft, jm, tm):
    B, L, _ = z.shape
    pq, aux = _fourier_pq(z, jm, cw, sw, tm)
    return _fourier_mix(dft, pq.reshape(B, L, BRANCH_W), aux, jm, z, tm)


def _lane_masks(heads, hd):
    lane = lax.broadcasted_iota(jnp.int32, (1, LANES), 1)
    return [((lane >= g * hd) & (lane < (g + 1) * hd)) for g in heads]


def _attend(q, segs, sink_col, heads, hd):
    tq = q.shape[0]
    masks = _lane_masks(heads, hd)
    zero = jnp.zeros_like(q)
    qx = jnp.concatenate([jnp.where(m, q, zero) for m in masks], axis=0)
    scores = []
    for k, _, bias in segs:
        s = lax.dot_general(qx, k, (((1,), (1,)), ((), ())), preferred_element_type=F32)
        if bias is not None:
            s = s + bias
        scores.append(s)
    m = scores[0].max(axis=-1, keepdims=True)
    for s in scores[1:]:
        m = jnp.maximum(m, s.max(axis=-1, keepdims=True))
    if sink_col is not None:
        m = jnp.maximum(m, sink_col)
    den = None
    acc = None
    for s, (_, v, _) in zip(scores, segs):
        p = jnp.exp2(s - m)
        d = p.sum(axis=-1, keepdims=True)
        o = jnp.dot(p.astype(BF16), v, preferred_element_type=F32)
        den = d if den is None else den + d
        acc = o if acc is None else acc + o
    if sink_col is not None:
        den = den + jnp.exp2(sink_col - m)
    acc = acc / den
    out = None
    for g, msk in enumerate(masks):
        part = jnp.where(msk, acc[g * tq:(g + 1) * tq], 0.0)
        out = part if out is None else out + part
    return out


def _attend_blocks(q_lists, local_list, ctx_kv, sink_cols, heads, hd):
    tq = q_lists[0][0].shape[0]
    nsub = len(q_lists[0])
    masks = _lane_masks(heads, hd)
    rows = len(masks) * tq
    nt = (((1,), (1,)), ((), ()))

    def expand(q):
        zero = jnp.zeros_like(q)
        return jnp.concatenate([jnp.where(m, q, zero) for m in masks], axis=0)

    def cat(parts):
        return parts[0] if len(parts) == 1 else jnp.concatenate(parts, axis=0)

    def fold(x, op):
        return functools.reduce(op, [x[:, o:o + LANES] for o in range(0, x.shape[1], LANES)])

    qx = [cat([expand(q) for q in ql]) for ql in q_lists]
    kc, vc = ctx_kv
    s_c = lax.dot_general(cat(qx), kc, nt, preferred_element_type=F32)
    p_c_list, den_list, o_list = [], [], []
    for j, (qx_j, (k, v, bias)) in enumerate(zip(qx, local_list)):
        s_all = lax.dot_general(qx_j, k, nt, preferred_element_type=F32)
        p_tiles = []
        for t in range(nsub):
            idx = j * nsub + t
            s = s_all[t * rows:(t + 1) * rows] + bias
            s_cj = s_c[idx * rows:(idx + 1) * rows]
            m = jnp.maximum(fold(s, jnp.maximum), fold(s_cj, jnp.maximum)).max(axis=-1, keepdims=True)
            if sink_cols is not None:
                m = jnp.maximum(m, sink_cols[t])
            p = jnp.exp2(s - m)
            p_c = jnp.exp2(s_cj - m)
            den = (fold(p, jnp.add) + fold(p_c, jnp.add)).sum(axis=-1, keepdims=True)
            if sink_cols is not None:
                den = den + jnp.exp2(sink_cols[t] - m)
            p_tiles.append(p.astype(BF16))
            p_c_list.append(p_c.astype(BF16))
            den_list.append(den)
        o_list.append(jnp.dot(cat(p_tiles), v, preferred_element_type=F32))
    o_c = jnp.dot(cat(p_c_list), vc, preferred_element_type=F32)
    outs = []
    for j in range(len(q_lists)):
        outs_j = []
        for t in range(nsub):
            idx = j * nsub + t
            acc = (o_list[j][t * rows:(t + 1) * rows] + o_c[idx * rows:(idx + 1) * rows]) / den_list[idx]
            out = None
            for g, msk in enumerate(masks):
                part = jnp.where(msk, acc[g * tq:(g + 1) * tq], 0.0)
                out = part if out is None else out + part
            outs_j.append(out)
        outs.append(outs_j)
    return outs


def _branch_b_kernel(sink_ref, q_ref, kv_ref, ckv_ref, bz_ref, o_ref, *, seq, qt):
    nl = qt + 2 * B_WIN
    half = BRANCH_W // 2
    n_blk = q_ref.shape[0] // qt
    starts, biases = [], []
    for blk in range(n_blk):
        i = pl.program_id(1) * n_blk + blk
        start = pl.multiple_of(jnp.clip(i * qt - B_WIN, 0, seq - nl), B_BLK)
        qpos = i * qt + lax.broadcasted_iota(jnp.int32, (qt, nl), 0)
        kpos = start + lax.broadcasted_iota(jnp.int32, (qt, nl), 1)
        bias1 = jnp.where(jnp.abs(kpos - qpos) <= B_WIN, 0.0, NEG).astype(F32)
        starts.append(start)
        biases.append(jnp.concatenate([bias1, bias1], axis=0))
    for kvh in range(B_KVH):
        kc = slice(kvh * LANES, (kvh + 1) * LANES)
        vc = slice(half + kvh * LANES, half + (kvh + 1) * LANES)
        grps = (2 * kvh, 2 * kvh + 1)
        sink_cols = [jnp.concatenate([jnp.full((qt, 1), sink_ref[2 * grp + g] * LOG2E, F32) for g in range(2)],
                                     axis=0) for grp in grps]
        q_lists = [[q_ref[blk * qt:(blk + 1) * qt, grp * LANES:(grp + 1) * LANES] for grp in grps]
                   for blk in range(n_blk)]
        local = [(kv_ref[pl.ds(s, nl), kc], kv_ref[pl.ds(s, nl), vc], b) for s, b in zip(starts, biases)]
        outs = _attend_blocks(q_lists, local, (ckv_ref[:, kc], ckv_ref[:, vc]), sink_cols, range(2), B_HD)
        for blk, outs_blk in enumerate(outs):
            rows = slice(blk * qt, (blk + 1) * qt)
            for grp, o in zip(grps, outs_blk):
                cols = slice(grp * LANES, (grp + 1) * LANES)
                o_ref[rows, cols] = (o * _silu(bz_ref[rows, cols].astype(F32))).astype(BF16)


def _branch_b(z, zc, sink, tq, qt):
    B, S, _ = z.shape
    C = zc.shape[1]
    W = BRANCH_W
    assert tq % qt == 0 and qt % B_BLK == 0 and S >= qt + 2 * B_WIN
    return pl.pallas_call(
        functools.partial(_branch_b_kernel, seq=S, qt=qt),
        grid=(B, S // tq),
        in_specs=[pl.BlockSpec(memory_space=pltpu.SMEM),
                  pl.BlockSpec((None, tq, W), lambda b, i: (b, i, COL_BQ)),
                  pl.BlockSpec((None, S, W), lambda b, i: (b, 0, COL_BKV)),
                  pl.BlockSpec((None, C, W), lambda b, i: (b, 0, COL_BKV)),
                  pl.BlockSpec((None, tq, W), lambda b, i: (b, i, COL_BZ))],
        out_specs=pl.BlockSpec((None, tq, W), lambda b, i: (b, i, 0)),
        out_shape=jax.ShapeDtypeStruct((B, S, W), BF16),
        compiler_params=_cparams(2),
        name="branch_b",
    )(sink, z, z, zc, z)


def _ctx_b_kernel(sink_ref, q_ref, kv_ref, bz_ref, o_ref):
    tq = q_ref.shape[0]
    half = BRANCH_W // 2
    for grp in range(BRANCH_W // LANES):
        kvh = grp // 2
        kc = slice(kvh * LANES, (kvh + 1) * LANES)
        vc = slice(half + kvh * LANES, half + (kvh + 1) * LANES)
        cols = slice(grp * LANES, (grp + 1) * LANES)
        sink_col = jnp.concatenate(
            [jnp.full((tq, 1), sink_ref[2 * grp + g] * LOG2E, F32) for g in range(2)], axis=0)
        o = _attend(q_ref[:, cols], [(kv_ref[:, kc], kv_ref[:, vc], None)], sink_col, range(2), B_HD)
        o_ref[:, cols] = (o * _silu(bz_ref[:, cols].astype(F32))).astype(BF16)


def _ctx_b(zc, sink):
    B, C, _ = zc.shape
    W = BRANCH_W
    col = lambda cb: pl.BlockSpec((None, C, W), lambda b: (b, 0, cb))
    return pl.pallas_call(
        _ctx_b_kernel,
        grid=(B,),
        in_specs=[pl.BlockSpec(memory_space=pltpu.SMEM), col(COL_BQ), col(COL_BKV), col(COL_BZ)],
        out_specs=pl.BlockSpec((None, C, W), lambda b: (b, 0, 0)),
        out_shape=jax.ShapeDtypeStruct((B, C, W), BF16),
        compiler_params=_cparams(1),
        name="ctx_b",
    )(sink, zc, zc, zc)


def _na_row_start(r, rows):
    return jnp.clip(r - NA_ROWS // 2, 0, rows - NA_ROWS)


def _branch_d_kernel(q_ref, k_ref, v_ref, ck_ref, cv_ref, dz_ref, *rest, rows):
    bias_refs, o_ref = rest[:-1], rest[-1]
    nk = NA_ROWS * GRID_W
    hpg = LANES // D_HD
    n_rows = len(bias_refs)
    starts = [pl.multiple_of(_na_row_start(pl.program_id(1) * n_rows + rr, rows) * GRID_W, GRID_W)
              for rr in range(n_rows)]
    for grp in range(BRANCH_W // LANES):
        cols = slice(grp * LANES, (grp + 1) * LANES)
        q_lists = [[q_ref[rr * GRID_W:(rr + 1) * GRID_W, cols]] for rr in range(n_rows)]
        local = [(k_ref[pl.ds(s, nk), cols], v_ref[pl.ds(s, nk), cols],
                  b[grp * hpg:(grp + 1) * hpg].reshape(hpg * GRID_W, nk)) for s, b in zip(starts, bias_refs)]
        outs = _attend_blocks(q_lists, local, (ck_ref[:, cols], cv_ref[:, cols]), None, range(hpg), D_HD)
        for rr, (o,) in enumerate(outs):
            qrows = slice(rr * GRID_W, (rr + 1) * GRID_W)
            o_ref[qrows, cols] = (o * _silu(dz_ref[qrows, cols].astype(F32))).astype(BF16)


def _branch_d(z, zc, bias_tabs, rps, l):
    B, S, _ = z.shape
    C = zc.shape[1]
    W = BRANCH_W
    rows = S // GRID_W
    nk = NA_ROWS * GRID_W
    tile = lambda cb: pl.BlockSpec((None, rps * GRID_W, W), lambda b, i: (b, i, cb))
    full = lambda cb: pl.BlockSpec((None, S, W), lambda b, i: (b, 0, cb))
    ctx = lambda cb: pl.BlockSpec((None, C, W), lambda b, i: (b, 0, cb))

    def bias_spec(rr):
        def index(b, i):
            r = i * rps + rr
            return (l, r - _na_row_start(r, rows), 0, 0, 0)
        return pl.BlockSpec((None, None, D_H, GRID_W, nk), index)

    return pl.pallas_call(
        functools.partial(_branch_d_kernel, rows=rows),
        grid=(B, rows // rps),
        in_specs=[tile(COL_DQ), full(COL_DK), full(COL_DV), ctx(COL_DK), ctx(COL_DV), tile(COL_DZ)]
                 + [bias_spec(rr) for rr in range(rps)],
        out_specs=pl.BlockSpec((None, rps * GRID_W, W), lambda b, i: (b, i, 0)),
        out_shape=jax.ShapeDtypeStruct((B, S, W), BF16),
        compiler_params=_cparams(2),
        name="branch_d",
    )(z, z, z, zc, zc, z, *([bias_tabs] * rps))


def _branch_bd_kernel(sink_ref, bq_ref, bkv_ref, bckv_ref, bz_ref, dq_ref, dk_ref, dv_ref, dck_ref, dcv_ref,
                      dz_ref, *rest, seq, qt, rows):
    bias_refs, ob_ref, od_ref = rest[:-2], rest[-2], rest[-1]
    _branch_b_kernel(sink_ref, bq_ref, bkv_ref, bckv_ref, bz_ref, ob_ref, seq=seq, qt=qt)
    _branch_d_kernel(dq_ref, dk_ref, dv_ref, dck_ref, dcv_ref, dz_ref, *bias_refs, od_ref, rows=rows)


def _branch_bd(z, zc, sink, bias_tabs, rps, qt, l):
    B, S, _ = z.shape
    C = zc.shape[1]
    W = BRANCH_W
    rows = S // GRID_W
    nk = NA_ROWS * GRID_W
    tq = rps * GRID_W
    assert tq % qt == 0 and qt % B_BLK == 0 and S >= qt + 2 * B_WIN
    tile = lambda cb: pl.BlockSpec((None, tq, W), lambda b, i: (b, i, cb))
    full = lambda cb: pl.BlockSpec((None, S, W), lambda b, i: (b, 0, cb), pipeline_mode=pl.Buffered(1))
    ctx = lambda cb: pl.BlockSpec((None, C, W), lambda b, i: (b, 0, cb), pipeline_mode=pl.Buffered(1))

    def bias_spec(rr):
        def index(b, i):
            r = i * rps + rr
            return (l, r - _na_row_start(r, rows), 0, 0, 0)
        return pl.BlockSpec((None, None, D_H, GRID_W, nk), index)

    out = jax.ShapeDtypeStruct((B, S, W), BF16)
    out_spec = lambda: pl.BlockSpec((None, tq, W), lambda b, i: (b, i, 0))
    return pl.pallas_call(
        functools.partial(_branch_bd_kernel, seq=S, qt=qt, rows=rows),
        grid=(B, rows // rps),
        in_specs=[pl.BlockSpec(memory_space=pltpu.SMEM),
                  tile(COL_BQ), full(COL_BKV), ctx(COL_BKV), tile(COL_BZ),
                  tile(COL_DQ), full(COL_DK), full(COL_DV), ctx(COL_DK), ctx(COL_DV), tile(COL_DZ)]
                 + [bias_spec(rr) for rr in range(rps)],
        out_specs=[out_spec(), out_spec()],
        out_shape=[out, out],
        compiler_params=_cparams(2),
        name="branch_bd",
    )(sink, z, z, zc, z, z, z, z, zc, zc, z, *([bias_tabs] * rps))


def _ctx_d_kernel(q_ref, k_ref, v_ref, dz_ref, o_ref):
    hpg = LANES // D_HD
    for grp in range(BRANCH_W // LANES):
        cols = slice(grp * LANES, (grp + 1) * LANES)
        o = _attend(q_ref[:, cols], [(k_ref[:, cols], v_ref[:, cols], None)], None, range(hpg), D_HD)
        o_ref[:, cols] = (o * _silu(dz_ref[:, cols].astype(F32))).astype(BF16)


def _ctx_d(zc):
    B, C, _ = zc.shape
    W = BRANCH_W
    col = lambda cb: pl.BlockSpec((None, C, W), lambda b: (b, 0, cb))
    return pl.pallas_call(
        _ctx_d_kernel,
        grid=(B,),
        in_specs=[col(COL_DQ), col(COL_DK), col(COL_DV), col(COL_DZ)],
        out_specs=pl.BlockSpec((None, C, W), lambda b: (b, 0, 0)),
        out_shape=jax.ShapeDtypeStruct((B, C, W), BF16),
        compiler_params=_cparams(1),
        name="ctx_d",
    )(zc, zc, zc, zc)


def _gate_mix_kernel(n_ref, ua_ref, ub_ref, uc_ref, ud_ref, wg_ref, wb_ref, o_ref):
    n = n_ref[...]
    mix = None
    for b, u_ref in enumerate((ua_ref, ub_ref, uc_ref, ud_ref)):
        gate = _sigmoid(jnp.dot(n, wg_ref[b], preferred_element_type=F32))
        t = gate * jnp.dot(u_ref[...], wb_ref[b], preferred_element_type=F32)
        mix = t if mix is None else mix + t
    o_ref[...] = mix.astype(BF16)


def _gate_mix(n, us, wg, wb, tm, cn, l):
    B, S, D = n.shape
    W = BRANCH_W
    row = lambda width: pl.BlockSpec((None, tm, width), lambda j, b, i: (b, i, 0))
    uc_spec = row(W)
    if us[2].ndim == 4:
        nt = S // (2 * tm)
        uc_spec = pl.BlockSpec((None, None, tm, W),
                               lambda j, b, i: (b, i // nt, jnp.where(i < nt, i, 2 * nt - 1 - i), 0))
    return pl.pallas_call(
        _gate_mix_kernel,
        grid=(D // cn, B, S // tm),
        in_specs=[row(D), row(W), row(W), uc_spec, row(W),
                  pl.BlockSpec((None, N_BRANCH, D, cn), lambda j, b, i: (l, 0, 0, j)),
                  pl.BlockSpec((None, N_BRANCH, W, cn), lambda j, b, i: (l, 0, 0, j))],
        out_specs=pl.BlockSpec((None, tm, cn), lambda j, b, i: (b, i, j)),
        out_shape=jax.ShapeDtypeStruct((B, S, D), BF16),
        compiler_params=_cparams(3),
        name="gate_mix",
    )(n, *us, wg, wb)


def _out_proj_kernel(h_ref, mod_ref, mix_ref, wo_ref, o_ref):
    D = h_ref.shape[-1]
    y = jnp.dot(mix_ref[...], wo_ref[...], preferred_element_type=F32)
    o_ref[...] = h_ref[...] + mod_ref[:, 2 * D:3 * D] * y


def _out_proj(h, mod, mod_row, mix, wo, tm, l):
    B, S, D = h.shape
    row = lambda: pl.BlockSpec((None, tm, D), lambda b, i: (b, i, 0))
    return pl.pallas_call(
        _out_proj_kernel,
        grid=(B, S // tm),
        in_specs=[row(),
                  pl.BlockSpec((None, None, 1, 3 * D), lambda b, i: (l, mod_row(b), 0, 0)),
                  row(),
                  pl.BlockSpec((None, D, D), lambda b, i: (l, 0, 0), pipeline_mode=pl.Buffered(1))],
        out_specs=row(),
        out_shape=jax.ShapeDtypeStruct((B, S, D), F32),
        compiler_params=_cparams(2),
        name="out_proj",
    )(h, mod, mix, wo)


def _merge(h, mod, mod_row, n, us, wg, wb, wo, tm, tm_out, cn, l):
    return _out_proj(h, mod, mod_row, _gate_mix(n, us, wg, wb, tm, cn, l), wo, tm_out, l)


def _rope_tables(S):
    t = np.arange(S)
    pos = np.stack([t // GRID_W, t % GRID_W], axis=1).astype(np.float32)
    lane = np.arange(B_HD)
    which = lane // (B_HD // 2)
    fi = lane % (B_HD // 4)
    inv = (ROPE_BASE ** (-(fi.astype(np.float32)) / (B_HD // 4))).astype(np.float32)
    ang = pos[:, which] * inv[None, :]
    cos, sin = np.cos(ang), np.sin(ang)
    lower = (lane % (B_HD // 2)) < (B_HD // 4)
    sina = np.where(lower[None, :], -sin, 0.0)
    sinb = np.where(lower[None, :], 0.0, sin)
    rep = lambda a: jnp.asarray(np.tile(a.astype(np.float32), (1, LANES // B_HD)))
    return rep(cos), rep(sina), rep(sinb)


def _block_diag_mean(width, hd):
    i = np.arange(width)
    return jnp.asarray(((i[:, None] // hd) == (i[None, :] // hd)).astype(np.float32) / hd, dtype=BF16)


def _dft_cos_sin(n):
    k = jnp.arange(n, dtype=jnp.int32)
    m = (k[:, None] * k[None, :]) % n
    ang = m.astype(F32) * (2.0 * math.pi / n)
    return jnp.cos(ang), jnp.sin(ang)


def _dft_half_matrix(n):
    k = jnp.arange(n // 2 + 8, dtype=jnp.int32)
    m = jnp.arange(n // 2, dtype=jnp.int32)
    ang = ((k[:, None] * m[None, :]) % n).astype(F32) * (2.0 * math.pi / n)
    return jnp.concatenate([jnp.cos(ang), -jnp.sin(ang)], axis=1).astype(BF16)


def _rev_shift_matrix(tm):
    r = np.arange(tm)
    return jnp.asarray(((r[:, None] >= 1) & (r[None, :] == tm - r[:, None])).astype(np.float32), dtype=BF16)


def _bias_expand_kernel(rpb_ref, e_ref, o_ref):
    o_ref[...] = LOG2E * jnp.dot(rpb_ref[...], e_ref[...], preferred_element_type=F32,
                                 precision=lax.Precision.HIGHEST)


def _na_bias_tables(rpb):
    L, H, n_dr, n_dc = rpb.shape
    c = np.arange(GRID_W)
    c0 = np.clip(c - NA_COLS // 2, 0, GRID_W - NA_COLS)
    kc = np.arange(GRID_W)
    inwin = (kc[None, :] >= c0[:, None]) & (kc[None, :] < c0[:, None] + NA_COLS)
    dc = kc[None, :] - c[:, None] + (NA_COLS - 1)
    n_dc_pad = 32
    onehot = (np.arange(n_dc_pad)[:, None, None] == dc[None]) & inwin[None]
    e2d = jnp.asarray(onehot.reshape(n_dc_pad, GRID_W * GRID_W).astype(np.float32))
    rpb2d = jnp.pad(rpb.reshape(L, H * n_dr, n_dc), ((0, 0), (0, 0), (0, n_dc_pad - n_dc)))
    t = pl.pallas_call(
        _bias_expand_kernel,
        grid=(L,),
        in_specs=[pl.BlockSpec((None, H * n_dr, n_dc_pad), lambda l: (l, 0, 0)),
                  pl.BlockSpec((n_dc_pad, GRID_W * GRID_W), lambda l: (0, 0))],
        out_specs=pl.BlockSpec((None, H * n_dr, GRID_W * GRID_W), lambda l: (l, 0, 0)),
        out_shape=jax.ShapeDtypeStruct((L, H * n_dr, GRID_W * GRID_W), F32),
        compiler_params=_cparams(1),
        name="bias_expand",
    )(rpb2d, e2d)
    t = t.reshape(L, H, n_dr, GRID_W, GRID_W)
    t = jnp.where(jnp.asarray(inwin)[None, None, None], t, NEG)
    t = t.transpose(0, 1, 3, 2, 4).reshape(L, H, GRID_W, n_dr * GRID_W)
    nk = NA_ROWS * GRID_W
    cases = [t[..., (NA_ROWS - 1 - d) * GRID_W:(NA_ROWS - 1 - d) * GRID_W + nk] for d in range(NA_ROWS)]
    return jnp.stack(cases, axis=1)


def kernel(x, c, ctx, c_ctx, norm_g, w_ada, b_ada, w_in, a_norm_g, a_w_s, a_b_s, b_q_g, b_k_g, b_sink,
           c_w_f, d_q_g, d_k_g, d_rpb, w_gate, w_branch, w_out):
    B, S, D = x.shape
    C = ctx.shape[1]
    L = norm_g.shape[0]
    assert S % 512 == 0 and S // GRID_W >= NA_ROWS and S >= 3 * B_BLK and C % CHUNK == 0
    tm = 512
    tm_ctx = math.gcd(B * C, 512)
    cn = 512
    tm_out = 2 * tm if S % (2 * tm) == 0 else tm
    tm_in = 512
    tq_b = 8 * B_BLK
    qt_b = B_BLK
    rps_d = 4
    W = BRANCH_W

    n_rows = -(-(B + 1) // 8) * 8
    cc = jnp.concatenate([c, c_ctx[None], jnp.zeros((n_rows - B - 1, D), F32)], axis=0)
    mod = _modulation(cc, w_ada, b_ada).reshape(L, n_rows, 1, 3 * D)
    lat_row = lambda b: b
    ctx_row = lambda b: B

    w_in_b = w_in.astype(BF16)
    w_gate_b, w_branch_b, w_out_b = w_gate.astype(BF16), w_branch.astype(BF16), w_out.astype(BF16)
    a_params = (a_norm_g, a_w_s.astype(BF16), jnp.swapaxes(a_b_s, 1, 2))
    gains = jnp.stack([jnp.tile(b_q_g, (1, W // B_HD)),
                       jnp.tile(b_k_g, (1, W // B_HD)),
                       jnp.tile(d_q_g, (1, W // D_HD)),
                       jnp.tile(d_k_g, (1, W // D_HD))] + [jnp.zeros((L, W), F32)] * 4, axis=1)
    bd64, bd32 = _block_diag_mean(W // 2, B_HD), _block_diag_mean(W // 2, D_HD)
    ropes = _rope_tables(S)
    c128, s128 = _dft_cos_sin(C_GW)
    tm_f, tm_fc = min(512, S // 2), min(512, C // 2)
    dft_lat, dft_ctx = _dft_half_matrix(S), _dft_half_matrix(C)
    jm_lat, jm_ctx = _rev_shift_matrix(tm_f), _rev_shift_matrix(tm_fc)
    assert tm_f == tm and tm_fc == C // 2
    bias_tabs = _na_bias_tables(d_rpb)

    h, hc = x, ctx.reshape(1, B * C, D)
    for l in range(L):
        last = l == L - 1
        n_lat, z, ua = _inproj(h, mod, lat_row, norm_g, w_in_b, gains, bd64, bd32, ropes, a_params, tm_in, True, l)
        n_ctx, zc, ua_ctx = _inproj(hc, mod, ctx_row, norm_g, w_in_b, gains, bd64, bd32, ropes, a_params,
                                    tm_ctx, False, l)
        zc = zc.reshape(B, C, Z_WIDTH)
        cw, sw = _dft_weights(c128, s128, c_w_f[l])

        ub, ud = _branch_bd(z, zc, b_sink[l], bias_tabs, rps_d, qt_b, l)
        u_lat = [ua, ub, _fourier_branch(z, cw, sw, dft_lat, jm_lat, tm_f), ud]
        h = _merge(h, mod, lat_row, n_lat, u_lat, w_gate_b, w_branch_b, w_out_b, tm, tm_out, cn, l)
        if not last:
            u_ctx = [
                ua_ctx,
                _ctx_b(zc, b_sink[l]),
                _fourier_branch(zc, cw, sw, dft_ctx, jm_ctx, tm_fc).reshape(B, C, W),
                _ctx_d(zc),
            ]
            u_ctx = [u.reshape(1, B * C, W) for u in u_ctx]
            hc = _merge(hc, mod, ctx_row, n_ctx, u_ctx, w_gate_b, w_branch_b, w_out_b, tm_ctx, tm_ctx, cn, l)
    return h
```
